```python
import math
import jax, jax.numpy as jnp
from jax import lax
import numpy as np

D_MODEL = 4096
BATCH = 4
SEQ = 2048
DEPTH = 2
DEC_BATCH = 128
DEC_SEQ = 8
PAST_LEN = 16384
PAGE_SIZE = 128

GLA_HEADS = 4
GLA_DK = D_MODEL // 4
GLA_DV = D_MODEL // 2
GLA_HK = GLA_DK // GLA_HEADS
GLA_HV = GLA_DV // GLA_HEADS
GLA_RANK = 16
GLA_TAU = 16.0
GLA_CHUNK = 64
RMS_EPS = 1e-6

S5_WIDTH = D_MODEL // 2
S5_GROUP = 16
S5_GROUPS = S5_WIDTH // S5_GROUP
S5_STATE = 64
S5_DT_MIN = 1e-3
S5_DT_MAX = 1e-1

PEER_KEYS = 128
PEER_EXPERTS = PEER_KEYS * PEER_KEYS
PEER_HEADS = 8
PEER_TOPK = 16
PEER_QDIM = 256
PEER_HALF = PEER_QDIM // 2
PEER_BLOCK = 128

DN_ALPHA = (2 * DEPTH) ** 0.25
DN_BETA = (8 * DEPTH) ** -0.25
LN_EPS = 1e-5

IN_WIDTHS = (GLA_DK, GLA_DK, GLA_DV, GLA_DV, GLA_RANK, S5_WIDTH, D_MODEL, D_MODEL)
W_IN_COLS = sum(IN_WIDTHS)
IN_OFFSETS = tuple(int(v) for v in np.cumsum(IN_WIDTHS)[:-1])

kernel_name = "gla_s5_peer_deepnorm_decoder_step"


def layer_norm(x, g, b):
    xf = x.astype(jnp.float32)
    mu = jnp.mean(xf, -1, keepdims=True)
    var = jnp.mean(jnp.square(xf - mu), -1, keepdims=True)
    y = (xf - mu) * lax.rsqrt(var + LN_EPS) * g.astype(jnp.float32) + b.astype(jnp.float32)
    return y.astype(x.dtype)


def gla_recurrence(q, k, v, log_a, s0):
    bsz, L, H = q.shape[:3]
    dv = v.shape[-1]
    c = min(GLA_CHUNK, L)
    n = -(-L // c)
    pad = n * c - L

    def prep(t):
        t = jnp.pad(t.astype(jnp.float32), ((0, 0), (0, pad), (0, 0), (0, 0)))
        return t.reshape(bsz, n, c, H, t.shape[-1]).transpose(1, 0, 3, 2, 4)

    qc, kc, vc, ac = prep(q), prep(k), prep(v), prep(log_a)
    mask = jnp.tril(jnp.ones((c, c), dtype=bool))

    def step(s, inp):
        qi, ki, vi, ai = inp
        b = jnp.cumsum(ai, axis=2)
        b_last = b[:, :, -1:, :]
        q_dec = qi * jnp.exp(b)
        k_inv = ki * jnp.exp(-b)
        k_end = ki * jnp.exp(b_last - b)
        scores = jnp.where(mask, jnp.einsum('bhtd,bhsd->bhts', q_dec, k_inv), 0.0)
        o = jnp.einsum('bhtd,bhdv->bhtv', q_dec, s) + jnp.einsum('bhts,bhsv->bhtv', scores, vi)
        s_new = jnp.exp(b_last[:, :, 0, :])[..., None] * s + jnp.einsum('bhsd,bhsv->bhdv', k_end, vi)
        return s_new, o

    s_fin, o = lax.scan(step, s0.astype(jnp.float32), (qc, kc, vc, ac))
    o = o.transpose(1, 0, 3, 2, 4).reshape(bsz, n * c, H, dv)[:, :L]
    return o, s_fin


def gla_branch(q, k, v, r, a_lr, w_a2, b_a, norm_g, s0):
    bsz, L = q.shape[:2]
    log_a = jax.nn.log_sigmoid((a_lr @ w_a2 + b_a).astype(jnp.float32)) / GLA_TAU
    qh = q.reshape(bsz, L, GLA_HEADS, GLA_HK).astype(jnp.float32) * (GLA_HK ** -0.5)
    kh = k.reshape(bsz, L, GLA_HEADS, GLA_HK)
    vh = v.reshape(bsz, L, GLA_HEADS, GLA_HV)
    ah = log_a.reshape(bsz, L, GLA_HEADS, GLA_HK)
    o, s_new = gla_recurrence(qh, kh, vh, ah, s0)
    o = o * lax.rsqrt(jnp.mean(jnp.square(o), -1, keepdims=True) + RMS_EPS) * norm_g.astype(jnp.float32)
    o = o.reshape(bsz, L, GLA_DV).astype(r.dtype) * jax.nn.silu(r)
    return o, s_new


def s5_branch(u, a_re, a_im, log_dt, b_re, b_im, c_re, c_im, d_skip, w_glu, b_glu, s0_re, s0_im):
    bsz, L = u.shape[:2]
    f32 = jnp.float32
    ug = u.reshape(bsz, L, S5_GROUPS, S5_GROUP).astype(f32)
    lam = lax.complex(a_re.astype(f32), a_im.astype(f32))
    dt = jnp.exp(log_dt.astype(f32))[:, None]
    a_bar = jnp.exp(lam * dt)
    b_c = lax.complex(b_re.astype(f32), b_im.astype(f32))
    b_bar = ((a_bar - 1.0) / lam)[..., None] * b_c
    bu = jnp.einsum('gpn,blgn->blgp', b_bar, ug.astype(jnp.complex64))
    a_seq = jnp.broadcast_to(a_bar, bu.shape)

    def combine(e1, e2):
        a1, x1 = e1
        a2, x2 = e2
        return a2 * a1, a2 * x1 + x2

    a_cum, s = lax.associative_scan(combine, (a_seq, bu), axis=1)
    s0 = lax.complex(s0_re.astype(f32), s0_im.astype(f32))
    s = s + a_cum * s0[:, None]
    c_c = lax.complex(c_re.astype(f32), c_im.astype(f32))
    y = jnp.einsum('gnp,blgp->blgn', c_c, s).real + d_skip.astype(f32).reshape(S5_GROUPS, S5_GROUP) * ug
    y = jax.nn.gelu(y.reshape(bsz, L, S5_WIDTH))
    y = y * jax.nn.sigmoid(y @ w_glu.astype(f32) + b_glu.astype(f32))
    s_last = s[:, -1]
    return y.astype(u.dtype), s_last.real, s_last.imag


def mixer_block(x, p, s_gla, s_re, s_im):
    proj = x @ p["w_in"]
    q, k, v, r, a_lr, u, z_gla, z_s5 = jnp.split(proj, IN_OFFSETS, axis=-1)
    o_gla, s_gla_new = gla_branch(q, k, v, r, a_lr, p["gla_w_a2"], p["gla_b_a"], p["gla_norm_g"], s_gla)
    o_s5, s_re_new, s_im_new = s5_branch(u, p["s5_a_re"], p["s5_a_im"], p["s5_log_dt"], p["s5_b_re"],
                                         p["s5_b_im"], p["s5_c_re"], p["s5_c_im"], p["s5_d"],
                                         p["s5_w_glu"], p["s5_b_glu"], s_re, s_im)
    w_br = p["w_branch"]
    merged = (jax.nn.sigmoid(z_gla) * (o_gla @ w_br[:GLA_DV])
              + jax.nn.sigmoid(z_s5) * (o_s5 @ w_br[GLA_DV:]))
    return merged @ p["w_out"], s_gla_new, s_re_new, s_im_new


def peer_ffn(x, w_q, key1, key2, u_tab, v_tab):
    lead = x.shape[:-1]
    xt = x.reshape(-1, D_MODEL)
    T = xt.shape[0]
    f32 = jnp.float32
    q = (xt @ w_q).reshape(T, PEER_HEADS, 2, PEER_HALF).astype(f32)
    s1 = jnp.einsum('thd,kd->thk', q[:, :, 0], key1.astype(f32))
    s2 = jnp.einsum('thd,kd->thk', q[:, :, 1], key2.astype(f32))
    v1, i1 = lax.top_k(s1, PEER_TOPK)
    v2, i2 = lax.top_k(s2, PEER_TOPK)
    cand = (v1[..., :, None] + v2[..., None, :]).reshape(T, PEER_HEADS, PEER_TOPK * PEER_TOPK)
    cand_idx = (i1[..., :, None] * PEER_KEYS + i2[..., None, :]).reshape(T, PEER_HEADS, PEER_TOPK * PEER_TOPK)
    top_s, pos = lax.top_k(cand, PEER_TOPK)
    idx = jnp.take_along_axis(cand_idx, pos, axis=-1)
    gate = jax.nn.softmax(top_s, axis=-1)
    HK = PEER_HEADS * PEER_TOPK
    nb = -(-T // PEER_BLOCK)
    pad = nb * PEER_BLOCK - T
    xb = jnp.pad(xt, ((0, pad), (0, 0))).reshape(nb, PEER_BLOCK, D_MODEL)
    ib = jnp.pad(idx.reshape(T, HK), ((0, pad), (0, 0))).reshape(nb, PEER_BLOCK, HK)
    gb = jnp.pad(gate.reshape(T, HK), ((0, pad), (0, 0))).reshape(nb, PEER_BLOCK, HK)

    def block(args):
        xi, ii, gi = args
        u_sel = u_tab[ii]
        h = jax.nn.gelu(jnp.einsum('td,ted->te', xi, u_sel).astype(f32))
        v_sel = v_tab[ii]
        return jnp.einsum('te,ted->td', (gi * h).astype(v_sel.dtype), v_sel)

    y = lax.map(block, (xb, ib, gb)).reshape(nb * PEER_BLOCK, D_MODEL)[:T]
    return y.reshape(*lead, D_MODEL).astype(x.dtype)


def forward(x, st_gla, st_re, st_im, w):
    new_gla, new_re, new_im = [], [], []
    for l in range(DEPTH):
        p = {name: arr[l] for name, arr in w.items()}
        mix, s_gla, s_re, s_im = mixer_block(x, p, st_gla[l], st_re[l], st_im[l])
        x = layer_norm(DN_ALPHA * x + mix, p["ln1_g"], p["ln1_b"])
        ffn = peer_ffn(x, p["peer_w_q"], p["peer_key1"], p["peer_key2"], p["peer_u"], p["peer_v"])
        x = layer_norm(DN_ALPHA * x + ffn, p["ln2_g"], p["ln2_b"])
        new_gla.append(s_gla)
        new_re.append(s_re)
        new_im.append(s_im)
    return x, jnp.stack(new_gla), jnp.stack(new_re), jnp.stack(new_im)


def setup_inputs(seed: int = 0) -> dict:
    key = jax.random.key(seed)
    ks = jax.random.split(key, 32)
    nrm = lambda k, shape, s: jax.random.normal(k, shape, jnp.float32) * s
    L_ = DEPTH
    inp = {}
    inp["x_prompt"] = nrm(ks[0], (BATCH, SEQ, D_MODEL), 1.0)
    inp["x_sample"] = nrm(ks[1], (DEC_BATCH, DEC_SEQ, D_MODEL), 1.0)
    inp["state_gla"] = nrm(ks[2], (L_, DEC_BATCH, GLA_HEADS, GLA_HK, GLA_HV), 1.0)
    inp["state_s5_re"] = nrm(ks[3], (L_, DEC_BATCH, S5_GROUPS, S5_STATE), 1.0)
    inp["state_s5_im"] = nrm(ks[4], (L_, DEC_BATCH, S5_GROUPS, S5_STATE), 1.0)
    inp["ln1_g"] = 1.0 + nrm(ks[5], (L_, D_MODEL), 0.02)
    inp["ln1_b"] = nrm(ks[6], (L_, D_MODEL), 0.02)
    inp["w_in"] = nrm(ks[7], (L_, D_MODEL, W_IN_COLS), D_MODEL ** -0.5)
    inp["gla_w_a2"] = nrm(ks[8], (L_, GLA_RANK, GLA_DK), GLA_RANK ** -0.5)
    inp["gla_b_a"] = nrm(ks[9], (L_, GLA_DK), 0.1)
    inp["gla_norm_g"] = 1.0 + nrm(ks[10], (L_, GLA_HV), 0.02)
    inp["s5_a_re"] = -0.5 + nrm(ks[11], (L_, S5_GROUPS, S5_STATE), 0.01)
    inp["s5_a_im"] = (jnp.pi * jnp.arange(S5_STATE, dtype=jnp.float32))[None, None, :] + nrm(ks[12], (L_, S5_GROUPS, S5_STATE), 0.01)
    inp["s5_log_dt"] = jax.random.uniform(ks[13], (L_, S5_GROUPS), jnp.float32, math.log(S5_DT_MIN), math.log(S5_DT_MAX))
    inp["s5_b_re"] = nrm(ks[14], (L_, S5_GROUPS, S5_STATE, S5_GROUP), (2 * S5_GROUP) ** -0.5)
    inp["s5_b_im"] = nrm(ks[15], (L_, S5_GROUPS, S5_STATE, S5_GROUP), (2 * S5_GROUP) ** -0.5)
    inp["s5_c_re"] = nrm(ks[16], (L_, S5_GROUPS, S5_GROUP, S5_STATE), (2 * S5_STATE) ** -0.5)
    inp["s5_c_im"] = nrm(ks[17], (L_, S5_GROUPS, S5_GROUP, S5_STATE), (2 * S5_STATE) ** -0.5)
    inp["s5_d"] = nrm(ks[18], (L_, S5_WIDTH), 1.0)
    inp["s5_w_glu"] = nrm(ks[19], (L_, S5_WIDTH, S5_WIDTH), S5_WIDTH ** -0.5)
    inp["s5_b_glu"] = nrm(ks[20], (L_, S5_WIDTH), 0.01)
    inp["w_branch"] = nrm(ks[21], (L_, GLA_DV + S5_WIDTH, D_MODEL), GLA_DV ** -0.5)
    inp["w_out"] = nrm(ks[22], (L_, D_MODEL, D_MODEL), DN_BETA * D_MODEL ** -0.5)
    inp["ln2_g"] = 1.0 + nrm(ks[23], (L_, D_MODEL), 0.02)
    inp["ln2_b"] = nrm(ks[24], (L_, D_MODEL), 0.02)
    inp["peer_w_q"] = nrm(ks[25], (L_, D_MODEL, PEER_HEADS * PEER_QDIM), D_MODEL ** -0.5)
    inp["peer_key1"] = nrm(ks[26], (L_, PEER_KEYS, PEER_HALF), PEER_HALF ** -0.5)
    inp["peer_key2"] = nrm(ks[27], (L_, PEER_KEYS, PEER_HALF), PEER_HALF ** -0.5)
    inp["peer_u"] = nrm(ks[28], (L_, PEER_EXPERTS, D_MODEL), D_MODEL ** -0.5)
    inp["peer_v"] = nrm(ks[29], (L_, PEER_EXPERTS, D_MODEL), DN_BETA * PEER_HEADS ** -0.5)
    return inp


def reference(x_prompt, x_sample, state_gla, state_s5_re, state_s5_im,
              ln1_g, ln1_b, w_in, gla_w_a2, gla_b_a, gla_norm_g,
              s5_a_re, s5_a_im, s5_log_dt, s5_b_re, s5_b_im, s5_c_re, s5_c_im,
              s5_d, s5_w_glu, s5_b_glu, w_branch, w_out, ln2_g, ln2_b,
              peer_w_q, peer_key1, peer_key2, peer_u, peer_v):
    w = dict(ln1_g=ln1_g, ln1_b=ln1_b, w_in=w_in, gla_w_a2=gla_w_a2, gla_b_a=gla_b_a,
             gla_norm_g=gla_norm_g, s5_a_re=s5_a_re, s5_a_im=s5_a_im, s5_log_dt=s5_log_dt,
             s5_b_re=s5_b_re, s5_b_im=s5_b_im, s5_c_re=s5_c_re, s5_c_im=s5_c_im, s5_d=s5_d,
             s5_w_glu=s5_w_glu, s5_b_glu=s5_b_glu, w_branch=w_branch, w_out=w_out,
             ln2_g=ln2_g, ln2_b=ln2_b, peer_w_q=peer_w_q, peer_key1=peer_key1,
             peer_key2=peer_key2, peer_u=peer_u, peer_v=peer_v)
    pb = x_prompt.shape[0]
    zero_gla = jnp.zeros((DEPTH, pb, GLA_HEADS, GLA_HK, GLA_HV), jnp.float32)
    zero_s5 = jnp.zeros((DEPTH, pb, S5_GROUPS, S5_STATE), jnp.float32)
    y_prompt, gla_p, re_p, im_p = forward(x_prompt, zero_gla, zero_s5, zero_s5, w)
    y_sample, gla_s, re_s, im_s = forward(x_sample, state_gla, state_s5_re, state_s5_im, w)
    return (y_prompt, y_sample, gla_p, re_p, im_p, gla_s, re_s, im_s)
```

```python
import functools
import math

import jax
import jax.numpy as jnp
from jax import lax
from jax.experimental import pallas as pl
from jax.experimental.pallas import tpu as pltpu

F32 = jnp.float32
BF16 = jnp.bfloat16
HIGHEST = lax.Precision.HIGHEST

V7X_VMEM_BYTES = 64 * 1024 * 1024
VMEM_LIMIT_BYTES = V7X_VMEM_BYTES - 8 * 1024 * 1024
LANES = 128

GLA_HEADS = 4
GLA_RANK = 16
GLA_TAU = 16.0
GLA_CHUNK = 64
RMS_EPS = 1e-6
S5_GROUP = 16
S5_STATE = 64
S5_CHUNK = 8
PEER_KEYS = 128
PEER_HEADS = 8
PEER_TOPK = 16
LN_EPS = 1e-5


def _params(*semantics):
    return pltpu.CompilerParams(dimension_semantics=semantics, vmem_limit_bytes=VMEM_LIMIT_BYTES)


def _dot(a, b):
    return jnp.dot(a, b, preferred_element_type=F32)


def _dot_nt(a, b):
    return lax.dot_general(a, b, (((1,), (1,)), ((), ())), preferred_element_type=F32)


def _mm_kernel(a_ref, b_ref, o_ref):
    o_ref[...] = _dot(a_ref[...], b_ref[...]).astype(o_ref.dtype)


def _mm(a, b, tm, tn, out_dtype=F32):
    m, k = a.shape
    n = b.shape[1]
    return pl.pallas_call(
        _mm_kernel,
        grid=(m // tm, n // tn),
        in_specs=[pl.BlockSpec((tm, k), lambda i, j: (i, 0)),
                  pl.BlockSpec((k, tn), lambda i, j: (0, j))],
        out_specs=pl.BlockSpec((tm, tn), lambda i, j: (i, j)),
        out_shape=jax.ShapeDtypeStruct((m, n), out_dtype),
        compiler_params=_params("parallel", "parallel"),
        name="mm",
    )(a, b)


def _glu_kernel(ybf_ref, w_ref, b_ref, y_ref, o_ref):
    acc = _dot(ybf_ref[...], w_ref[...]) + b_ref[...]
    o_ref[...] = (y_ref[...] * jax.nn.sigmoid(acc)).astype(o_ref.dtype)


def _glu(y, y_bf, w, b, tm, tn):
    m, k = y_bf.shape
    n = w.shape[1]
    return pl.pallas_call(
        _glu_kernel,
        grid=(m // tm, n // tn),
        in_specs=[pl.BlockSpec((tm, k), lambda i, j: (i, 0)),
                  pl.BlockSpec((k, tn), lambda i, j: (0, j)),
                  pl.BlockSpec((1, tn), lambda i, j: (0, j)),
                  pl.BlockSpec((tm, tn), lambda i, j: (i, j))],
        out_specs=pl.BlockSpec((tm, tn), lambda i, j: (i, j)),
        out_shape=jax.ShapeDtypeStruct((m, n), BF16),
        compiler_params=_params("parallel", "parallel"),
        name="s5_glu",
    )(y_bf, w, b.reshape(1, n), y)


def _branch_kernel(og_ref, os_ref, w1_ref, w2_ref, zg_ref, zs_ref, o_ref):
    m1 = _dot(og_ref[...], w1_ref[...])
    m2 = _dot(os_ref[...], w2_ref[...])
    o_ref[...] = (jax.nn.sigmoid(zg_ref[...]) * m1 + jax.nn.sigmoid(zs_ref[...]) * m2).astype(o_ref.dtype)


def _branch_merge(o_gla, o_s5, w1, w2, proj, zg_col, zs_col, tm, tn):
    m, k1 = o_gla.shape
    k2 = o_s5.shape[1]
    n = w1.shape[1]
    zg_blk, zs_blk = zg_col // tn, zs_col // tn
    return pl.pallas_call(
        _branch_kernel,
        grid=(m // tm, n // tn),
        in_specs=[pl.BlockSpec((tm, k1), lambda i, j: (i, 0)),
                  pl.BlockSpec((tm, k2), lambda i, j: (i, 0)),
                  pl.BlockSpec((k1, tn), lambda i, j: (0, j)),
                  pl.BlockSpec((k2, tn), lambda i, j: (0, j)),
                  pl.BlockSpec((tm, tn), lambda i, j: (i, zg_blk + j)),
                  pl.BlockSpec((tm, tn), lambda i, j: (i, zs_blk + j))],
        out_specs=pl.BlockSpec((tm, tn), lambda i, j: (i, j)),
        out_shape=jax.ShapeDtypeStruct((m, n), BF16),
        compiler_params=_params("parallel", "parallel"),
        name="branch_merge",
    )(o_gla, o_s5, w1, w2, proj, proj)


def _ln_kernel(x_ref, y_ref, g_ref, b_ref, o_ref, obf_ref, *, alpha):
    h = alpha * x_ref[...] + y_ref[...]
    mu = jnp.mean(h, axis=-1, keepdims=True)
    d = h - mu
    var = jnp.mean(d * d, axis=-1, keepdims=True)
    o = d * lax.rsqrt(var + LN_EPS) * g_ref[...] + b_ref[...]
    o_ref[...] = o
    obf_ref[...] = o.astype(BF16)


def _deepnorm_ln(x, y, g, b, alpha, tm):
    m, d = x.shape
    row = pl.BlockSpec((tm, d), lambda i: (i, 0))
    vec = pl.BlockSpec((1, d), lambda i: (0, 0))
    return pl.pallas_call(
        functools.partial(_ln_kernel, alpha=alpha),
        grid=(m // tm,),
        in_specs=[row, row, vec, vec],
        out_specs=[row, row],
        out_shape=[jax.ShapeDtypeStruct((m, d), F32), jax.ShapeDtypeStruct((m, d), BF16)],
        compiler_params=_params("parallel"),
        name="deepnorm_ln",
    )(x, y, g.reshape(1, d), b.reshape(1, d))


def _gla_kernel(*refs, c, nseq, scale, has_init):
    if has_init:
        q_ref, k_ref, v_ref, r_ref, alr_ref, wa2_ref, ba_ref, g_ref, s0_ref, o_ref, s_ref = refs
    else:
        q_ref, k_ref, v_ref, r_ref, alr_ref, wa2_ref, ba_ref, g_ref, o_ref, s_ref = refs
        s0_ref = None
    dk = q_ref.shape[-1]
    kpad = max(c, LANES)
    qpad = max(c, 16)

    @pl.when(pl.program_id(2) == 0)
    def _():
        if has_init:
            s_ref[...] = s0_ref[...]
        else:
            s_ref[...] = jnp.zeros_like(s_ref)

    row = lax.broadcasted_iota(jnp.int32, (c, c), 0)
    col = lax.broadcasted_iota(jnp.int32, (c, c), 1)
    tri_cc = (row >= col).astype(F32)
    rowp = lax.broadcasted_iota(jnp.int32, (qpad, kpad), 0)
    colp = lax.broadcasted_iota(jnp.int32, (qpad, kpad), 1)
    causal = rowp >= colp
    eye = lax.broadcasted_iota(jnp.int32, (dk, dk), 0) == lax.broadcasted_iota(jnp.int32, (dk, dk), 1)
    eye_bf = eye.astype(BF16)

    def pad_rows(x, rows):
        if x.shape[0] == rows:
            return x
        return jnp.concatenate([x, jnp.zeros((rows - x.shape[0], x.shape[1]), x.dtype)], axis=0)

    outs = []
    for s in range(nseq):
        rows = slice(s * c, (s + 1) * c)
        q = q_ref[rows, :]
        k = k_ref[rows, :]
        v = v_ref[rows, :]
        z = jnp.dot(alr_ref[rows, :], wa2_ref[...], precision=HIGHEST, preferred_element_type=F32) + ba_ref[...]
        log_a = (jnp.minimum(z, 0.0) - jnp.log1p(jnp.exp(-jnp.abs(z)))) * (1.0 / GLA_TAU)
        if c >= LANES // 2:
            b = jnp.dot(tri_cc, log_a, precision=HIGHEST, preferred_element_type=F32)
        else:
            ridx = lax.broadcasted_iota(jnp.int32, (c, dk), 0)
            b = jnp.zeros((c, dk), F32)
            for t in range(c):
                b = b + jnp.where(ridx >= t, log_a[t:t + 1, :], 0.0)
        b_last = b[c - 1:c, :]
        q_dec = pad_rows((q * scale * jnp.exp(b)).astype(BF16), qpad)
        k_inv = pad_rows((k * jnp.exp(-b)).astype(BF16), kpad)
        k_end = pad_rows((k * jnp.exp(b_last - b)).astype(BF16), kpad)
        v_bf = pad_rows(v.astype(BF16), kpad)
        scores = jnp.where(causal, _dot_nt(q_dec, k_inv), 0.0).astype(BF16)
        state = s_ref[s, 0]
        o = (_dot(q_dec, state.astype(BF16)) + _dot(scores, v_bf))[:c]
        k_end_t = _dot_nt(eye_bf, k_end).astype(BF16)
        decay_col = jnp.exp(jnp.sum(jnp.where(eye, b_last, 0.0), axis=1, keepdims=True))
        s_ref[s, 0] = decay_col * state + _dot(k_end_t, v_bf)
        o = o * lax.rsqrt(jnp.mean(o * o, axis=-1, keepdims=True) + RMS_EPS) * g_ref[...]
        r = r_ref[rows, :]
        outs.append(o * (r * jax.nn.sigmoid(r)))
    o_all = outs[0] if nseq == 1 else jnp.concatenate(outs, axis=0)
    o_ref[...] = o_all.astype(o_ref.dtype)


def _gla(proj, alr, wa2, ba, norm_g, s0, *, row0, nb, seq_len, c, nseq, cols):
    heads = GLA_HEADS
    dk = wa2.shape[1] // heads
    dv = norm_g.shape[0]
    nchunk = seq_len // c
    assert nseq == 1 or nchunk == 1
    rb = nseq * c
    blk0 = row0 // rb
    q_col, k_col, v_col, r_col = cols

    def rows_map(col_blk):
        return lambda b, h, j: (blk0 + b * nchunk + j, col_blk + h)

    in_specs = [
        pl.BlockSpec((rb, dk), rows_map(q_col // dk)),
        pl.BlockSpec((rb, dk), rows_map(k_col // dk)),
        pl.BlockSpec((rb, dv), rows_map(v_col // dv)),
        pl.BlockSpec((rb, dv), rows_map(r_col // dv)),
        pl.BlockSpec((rb, LANES), lambda b, h, j: (blk0 + b * nchunk + j, 0)),
        pl.BlockSpec((LANES, dk), lambda b, h, j: (0, h)),
        pl.BlockSpec((1, dk), lambda b, h, j: (0, h)),
        pl.BlockSpec((1, dv), lambda b, h, j: (0, 0)),
    ]
    args = [proj, proj, proj, proj, alr, wa2, ba.reshape(1, -1), norm_g.reshape(1, dv)]
    state_spec = pl.BlockSpec((nseq, 1, dk, dv), lambda b, h, j: (b, h, 0, 0))
    if s0 is not None:
        in_specs.append(state_spec)
        args.append(s0)
    kern = functools.partial(_gla_kernel, c=c, nseq=nseq, scale=dk ** -0.5, has_init=s0 is not None)
    return pl.pallas_call(
        kern,
        grid=(nb // nseq, heads, nchunk),
        in_specs=in_specs,
        out_specs=[pl.BlockSpec((rb, dv), lambda b, h, j: (b * nchunk + j, h)), state_spec],
        out_shape=[jax.ShapeDtypeStruct((nb * seq_len, heads * dv), BF16),
                   jax.ShapeDtypeStruct((nb, heads, dk, dv), F32)],
        compiler_params=_params("parallel", "parallel", "arbitrary"),
        name="gla",
    )(*args)


def _s5_tables(a_re, a_im, log_dt, b_re, b_im, c_re, c_im, c):
    dt = jnp.exp(log_dt)[:, None]
    mag = jnp.exp(a_re * dt)
    ar, ai = mag * jnp.cos(a_im * dt), mag * jnp.sin(a_im * dt)
    den = a_re * a_re + a_im * a_im
    cr = ((ar - 1.0) * a_re + ai * a_im) / den
    ci = (ai * a_re - (ar - 1.0) * a_im) / den
    bbr = cr[..., None] * b_re - ci[..., None] * b_im
    bbi = cr[..., None] * b_im + ci[..., None] * b_re
    tau = jnp.arange(c + 1, dtype=F32)[:, None, None]
    pmag = jnp.exp(tau * (a_re * dt))
    pw_r, pw_i = pmag * jnp.cos(tau * (a_im * dt)), pmag * jnp.sin(tau * (a_im * dt))
    ca_r = c_re[None] * pw_r[:, :, None, :] - c_im[None] * pw_i[:, :, None, :]
    ca_i = c_re[None] * pw_i[:, :, None, :] + c_im[None] * pw_r[:, :, None, :]
    kern = (jnp.einsum('tgmp,gpn->tgmn', ca_r[:c], bbr, precision=HIGHEST)
            - jnp.einsum('tgmp,gpn->tgmn', ca_i[:c], bbi, precision=HIGHEST))
    g, m, n = kern.shape[1:]
    kern_z = jnp.concatenate([kern, jnp.zeros((1, g, m, n), F32)], axis=0)
    s_idx = jnp.arange(c)[:, None]
    t_idx = jnp.arange(c)[None, :]
    lag = jnp.where(t_idx >= s_idx, t_idx - s_idx, c)
    toep = kern_z[lag]
    toep = toep.transpose(2, 0, 4, 1, 3).reshape(g, c * n, c * m)
    rev_r, rev_i = pw_r[c - 1::-1][:c], pw_i[c - 1::-1][:c]
    inj_r = rev_r[..., None] * bbr[None] - rev_i[..., None] * bbi[None]
    inj_i = rev_r[..., None] * bbi[None] + rev_i[..., None] * bbr[None]
    inj = jnp.concatenate([inj_r, inj_i], axis=2).transpose(1, 0, 3, 2).reshape(g, c * n, -1)
    read = jnp.concatenate([ca_r[1:], -ca_i[1:]], axis=3)
    read = read.transpose(1, 3, 0, 2).reshape(g, -1, c * m)
    step_same = jnp.concatenate([pw_r[c], pw_r[c]], axis=-1)
    step_swap = jnp.concatenate([-pw_i[c], pw_i[c]], axis=-1)
    return toep, inj, read, step_same, step_swap


def _s5_in_kernel(u_ref, w_ref, yi_ref, e_ref):
    half = yi_ref.shape[-1]
    for g in range(u_ref.shape[0]):
        res = _dot(u_ref[g].astype(BF16), w_ref[g])
        yi_ref[g] = res[:, :half]
        e_ref[g] = res[:, half:]


def _s5_in(uc, w, gb):
    g, r, k = uc.shape
    n_out = w.shape[2] - k
    return pl.pallas_call(
        _s5_in_kernel,
        grid=(g // gb,),
        in_specs=[pl.BlockSpec((gb, r, k), lambda i: (i, 0, 0)),
                  pl.BlockSpec((gb, k, w.shape[2]), lambda i: (i, 0, 0))],
        out_specs=[pl.BlockSpec((gb, r, k), lambda i: (i, 0, 0)),
                   pl.BlockSpec((gb, r, n_out), lambda i: (i, 0, 0))],
        out_shape=[jax.ShapeDtypeStruct((g, r, k), F32), jax.ShapeDtypeStruct((g, r, n_out), F32)],
        compiler_params=_params("parallel"),
        name="s5_in",
    )(uc, w)


def _s5_scan_kernel(e_ref, x0_ref, same_ref, swap_ref, xprev_ref, xfin_ref):
    half = x0_ref.shape[-1] // 2
    same = same_ref[...]
    swap = swap_ref[...]

    def body(j, x):
        xprev_ref[j] = x
        return same * x + swap * pltpu.roll(x, half, 1) + e_ref[j]

    xfin_ref[...] = lax.fori_loop(0, e_ref.shape[0], body, x0_ref[...])


def _s5_scan(e, x0, same, swap, cb):
    nj, chains, w = e.shape
    cb = min(cb, chains)
    seq =pl.BlockSpec((nj, cb, w), lambda i: (0, i, 0))
    vec = pl.BlockSpec((cb, w), lambda i: (i, 0))
    return pl.pallas_call(
        _s5_scan_kernel,
        grid=(chains // cb,),
        in_specs=[seq, vec, vec, vec],
        out_specs=[seq, vec],
        out_shape=[jax.ShapeDtypeStruct((nj, chains, w), F32), jax.ShapeDtypeStruct((chains, w), F32)],
        compiler_params=_params("parallel"),
        name="s5_scan",
    )(e, x0, same, swap)


def _s5_out_kernel(yi_ref, xp_ref, rd_ref, u_ref, d_ref, y_ref):
    for g in range(yi_ref.shape[0]):
        y = yi_ref[g] + _dot(xp_ref[g].astype(BF16), rd_ref[g]) + d_ref[g] * u_ref[g]
        y_ref[g] = jax.nn.gelu(y)


def _s5_out(yi, xprev, read, uc, d_row, gb):
    g, r, k = yi.shape
    w = xprev.shape[2]
    return pl.pallas_call(
        _s5_out_kernel,
        grid=(g // gb,),
        in_specs=[pl.BlockSpec((gb, r, k), lambda i: (i, 0, 0)),
                  pl.BlockSpec((gb, r, w), lambda i: (i, 0, 0)),
                  pl.BlockSpec((gb, w, k), lambda i: (i, 0, 0)),
                  pl.BlockSpec((gb, r, k), lambda i: (i, 0, 0)),
                  pl.BlockSpec((gb, 1, k), lambda i: (i, 0, 0))],
        out_specs=pl.BlockSpec((gb, r, k), lambda i: (i, 0, 0)),
        out_shape=jax.ShapeDtypeStruct((g, r, k), F32),
        compiler_params=_params("parallel"),
        name="s5_out",
    )(yi, xprev, read, uc, d_row)


def _s5_mixer(u, tables, d_skip, st_re, st_im, n_prompt_seq, prompt_len, n_sample_seq):
    toep_inj, read, step_same, step_swap = tables
    t_tokens, width = u.shape
    c, n = S5_CHUNK, S5_GROUP
    groups = width // n
    sw = step_same.shape[-1]
    r_all = t_tokens // c
    jp = prompt_len // c
    rp = n_prompt_seq * jp
    uc = u.reshape(r_all, c, groups, n).transpose(2, 0, 1, 3).reshape(groups, r_all, c * n)
    yi, e = _s5_in(uc, toep_inj, 8)
    e_p = e[:, :rp].reshape(groups, n_prompt_seq, jp, sw).transpose(2, 0, 1, 3).reshape(jp, groups * n_prompt_seq, sw)
    zeros_p = jnp.zeros((groups * n_prompt_seq, sw), F32)
    xprev_p, xfin_p = _s5_scan(e_p, zeros_p, jnp.repeat(step_same, n_prompt_seq, axis=0),
                               jnp.repeat(step_swap, n_prompt_seq, axis=0), 64)
    xprev_p = xprev_p.reshape(jp, groups, n_prompt_seq, sw).transpose(1, 2, 0, 3).reshape(groups, rp, sw)
    x0_s = jnp.concatenate([st_re, st_im], axis=-1).transpose(1, 0, 2)
    e_s = e[:, rp:].reshape(1, groups * n_sample_seq, sw)
    _, xfin_s = _s5_scan(e_s, x0_s.reshape(groups * n_sample_seq, sw),
                         jnp.repeat(step_same, n_sample_seq, axis=0),
                         jnp.repeat(step_swap, n_sample_seq, axis=0), 2048)
    xprev = jnp.concatenate([xprev_p, x0_s], axis=1)
    d_row = jnp.tile(d_skip.reshape(groups, 1, n), (1, c, 1)).reshape(groups, 1, c * n)
    y = _s5_out(yi, xprev, read, uc, d_row, 8)
    y = y.reshape(groups, r_all, c, n).transpose(1, 2, 0, 3).reshape(t_tokens, width)
    half = sw // 2
    xfin_p = xfin_p.reshape(groups, n_prompt_seq, sw).transpose(1, 0, 2)
    xfin_s = xfin_s.reshape(groups, n_sample_seq, sw).transpose(1, 0, 2)
    return y, (xfin_p[..., :half], xfin_p[..., half:]), (xfin_s[..., :half], xfin_s[..., half:])


def _peer_route_kernel(q_ref, k1_ref, k2_ref, s1_ref, w1_ref, s2_ref, e2_ref, tau_ref):
    half = k1_ref.shape[1]
    neg = jnp.float32(-jnp.inf)

    def top_values(x):
        vals = []
        for _ in range(PEER_TOPK):
            m = jnp.max(x, axis=0, keepdims=True)
            vals.append(m)
            x = jnp.where(x == m, neg, x)
        return vals

    for h in range(PEER_HEADS):
        q1 = q_ref[:, (2 * h) * half:(2 * h + 1) * half]
        q2 = q_ref[:, (2 * h + 1) * half:(2 * h + 2) * half]
        s1 = lax.dot_general(k1_ref[...], q1, (((1,), (1,)), ((), ())), precision=HIGHEST,
                             preferred_element_type=F32)
        s2 = lax.dot_general(k2_ref[...], q2, (((1,), (1,)), ((), ())), precision=HIGHEST,
                             preferred_element_type=F32)
        a = top_values(s1)
        b = jnp.concatenate(top_values(s2), axis=0)
        b0 = b[0:1]
        cand = jnp.concatenate([ai + b for ai in a], axis=0)
        top = top_values(cand)
        zsum = top[0] * 0.0
        for m in top:
            zsum = zsum + jnp.exp(m - top[0])
        s1_ref[h] = s1
        s2_ref[h] = s2
        w1_ref[h] = jnp.exp(s1 - a[0]) / zsum
        e2_ref[h] = jnp.exp(s2 - b0)
        tau_ref[h] = top[-1]


def _peer_route(q, key1, key2, tm):
    t_tokens = q.shape[0]
    nk, half = key1.shape
    big = pl.BlockSpec((PEER_HEADS, nk, tm), lambda i: (0, 0, i))
    big_shape = jax.ShapeDtypeStruct((PEER_HEADS, nk, t_tokens), F32)
    return pl.pallas_call(
        _peer_route_kernel,
        grid=(t_tokens // tm,),
        in_specs=[pl.BlockSpec((tm, q.shape[1]), lambda i: (i, 0)),
                  pl.BlockSpec((nk, half), lambda i: (0, 0)),
                  pl.BlockSpec((nk, half), lambda i: (0, 0))],
        out_specs=[big, big, big, big, pl.BlockSpec((PEER_HEADS, 1, tm), lambda i: (0, 0, i))],
        out_shape=[big_shape, big_shape, big_shape, big_shape,
                   jax.ShapeDtypeStruct((PEER_HEADS, 1, t_tokens), F32)],
        compiler_params=_params("parallel"),
        name="peer_route",
    )(q, key1, key2)


def _peer_ffn_kernel(x_ref, u_ref, v_ref, s1_ref, w1_ref, s2_ref, e2_ref, tau_ref, o_ref, *, ni):
    et = pl.program_id(1)
    nk = s2_ref.shape[1]

    @pl.when(et == 0)
    def _():
        o_ref[...] = jnp.zeros_like(o_ref)

    h_t = _dot_nt(u_ref[...], x_ref[...])
    gates = []
    for ii in range(ni):
        i = et * ni + ii
        g = None
        for h in range(PEER_HEADS):
            s1_row = s1_ref[h, pl.ds(i, 1), :]
            w1_row = w1_ref[h, pl.ds(i, 1), :]
            hit = (s1_row + s2_ref[h]) >= tau_ref[h]
            term = jnp.where(hit, e2_ref[h] * w1_row, 0.0)
            g = term if g is None else g + term
        gates.append(g)
    gate = gates[0] if ni == 1 else jnp.concatenate(gates, axis=0)
    p_t = (gate * jax.nn.gelu(h_t)).astype(BF16)
    o_ref[...] += lax.dot_general(p_t, v_ref[...], (((0,), (0,)), ((), ())), preferred_element_type=F32)


def _peer_ffn(x_bf, u_tab, v_tab, route, tm, te):
    s1, w1, s2, e2, tau = route
    t_tokens, d = x_bf.shape
    n_exp = u_tab.shape[0]
    nk = s2.shape[1]
    ni = te // nk
    once = pl.Buffered(1)
    tok = pl.BlockSpec((PEER_HEADS, nk, tm), lambda i, j: (0, 0, i), pipeline_mode=once)
    return pl.pallas_call(
        functools.partial(_peer_ffn_kernel, ni=ni),
        grid=(t_tokens // tm, n_exp // te),
        in_specs=[pl.BlockSpec((tm, d), lambda i, j: (i, 0), pipeline_mode=once),
                  pl.BlockSpec((te, d), lambda i, j: (j, 0)),
                  pl.BlockSpec((te, d), lambda i, j: (j, 0)),
                  tok, tok, tok, tok,
                  pl.BlockSpec((PEER_HEADS, 1, tm), lambda i, j: (0, 0, i), pipeline_mode=once)],
        out_specs=pl.BlockSpec((tm, d), lambda i, j: (i, 0)),
        out_shape=jax.ShapeDtypeStruct((t_tokens, d), F32),
        compiler_params=_params("parallel", "arbitrary"),
        name="peer_ffn",
    )(x_bf, u_tab, v_tab, s1, w1, s2, e2, tau)


def kernel(x_prompt, x_sample, state_gla, state_s5_re, state_s5_im, ln1_g, ln1_b, w_in, gla_w_a2, gla_b_a, gla_norm_g, s5_a_re, s5_a_im, s5_log_dt, s5_b_re, s5_b_im, s5_c_re, s5_c_im, s5_d, s5_w_glu, s5_b_glu, w_branch, w_out, ln2_g, ln2_b, peer_w_q, peer_key1, peer_key2, peer_u, peer_v):
    depth = w_in.shape[0]
    pb, plen, d = x_prompt.shape
    sb, slen, _ = x_sample.shape
    tp, ts = pb * plen, sb * slen
    t_all = tp + ts
    dk_tot = gla_w_a2.shape[2]
    dv_tot = GLA_HEADS * gla_norm_g.shape[1]
    s5_w = s5_d.shape[1]
    alpha = (2 * depth) ** 0.25
    widths = (dk_tot, dk_tot, dv_tot, dv_tot, GLA_RANK, s5_w, d, d)
    off = [0]
    for wdt in widths:
        off.append(off[-1] + wdt)
    lr0, lr1 = off[4], off[5]
    q_col, k_col, v_col, r_col = off[0], off[1], off[2], off[3]
    u_col = lr0
    zg_col = u_col + s5_w
    zs_col = zg_col + d

    x = jnp.concatenate([x_prompt.reshape(tp, d), x_sample.reshape(ts, d)], axis=0)
    x_bf = x.astype(BF16)
    gla_p, gla_s, re_p, im_p, re_s, im_s = [], [], [], [], [], []
    for l in range(depth):
        w_main = jnp.concatenate([w_in[l, :, :lr0], w_in[l, :, lr1:]], axis=1).astype(BF16)
        w_lr = jnp.pad(w_in[l, :, lr0:lr1], ((0, 0), (0, LANES - GLA_RANK))).astype(BF16)
        wa2 = jnp.pad(gla_w_a2[l], ((0, LANES - GLA_RANK), (0, 0)))
        proj = _mm(x_bf, w_main, 1024, 1024)
        alr = _mm(x_bf, w_lr, 1024, LANES)

        cols = (q_col, k_col, v_col, r_col)
        o_gla, s_p = _gla(proj, alr, wa2, gla_b_a[l], gla_norm_g[l], None, row0=0, nb=pb, seq_len=plen,
                          c=min(GLA_CHUNK, plen), nseq=1, cols=cols)
        o_gla_s, s_s = _gla(proj, alr, wa2, gla_b_a[l], gla_norm_g[l], state_gla[l], row0=tp, nb=sb,
                            seq_len=slen, c=min(GLA_CHUNK, slen), nseq=2, cols=cols)
        o_gla = jnp.concatenate([o_gla, o_gla_s], axis=0)

        toep, inj, read, step_same, step_swap = _s5_tables(
            s5_a_re[l], s5_a_im[l], s5_log_dt[l], s5_b_re[l], s5_b_im[l], s5_c_re[l], s5_c_im[l], S5_CHUNK)
        tables = (jnp.concatenate([toep, inj], axis=2).astype(BF16), read.astype(BF16), step_same, step_swap)
        y_s5, (r_p, i_p), (r_s, i_s) = _s5_mixer(proj[:, u_col:u_col + s5_w], tables, s5_d[l],
                                                 state_s5_re[l], state_s5_im[l], pb, plen, sb)
        o_s5 = _glu(y_s5, y_s5.astype(BF16), s5_w_glu[l].astype(BF16), s5_b_glu[l], 1024, 1024)

        w_br = w_branch[l].astype(BF16)
        merged = _branch_merge(o_gla, o_s5, w_br[:dv_tot], w_br[dv_tot:], proj, zg_col, zs_col, 1024, 512)
        mix = _mm(merged, w_out[l].astype(BF16), 1024, 1024)
        x, x_bf = _deepnorm_ln(x, mix, ln1_g[l], ln1_b[l], alpha, 256)

        q = _mm(x_bf, peer_w_q[l].astype(BF16), 1024, 1024)
        route = _peer_route(q, peer_key1[l], peer_key2[l], 256)
        ffn = _peer_ffn(x_bf, peer_u[l].astype(BF16), peer_v[l].astype(BF16), route, 512, 512)
        x, x_bf = _deepnorm_ln(x, ffn, ln2_g[l], ln2_b[l], alpha, 256)

        gla_p.append(s_p)
        gla_s.append(s_s)
        re_p.append(r_p)
        im_p.append(i_p)
        re_s.append(r_s)
        im_s.append(i_s)

    y_prompt = x[:tp].reshape(pb, plen, d)
    y_sample = x[tp:].reshape(sb, slen, d)
    return (y_prompt, y_sample, jnp.stack(gla_p), jnp.stack(re_p), jnp.stack(im_p),
            jnp.stack(gla_s), jnp.stack(re_s), jnp.stack(im_s))
```

```python
import functools
import math

import jax
import jax.numpy as jnp
from jax import lax
from jax.experimental import pallas as pl
from jax.experimental.pallas import tpu as pltpu

F32 = jnp.float32
BF16 = jnp.bfloat16
HIGHEST = lax.Precision.HIGHEST

V7X_VMEM_BYTES = 64 * 1024 * 1024
VMEM_LIMIT_BYTES = V7X_VMEM_BYTES - 8 * 1024 * 1024
LANES = 128

GLA_HEADS = 4
GLA_RANK = 16
GLA_TAU = 16.0
GLA_CHUNK = 64
RMS_EPS = 1e-6
S5_GROUP = 16
S5_STATE = 64
S5_CHUNK = 8
PEER_KEYS = 128
PEER_HEADS = 8
PEER_TOPK = 16
LN_EPS = 1e-5


def _params(*semantics, flags=None):
    return pltpu.CompilerParams(dimension_semantics=semantics, vmem_limit_bytes=VMEM_LIMIT_BYTES, flags=flags)


def _dot(a, b):
    return jnp.dot(a, b, preferred_element_type=F32)


def _dot_nt(a, b):
    return lax.dot_general(a, b, (((1,), (1,)), ((), ())), preferred_element_type=F32)


def _mm_kernel(a_ref, b_ref, o_ref):
    o_ref[...] = _dot(a_ref[...], b_ref[...]).astype(o_ref.dtype)


def _layer_cols(w, layer, tn, row0=0):
    def spec(rows):
        return pl.BlockSpec((None, rows, tn), lambda i, j: (layer, row0, j))
    return spec


def _mm(a, w, layer, tm, tn, out_dtype=F32):
    m, k = a.shape
    n = w.shape[2]
    return pl.pallas_call(
        _mm_kernel,
        grid=(m // tm, n // tn),
        in_specs=[pl.BlockSpec((tm, k), lambda i, j: (i, 0)),
                  _layer_cols(w, layer, tn)(k)],
        out_specs=pl.BlockSpec((tm, tn), lambda i, j: (i, j)),
        out_shape=jax.ShapeDtypeStruct((m, n), out_dtype),
        compiler_params=_params("parallel", "parallel"),
        name="mm",
    )(a, w)


def _glu_kernel(ybf_ref, w_ref, b_ref, y_ref, o_ref):
    acc = _dot(ybf_ref[...], w_ref[...]) + b_ref[...]
    o_ref[...] = (y_ref[...] * jax.nn.sigmoid(acc)).astype(o_ref.dtype)


def _glu(y, y_bf, w, layer, b, tm, tn):
    m, k = y_bf.shape
    n = w.shape[2]
    return pl.pallas_call(
        _glu_kernel,
        grid=(m // tm, n // tn),
        in_specs=[pl.BlockSpec((tm, k), lambda i, j: (i, 0)),
                  _layer_cols(w, layer, tn)(k),
                  pl.BlockSpec((1, tn), lambda i, j: (0, j)),
                  pl.BlockSpec((tm, tn), lambda i, j: (i, j))],
        out_specs=pl.BlockSpec((tm, tn), lambda i, j: (i, j)),
        out_shape=jax.ShapeDtypeStruct((m, n), BF16),
        compiler_params=_params("parallel", "parallel"),
        name="s5_glu",
    )(y_bf, w, b.reshape(1, n), y)


def _branch_kernel(og_ref, os_ref, w1_ref, w2_ref, zg_ref, zs_ref, o_ref):
    m1 = _dot(og_ref[...], w1_ref[...])
    m2 = _dot(os_ref[...], w2_ref[...])
    o_ref[...] = (jax.nn.sigmoid(zg_ref[...]) * m1 + jax.nn.sigmoid(zs_ref[...]) * m2).astype(o_ref.dtype)


def _branch_merge(o_gla, o_s5, w, layer, proj, zg_col, zs_col, tm, tn):
    m, k = o_gla.shape
    assert o_s5.shape[1] == k and w.shape[1] == 2 * k
    n = w.shape[2]
    zg_blk, zs_blk = zg_col // tn, zs_col // tn
    return pl.pallas_call(
        _branch_kernel,
        grid=(m // tm, n // tn),
        in_specs=[pl.BlockSpec((tm, k), lambda i, j: (i, 0)),
                  pl.BlockSpec((tm, k), lambda i, j: (i, 0)),
                  _layer_cols(w, layer, tn, 0)(k),
                  _layer_cols(w, layer, tn, 1)(k),
                  pl.BlockSpec((tm, tn), lambda i, j: (i, zg_blk + j)),
                  pl.BlockSpec((tm, tn), lambda i, j: (i, zs_blk + j))],
        out_specs=pl.BlockSpec((tm, tn), lambda i, j: (i, j)),
        out_shape=jax.ShapeDtypeStruct((m, n), BF16),
        compiler_params=_params("parallel", "parallel"),
        name="branch_merge",
    )(o_gla, o_s5, w, w, proj, proj)


def _ln_kernel(x_ref, y_ref, g_ref, b_ref, o_ref, obf_ref, *maybe_obf_t_ref, alpha, y_transposed):
    y = y_ref[...].T if y_transposed else y_ref[...]
    h = alpha * x_ref[...] + y
    mu = jnp.mean(h, axis=-1, keepdims=True)
    d = h - mu
    var = jnp.mean(d * d, axis=-1, keepdims=True)
    o = d * lax.rsqrt(var + LN_EPS) * g_ref[...] + b_ref[...]
    o_ref[...] = o
    obf_ref[...] = o.astype(BF16)
    for obf_t_ref in maybe_obf_t_ref:
        obf_t_ref[...] = o.T.astype(BF16)


def _deepnorm_ln(x, y, g, b, alpha, tm, y_transposed=False, also_transposed=False):
    m, d = x.shape
    row = pl.BlockSpec((tm, d), lambda i: (i, 0))
    col = pl.BlockSpec((d, tm), lambda i: (0, i))
    vec = pl.BlockSpec((1, d), lambda i: (0, 0))
    out_specs = [row, row]
    out_shape = [jax.ShapeDtypeStruct((m, d), F32), jax.ShapeDtypeStruct((m, d), BF16)]
    if also_transposed:
        out_specs.append(col)
        out_shape.append(jax.ShapeDtypeStruct((d, m), BF16))
    return pl.pallas_call(
        functools.partial(_ln_kernel, alpha=alpha, y_transposed=y_transposed),
        grid=(m // tm,),
        in_specs=[row, col if y_transposed else row, vec, vec],
        out_specs=out_specs,
        out_shape=out_shape,
        compiler_params=_params("parallel"),
        name="deepnorm_ln",
    )(x, y, g.reshape(1, d), b.reshape(1, d))


def _gla_kernel(*refs, c, nseq, heads, scale, has_init, has_prev):
    refs = list(refs)
    q_ref, k_ref, v_ref, r_ref, alr_ref, wa2_ref, ba_ref, g_ref = refs[:8]
    rest = refs[8:]
    s0_ref = rest.pop(0) if has_init else None
    if has_prev:
        rest.pop(0)
    o_ref, s_ref = rest
    dk = q_ref.shape[-1] // heads
    dv = v_ref.shape[-1] // heads
    kpad = max(c, LANES)
    qpad = max(c, 16)

    @pl.when(pl.program_id(1) == 0)
    def _():
        if has_init:
            s_ref[...] = s0_ref[...]
        else:
            s_ref[...] = jnp.zeros_like(s_ref)

    row = lax.broadcasted_iota(jnp.int32, (c, c), 0)
    col = lax.broadcasted_iota(jnp.int32, (c, c), 1)
    tri_cc = (row >= col).astype(F32)
    rowp = lax.broadcasted_iota(jnp.int32, (qpad, kpad), 0)
    colp = lax.broadcasted_iota(jnp.int32, (qpad, kpad), 1)
    causal = rowp >= colp
    eye = lax.broadcasted_iota(jnp.int32, (dk, dk), 0) == lax.broadcasted_iota(jnp.int32, (dk, dk), 1)
    eye_bf = eye.astype(BF16)

    def pad_rows(x, rows):
        if x.shape[0] == rows:
            return x
        return jnp.concatenate([x, jnp.zeros((rows - x.shape[0], x.shape[1]), x.dtype)], axis=0)

    outs = []
    for s in range(nseq):
        rows = slice(s * c, (s + 1) * c)
        z = jnp.dot(alr_ref[rows, :], wa2_ref[...], precision=HIGHEST, preferred_element_type=F32) + ba_ref[...]
        log_a = (jnp.minimum(z, 0.0) - jnp.log1p(jnp.exp(-jnp.abs(z)))) * (1.0 / GLA_TAU)
        if c >= LANES // 2:
            b = jnp.dot(tri_cc, log_a, precision=HIGHEST, preferred_element_type=F32)
        else:
            ridx = lax.broadcasted_iota(jnp.int32, log_a.shape, 0)
            b = jnp.zeros_like(log_a)
            for t in range(c):
                b = b + jnp.where(ridx >= t, log_a[t:t + 1, :], 0.0)
        b_last = b[c - 1:c, :]
        k = k_ref[rows, :]
        q_dec_all = (q_ref[rows, :] * scale * jnp.exp(b)).astype(BF16)
        k_inv_all = (k * jnp.exp(-b)).astype(BF16)
        k_end_all = (k * jnp.exp(b_last - b)).astype(BF16)
        v_all = v_ref[rows, :].astype(BF16)
        r = r_ref[rows, :]
        gate = r * jax.nn.sigmoid(r)
        o_heads = []
        for h in range(heads):
            kc = slice(h * dk, (h + 1) * dk)
            vc = slice(h * dv, (h + 1) * dv)
            q_dec = pad_rows(q_dec_all[:, kc], qpad)
            k_inv = pad_rows(k_inv_all[:, kc], kpad)
            k_end = pad_rows(k_end_all[:, kc], kpad)
            v_bf = pad_rows(v_all[:, vc], kpad)
            scores = jnp.where(causal, _dot_nt(q_dec, k_inv), 0.0).astype(BF16)
            state = s_ref[s, h]
            o = (_dot(q_dec, state.astype(BF16)) + _dot(scores, v_bf))[:c]
            k_end_t = _dot_nt(eye_bf, k_end).astype(BF16)
            decay_col = jnp.exp(jnp.sum(jnp.where(eye, b_last[:, kc], 0.0), axis=1, keepdims=True))
            s_ref[s, h] = decay_col * state + _dot(k_end_t, v_bf)
            o = o * lax.rsqrt(jnp.mean(o * o, axis=-1, keepdims=True) + RMS_EPS) * g_ref[...]
            o_heads.append(o * gate[:, vc])
        outs.append(jnp.concatenate(o_heads, axis=1))
    o_all = outs[0] if nseq == 1 else jnp.concatenate(outs, axis=0)
    o_ref[...] = o_all.astype(o_ref.dtype)


def _gla(proj, alr, wa2, ba, norm_g, s0, layer, s_prev, *, depth, row0, nb, seq_len, c, nseq, cols):
    heads = GLA_HEADS
    dk_tot = wa2.shape[1]
    dk = dk_tot // heads
    dv = norm_g.shape[0]
    dv_tot = heads * dv
    nchunk = seq_len // c
    assert nseq == 1 or nchunk == 1
    rb = nseq * c
    blk0 = row0 // rb
    q_col, k_col, v_col, r_col = cols

    def rows_map(col_blk):
        return lambda b, j: (blk0 + b * nchunk + j, col_blk)

    in_specs = [
        pl.BlockSpec((rb, dk_tot), rows_map(q_col // dk_tot)),
        pl.BlockSpec((rb, dk_tot), rows_map(k_col // dk_tot)),
        pl.BlockSpec((rb, dv_tot), rows_map(v_col // dv_tot)),
        pl.BlockSpec((rb, dv_tot), rows_map(r_col // dv_tot)),
        pl.BlockSpec((rb, LANES), rows_map(0)),
        pl.BlockSpec((LANES, dk_tot), lambda b, j: (0, 0)),
        pl.BlockSpec((1, dk_tot), lambda b, j: (0, 0)),
        pl.BlockSpec((1, dv), lambda b, j: (0, 0)),
    ]
    args = [proj, proj, proj, proj, alr, wa2, ba.reshape(1, -1), norm_g.reshape(1, dv)]
    state_spec = pl.BlockSpec((None, nseq, heads, dk, dv), lambda b, j: (layer, b, 0, 0, 0))
    if s0 is not None:
        in_specs.append(state_spec)
        args.append(s0)
    aliases = {}
    if s_prev is not None:
        aliases = {len(args): 1}
        in_specs.append(pl.BlockSpec(memory_space=pl.ANY))
        args.append(s_prev)
    kern = functools.partial(_gla_kernel, c=c, nseq=nseq, heads=heads, scale=dk ** -0.5,
                             has_init=s0 is not None, has_prev=s_prev is not None)
    return pl.pallas_call(
        kern,
        grid=(nb // nseq, nchunk),
        in_specs=in_specs,
        out_specs=[pl.BlockSpec((rb, dv_tot), lambda b, j: (b * nchunk + j, 0)), state_spec],
        out_shape=[jax.ShapeDtypeStruct((nb * seq_len, dv_tot), BF16),
                   jax.ShapeDtypeStruct((depth, nb, heads, dk, dv), F32)],
        input_output_aliases=aliases,
        compiler_params=_params("parallel", "arbitrary"),
        name="gla",
    )(*args)


def _s5_tables(a_re, a_im, log_dt, b_re, b_im, c_re, c_im, c):
    dt = jnp.exp(log_dt)[:, None]
    mag = jnp.exp(a_re * dt)
    ar, ai = mag * jnp.cos(a_im * dt), mag * jnp.sin(a_im * dt)
    den = a_re * a_re + a_im * a_im
    cr = ((ar - 1.0) * a_re + ai * a_im) / den
    ci = (ai * a_re - (ar - 1.0) * a_im) / den
    bbr = cr[..., None] * b_re - ci[..., None] * b_im
    bbi = cr[..., None] * b_im + ci[..., None] * b_re
    tau = jnp.arange(c + 1, dtype=F32)[:, None, None]
    pmag = jnp.exp(tau * (a_re * dt))
    pw_r, pw_i = pmag * jnp.cos(tau * (a_im * dt)), pmag * jnp.sin(tau * (a_im * dt))
    ca_r = c_re[None] * pw_r[:, :, None, :] - c_im[None] * pw_i[:, :, None, :]
    ca_i = c_re[None] * pw_i[:, :, None, :] + c_im[None] * pw_r[:, :, None, :]
    kern = (jnp.einsum('tgmp,gpn->tgmn', ca_r[:c], bbr, precision=HIGHEST)
            - jnp.einsum('tgmp,gpn->tgmn', ca_i[:c], bbi, precision=HIGHEST))
    g, m, n = kern.shape[1:]
    kern_z = jnp.concatenate([kern, jnp.zeros((1, g, m, n), F32)], axis=0)
    s_idx = jnp.arange(c)[:, None]
    t_idx = jnp.arange(c)[None, :]
    lag = jnp.where(t_idx >= s_idx, t_idx - s_idx, c)
    toep = kern_z[lag]
    toep = toep.transpose(2, 0, 4, 1, 3).reshape(g, c * n, c * m)
    rev_r, rev_i = pw_r[c - 1::-1][:c], pw_i[c - 1::-1][:c]
    inj_r = rev_r[..., None] * bbr[None] - rev_i[..., None] * bbi[None]
    inj_i = rev_r[..., None] * bbi[None] + rev_i[..., None] * bbr[None]
    inj = jnp.concatenate([inj_r, inj_i], axis=2).transpose(1, 0, 3, 2).reshape(g, c * n, -1)
    read = jnp.concatenate([ca_r[1:], -ca_i[1:]], axis=3)
    read = read.transpose(1, 3, 0, 2).reshape(g, -1, c * m)
    step_same = jnp.concatenate([pw_r[c], pw_r[c]], axis=-1)
    step_swap = jnp.concatenate([-pw_i[c], pw_i[c]], axis=-1)
    return toep, inj, read, step_same, step_swap


def _s5_in_kernel(u_ref, w_ref, yi_ref, e_ref):
    half = yi_ref.shape[-1]
    for g in range(u_ref.shape[0]):
        res = _dot(u_ref[g].astype(BF16), w_ref[g])
        yi_ref[g] = res[:, :half]
        e_ref[g] = res[:, half:]


def _s5_in(uc, w, gb):
    g, r, k = uc.shape
    n_out = w.shape[2] - k
    return pl.pallas_call(
        _s5_in_kernel,
        grid=(g // gb,),
        in_specs=[pl.BlockSpec((gb, r, k), lambda i: (i, 0, 0)),
                  pl.BlockSpec((gb, k, w.shape[2]), lambda i: (i, 0, 0))],
        out_specs=[pl.BlockSpec((gb, r, k), lambda i: (i, 0, 0)),
                   pl.BlockSpec((gb, r, n_out), lambda i: (i, 0, 0))],
        out_shape=[jax.ShapeDtypeStruct((g, r, k), F32), jax.ShapeDtypeStruct((g, r, n_out), F32)],
        compiler_params=_params("parallel"),
        name="s5_in",
    )(uc, w)


def _s5_scan_kernel(e_ref, x0_ref, same_ref, swap_ref, xprev_ref, xfin_ref):
    half = x0_ref.shape[-1] // 2
    same = same_ref[...]
    swap = swap_ref[...]

    def body(j, x):
        xprev_ref[j] = x
        return same * x + swap * pltpu.roll(x, half, 1) + e_ref[j]

    xfin_ref[...] = lax.fori_loop(0, e_ref.shape[0], body, x0_ref[...])


def _s5_scan(e, x0, same, swap, cb):
    nj, chains, w = e.shape
    cb = min(cb, chains)
    seq =pl.BlockSpec((nj, cb, w), lambda i: (0, i, 0))
    vec = pl.BlockSpec((cb, w), lambda i: (i, 0))
    return pl.pallas_call(
        _s5_scan_kernel,
        grid=(chains // cb,),
        in_specs=[seq, vec, vec, vec],
        out_specs=[seq, vec],
        out_shape=[jax.ShapeDtypeStruct((nj, chains, w), F32), jax.ShapeDtypeStruct((chains, w), F32)],
        compiler_params=_params("parallel"),
        name="s5_scan",
    )(e, x0, same, swap)


def _s5_out_kernel(yi_ref, xp_ref, rd_ref, u_ref, d_ref, y_ref):
    for g in range(yi_ref.shape[0]):
        y = yi_ref[g] + _dot(xp_ref[g].astype(BF16), rd_ref[g]) + d_ref[g] * u_ref[g]
        y_ref[g] = jax.nn.gelu(y)


def _s5_out(yi, xprev, read, uc, d_row, gb):
    g, r, k = yi.shape
    w = xprev.shape[2]
    return pl.pallas_call(
        _s5_out_kernel,
        grid=(g // gb,),
        in_specs=[pl.BlockSpec((gb, r, k), lambda i: (i, 0, 0)),
                  pl.BlockSpec((gb, r, w), lambda i: (i, 0, 0)),
                  pl.BlockSpec((gb, w, k), lambda i: (i, 0, 0)),
                  pl.BlockSpec((gb, r, k), lambda i: (i, 0, 0)),
                  pl.BlockSpec((gb, 1, k), lambda i: (i, 0, 0))],
        out_specs=pl.BlockSpec((gb, r, k), lambda i: (i, 0, 0)),
        out_shape=jax.ShapeDtypeStruct((g, r, k), F32),
        compiler_params=_params("parallel"),
        name="s5_out",
    )(yi, xprev, read, uc, d_row)


def _s5_mixer(u, tables, d_skip, st_re, st_im, n_prompt_seq, prompt_len, n_sample_seq):
    toep_inj, read, step_same, step_swap = tables
    t_tokens, width = u.shape
    c, n = S5_CHUNK, S5_GROUP
    groups = width // n
    sw = step_same.shape[-1]
    r_all = t_tokens // c
    jp = prompt_len // c
    rp = n_prompt_seq * jp
    uc = u.reshape(r_all, c, groups, n).transpose(2, 0, 1, 3).reshape(groups, r_all, c * n)
    yi, e = _s5_in(uc, toep_inj, 8)
    e_p = e[:, :rp].reshape(groups, n_prompt_seq, jp, sw).transpose(2, 0, 1, 3).reshape(jp, groups * n_prompt_seq, sw)
    zeros_p = jnp.zeros((groups * n_prompt_seq, sw), F32)
    xprev_p, xfin_p = _s5_scan(e_p, zeros_p, jnp.repeat(step_same, n_prompt_seq, axis=0),
                               jnp.repeat(step_swap, n_prompt_seq, axis=0), 64)
    xprev_p = xprev_p.reshape(jp, groups, n_prompt_seq, sw).transpose(1, 2, 0, 3).reshape(groups, rp, sw)
    x0_s = jnp.concatenate([st_re, st_im], axis=-1).transpose(1, 0, 2)
    e_s = e[:, rp:].reshape(1, groups * n_sample_seq, sw)
    _, xfin_s = _s5_scan(e_s, x0_s.reshape(groups * n_sample_seq, sw),
                         jnp.repeat(step_same, n_sample_seq, axis=0),
                         jnp.repeat(step_swap, n_sample_seq, axis=0), 2048)
    xprev = jnp.concatenate([xprev_p, x0_s], axis=1)
    d_row = jnp.tile(d_skip.reshape(groups, 1, n), (1, c, 1)).reshape(groups, 1, c * n)
    y = _s5_out(yi, xprev, read, uc, d_row, 8)
    y = y.reshape(groups, r_all, c, n).transpose(1, 2, 0, 3).reshape(t_tokens, width)
    half = sw // 2
    xfin_p = xfin_p.reshape(groups, n_prompt_seq, sw).transpose(1, 0, 2)
    xfin_s = xfin_s.reshape(groups, n_sample_seq, sw).transpose(1, 0, 2)
    return y, (xfin_p[..., :half], xfin_p[..., half:]), (xfin_s[..., :half], xfin_s[..., half:])


def _peer_route_kernel(q_ref, k1_ref, k2_ref, s1_ref, w1_ref, s2_ref, e2_ref, tau_ref):
    half = k1_ref.shape[1]
    neg = jnp.float32(-jnp.inf)

    def top_values(x):
        vals = []
        for _ in range(PEER_TOPK):
            m = jnp.max(x, axis=0, keepdims=True)
            vals.append(m)
            x = jnp.where(x == m, neg, x)
        return vals

    for h in range(PEER_HEADS):
        q1 = q_ref[:, (2 * h) * half:(2 * h + 1) * half]
        q2 = q_ref[:, (2 * h + 1) * half:(2 * h + 2) * half]
        s1 = lax.dot_general(k1_ref[...], q1, (((1,), (1,)), ((), ())), precision=HIGHEST,
                             preferred_element_type=F32)
        s2 = lax.dot_general(k2_ref[...], q2, (((1,), (1,)), ((), ())), precision=HIGHEST,
                             preferred_element_type=F32)
        a = top_values(s1)
        b = jnp.concatenate(top_values(s2), axis=0)
        b0 = b[0:1]
        cand = [a[i] + b[:PEER_TOPK // (i + 1)] for i in range(PEER_TOPK)]
        n_cand = sum(x.shape[0] for x in cand)
        cand.append(jnp.full((-n_cand % 8, s1.shape[1]), neg, F32))
        top = top_values(jnp.concatenate(cand, axis=0))
        zsum = top[0] * 0.0
        for m in top:
            zsum = zsum + jnp.exp(m - top[0])
        s1_ref[h] = s1
        s2_ref[h] = s2
        w1_ref[h] = jnp.exp(s1 - a[0]) / zsum
        e2_ref[h] = jnp.exp(s2 - b0)
        tau_ref[h] = top[-1]


def _peer_route(q, key1, key2, tm):
    t_tokens = q.shape[0]
    nk, half = key1.shape
    big = pl.BlockSpec((PEER_HEADS, nk, tm), lambda i: (0, 0, i))
    big_shape = jax.ShapeDtypeStruct((PEER_HEADS, nk, t_tokens), F32)
    return pl.pallas_call(
        _peer_route_kernel,
        grid=(t_tokens // tm,),
        in_specs=[pl.BlockSpec((tm, q.shape[1]), lambda i: (i, 0)),
                  pl.BlockSpec((nk, half), lambda i: (0, 0)),
                  pl.BlockSpec((nk, half), lambda i: (0, 0))],
        out_specs=[big, big, big, big, pl.BlockSpec((PEER_HEADS, 1, tm), lambda i: (0, 0, i))],
        out_shape=[big_shape, big_shape, big_shape, big_shape,
                   jax.ShapeDtypeStruct((PEER_HEADS, 1, t_tokens), F32)],
        compiler_params=_params("parallel"),
        name="peer_route",
    )(q, key1, key2)


def _peer_ffn_kernel(xt_ref, u_ref, vt_ref, s1_ref, w1_ref, s2_ref, e2_ref, tau_ref, o_ref, p_ref, h_ref, g_ref,
                     *, ni, n_et):
    step = pl.program_id(1)
    et = jnp.minimum(step, n_et - 1)
    slot = step % 2

    @pl.when(step == 0)
    def _():
        o_ref[...] = jnp.zeros_like(o_ref)
        p_ref[...] = jnp.zeros_like(p_ref)

    nk = s2_ref.shape[1]
    te, tm = h_ref.shape
    d = o_ref.shape[0]
    chunks = [(ii, tc) for ii in range(ni) for tc in range(tm // LANES)]

    def block(ii, tc):
        return slice(ii * nk, (ii + 1) * nk), slice(tc * LANES, (tc + 1) * LANES)

    def gate_chunk(ii, tc):
        rows, lanes = block(ii, tc)
        i = et * ni + ii
        g = None
        for h in range(PEER_HEADS):
            s1_row = s1_ref[h, pl.ds(i, 1), :][:, lanes]
            w1_row = w1_ref[h, pl.ds(i, 1), :][:, lanes]
            hit = (s1_row + s2_ref[h, :, lanes]) >= tau_ref[h, :, lanes]
            term = jnp.where(hit, e2_ref[h, :, lanes] * w1_row, 0.0)
            g = term if g is None else g + term
        g_ref[rows, lanes] = g

    def finish_chunk(ii, tc):
        rows, lanes = block(ii, tc)
        p_ref[slot, rows, lanes] = (g_ref[rows, lanes] * jax.nn.gelu(h_ref[rows, lanes])).astype(BF16)

    quarter = len(chunks) // 4
    half = tm // 2
    for c in chunks[:quarter]:
        gate_chunk(*c)
    h_ref[:, :half] = _dot(u_ref[...], xt_ref[:, :half])
    for c in chunks[quarter:2 * quarter]:
        gate_chunk(*c)
    h_ref[:, half:] = _dot(u_ref[...], xt_ref[:, half:])
    p_prev = p_ref[1 - slot]
    vector_work = [[(gate_chunk, c) for c in chunks[2 * quarter:3 * quarter]],
                   [(gate_chunk, c) for c in chunks[3 * quarter:]],
                   [(finish_chunk, c) for c in chunks[:2 * quarter]],
                   [(finish_chunk, c) for c in chunks[2 * quarter:]]]
    dq = d // len(vector_work)
    for k, work in enumerate(vector_work):
        for fn, c in work:
            fn(*c)
        rows = slice(k * dq, (k + 1) * dq)
        o_ref[rows, :] += _dot(vt_ref[rows, :], p_prev)


def _peer_ffn(xt_bf, u_tab, vt_tab, layer, route, tm, te):
    s1, w1, s2, e2, tau = route
    d, t_tokens = xt_bf.shape
    n_exp = u_tab.shape[1]
    nk = s2.shape[1]
    ni = te // nk
    n_et = n_exp // te
    once = pl.Buffered(1)
    tok = pl.BlockSpec((PEER_HEADS, nk, tm), lambda i, j: (0, 0, i), pipeline_mode=once)
    return pl.pallas_call(
        functools.partial(_peer_ffn_kernel, ni=ni, n_et=n_et),
        grid=(t_tokens // tm, n_et + 1),
        in_specs=[pl.BlockSpec((d, tm), lambda i, j: (0, i), pipeline_mode=once),
                  pl.BlockSpec((None, te, d), lambda i, j: (layer, jnp.minimum(j, n_et - 1), 0)),
                  pl.BlockSpec((None, d, te), lambda i, j: (layer, 0, jnp.maximum(j - 1, 0))),
                  tok, tok, tok, tok,
                  pl.BlockSpec((PEER_HEADS, 1, tm), lambda i, j: (0, 0, i), pipeline_mode=once)],
        out_specs=pl.BlockSpec((d, tm), lambda i, j: (0, i)),
        out_shape=jax.ShapeDtypeStruct((d, t_tokens), F32),
        scratch_shapes=[pltpu.VMEM((2, te, tm), BF16), pltpu.VMEM((te, tm), F32), pltpu.VMEM((te, tm), F32)],
        compiler_params=_params("parallel", "arbitrary"),
        name="peer_ffn",
    )(xt_bf, u_tab, vt_tab, s1, w1, s2, e2, tau)


def kernel(x_prompt, x_sample, state_gla, state_s5_re, state_s5_im, ln1_g, ln1_b, w_in, gla_w_a2, gla_b_a, gla_norm_g, s5_a_re, s5_a_im, s5_log_dt, s5_b_re, s5_b_im, s5_c_re, s5_c_im, s5_d, s5_w_glu, s5_b_glu, w_branch, w_out, ln2_g, ln2_b, peer_w_q, peer_key1, peer_key2, peer_u, peer_v):
    depth = w_in.shape[0]
    pb, plen, d = x_prompt.shape
    sb, slen, _ = x_sample.shape
    tp, ts = pb * plen, sb * slen
    t_all = tp + ts
    dk_tot = gla_w_a2.shape[2]
    dv_tot = GLA_HEADS * gla_norm_g.shape[1]
    s5_w = s5_d.shape[1]
    alpha = (2 * depth) ** 0.25
    widths = (dk_tot, dk_tot, dv_tot, dv_tot, GLA_RANK, s5_w, d, d)
    off = [0]
    for wdt in widths:
        off.append(off[-1] + wdt)
    lr0, lr1 = off[4], off[5]
    q_col, k_col, v_col, r_col = off[0], off[1], off[2], off[3]
    u_col = lr0
    zg_col = u_col + s5_w
    zs_col = zg_col + d

    x = jnp.concatenate([x_prompt.reshape(tp, d), x_sample.reshape(ts, d)], axis=0)
    x_bf = x.astype(BF16)
    w_main = jnp.concatenate([w_in[:, :, :lr0], w_in[:, :, lr1:]], axis=2).astype(BF16)
    w_lr = jnp.pad(w_in[:, :, lr0:lr1], ((0, 0), (0, 0), (0, LANES - GLA_RANK))).astype(BF16)
    wa2 = jnp.pad(gla_w_a2, ((0, 0), (0, LANES - GLA_RANK), (0, 0)))
    w_glu_bf = s5_w_glu.astype(BF16)
    w_br_bf = w_branch.astype(BF16)
    w_out_bf = w_out.astype(BF16)
    w_q_bf = peer_w_q.astype(BF16)
    u_bf = peer_u.astype(BF16)
    vt_bf = peer_v.astype(BF16).transpose(0, 2, 1)
    cols = (q_col, k_col, v_col, r_col)
    gla_p = gla_s = None
    re_p, im_p, re_s, im_s = [], [], [], []
    for l in range(depth):
        proj = _mm(x_bf, w_main, l, 1024, 1024)
        alr = _mm(x_bf, w_lr, l, 1024, LANES)

        o_gla, gla_p = _gla(proj, alr, wa2[l], gla_b_a[l], gla_norm_g[l], None, l, gla_p, depth=depth, row0=0,
                            nb=pb, seq_len=plen, c=min(GLA_CHUNK, plen), nseq=1, cols=cols)
        o_gla_s, gla_s = _gla(proj, alr, wa2[l], gla_b_a[l], gla_norm_g[l], state_gla, l, gla_s, depth=depth,
                              row0=tp, nb=sb, seq_len=slen, c=min(GLA_CHUNK, slen), nseq=2, cols=cols)
        o_gla = jnp.concatenate([o_gla, o_gla_s], axis=0)

        toep, inj, read, step_same, step_swap = _s5_tables(
            s5_a_re[l], s5_a_im[l], s5_log_dt[l], s5_b_re[l], s5_b_im[l], s5_c_re[l], s5_c_im[l], S5_CHUNK)
        tables = (jnp.concatenate([toep, inj], axis=2).astype(BF16), read.astype(BF16), step_same, step_swap)
        y_s5, (r_p, i_p), (r_s, i_s) = _s5_mixer(proj[:, u_col:u_col + s5_w], tables, s5_d[l],
                                                 state_s5_re[l], state_s5_im[l], pb, plen, sb)
        o_s5 = _glu(y_s5, y_s5.astype(BF16), w_glu_bf, l, s5_b_glu[l], 1024, 1024)

        merged = _branch_merge(o_gla, o_s5, w_br_bf, l, proj, zg_col, zs_col, 1024, 512)
        mix = _mm(merged, w_out_bf, l, 1024, 1024)
        x, x_bf, xt_bf = _deepnorm_ln(x, mix, ln1_g[l], ln1_b[l], alpha, 256, also_transposed=True)

        q = _mm(x_bf, w_q_bf, l, 1024, 1024)
        route = _peer_route(q, peer_key1[l], peer_key2[l], 256)
        ffn_t = _peer_ffn(xt_bf, u_bf, vt_bf, l, route, 512, 512)
        x, x_bf = _deepnorm_ln(x, ffn_t, ln2_g[l], ln2_b[l], alpha, 256, y_transposed=True)

        re_p.append(r_p)
        im_p.append(i_p)
        re_s.append(r_s)
        im_s.append(i_s)

    y_prompt = x[:tp].reshape(pb, plen, d)
    y_sample = x[tp:].reshape(sb, slen, d)
    return (y_prompt, y_sample, gla_p, jnp.stack(re_p), jnp.stack(im_p),
            gla_s, jnp.stack(re_s), jnp.stack(im_s))
```

```python
import functools

import jax
import jax.numpy as jnp
from jax import lax
from jax.experimental import pallas as pl
from jax.experimental.pallas import tpu as pltpu

F32 = jnp.float32
BF16 = jnp.bfloat16
HIGHEST = lax.Precision.HIGHEST

V7X_VMEM_BYTES = 64 * 1024 * 1024
VMEM_LIMIT_BYTES = V7X_VMEM_BYTES - 8 * 1024 * 1024
LANES = 128

GLA_HEADS = 4
GLA_RANK = 16
GLA_TAU = 16.0
GLA_CHUNK = 64
RMS_EPS = 1e-6
S5_CHUNK = 8
PEER_HEADS = 8
PEER_TOPK = 16
LN_EPS = 1e-5


def _params(*semantics, flags=None):
    return pltpu.CompilerParams(dimension_semantics=semantics, vmem_limit_bytes=VMEM_LIMIT_BYTES, flags=flags)


def _dot(a, b):
    return jnp.dot(a, b, preferred_element_type=F32)


def _dot_nt(a, b):
    return lax.dot_general(a, b, (((1,), (1,)), ((), ())), preferred_element_type=F32)


def _mm_kernel(a_ref, b_ref, o_ref):
    o_ref[...] = _dot(a_ref[...], b_ref[...]).astype(o_ref.dtype)


def _layer_cols(w, layer, tn, row0=0):
    def spec(rows):
        return pl.BlockSpec((None, rows, tn), lambda i, j: (layer, row0, j))
    return spec


def _col_tile(n, cap=1024):
    return max(t for t in range(LANES, cap + 1, LANES) if n % t == 0)


def _mm(a, w, layer, tm, tn, out_dtype=F32):
    m, k = a.shape
    n = w.shape[2]
    return pl.pallas_call(
        _mm_kernel,
        grid=(m // tm, n // tn),
        in_specs=[pl.BlockSpec((tm, k), lambda i, j: (i, 0)),
                  _layer_cols(w, layer, tn)(k)],
        out_specs=pl.BlockSpec((tm, tn), lambda i, j: (i, j)),
        out_shape=jax.ShapeDtypeStruct((m, n), out_dtype),
        compiler_params=_params("parallel", "parallel"),
        name="mm",
    )(a, w)


def _glu_kernel(y_ref, w_ref, b_ref, ytile_ref, o_ref):
    acc = _dot(y_ref[...].astype(BF16), w_ref[...]) + b_ref[...]
    o_ref[...] = (ytile_ref[...] * jax.nn.sigmoid(acc)).astype(o_ref.dtype)


def _glu(y, w, layer, b, tm, tn):
    m, k = y.shape
    n = w.shape[2]
    return pl.pallas_call(
        _glu_kernel,
        grid=(m // tm, n // tn),
        in_specs=[pl.BlockSpec((tm, k), lambda i, j: (i, 0)),
                  _layer_cols(w, layer, tn)(k),
                  pl.BlockSpec((1, tn), lambda i, j: (0, j)),
                  pl.BlockSpec((tm, tn), lambda i, j: (i, j))],
        out_specs=pl.BlockSpec((tm, tn), lambda i, j: (i, j)),
        out_shape=jax.ShapeDtypeStruct((m, n), BF16),
        compiler_params=_params("parallel", "parallel"),
        name="s5_glu",
    )(y, w, b.reshape(1, n), y)


def _branch_kernel(og_ref, os_ref, w1_ref, w2_ref, zg_ref, zs_ref, o_ref):
    m1 = _dot(og_ref[...], w1_ref[...])
    m2 = _dot(os_ref[...], w2_ref[...])
    o_ref[...] = (jax.nn.sigmoid(zg_ref[...]) * m1 + jax.nn.sigmoid(zs_ref[...]) * m2).astype(o_ref.dtype)


def _branch_merge(o_gla, o_s5, w, layer, proj, zg_col, zs_col, tm, tn):
    m, k = o_gla.shape
    assert o_s5.shape[1] == k and w.shape[1] == 2 * k
    n = w.shape[2]
    zg_blk, zs_blk = zg_col // tn, zs_col // tn
    return pl.pallas_call(
        _branch_kernel,
        grid=(m // tm, n // tn),
        in_specs=[pl.BlockSpec((tm, k), lambda i, j: (i, 0)),
                  pl.BlockSpec((tm, k), lambda i, j: (i, 0)),
                  _layer_cols(w, layer, tn, 0)(k),
                  _layer_cols(w, layer, tn, 1)(k),
                  pl.BlockSpec((tm, tn), lambda i, j: (i, zg_blk + j)),
                  pl.BlockSpec((tm, tn), lambda i, j: (i, zs_blk + j))],
        out_specs=pl.BlockSpec((tm, tn), lambda i, j: (i, j)),
        out_shape=jax.ShapeDtypeStruct((m, n), BF16),
        compiler_params=_params("parallel", "parallel"),
        name="branch_merge",
    )(o_gla, o_s5, w, w, proj, proj)


def _ln_kernel(x_ref, y_ref, g_ref, b_ref, o_ref, obf_ref, *maybe_obf_t_ref, alpha, y_transposed):
    y = y_ref[...].T if y_transposed else y_ref[...]
    h = alpha * x_ref[...] + y
    mu = jnp.mean(h, axis=-1, keepdims=True)
    d = h - mu
    var = jnp.mean(d * d, axis=-1, keepdims=True)
    o = d * lax.rsqrt(var + LN_EPS) * g_ref[...] + b_ref[...]
    o_ref[...] = o
    obf_ref[...] = o.astype(BF16)
    for obf_t_ref in maybe_obf_t_ref:
        obf_t_ref[...] = o.T.astype(BF16)


def _deepnorm_ln(x, y, g, b, alpha, tm, y_transposed=False, also_transposed=False):
    m, d = x.shape
    row = pl.BlockSpec((tm, d), lambda i: (i, 0))
    col = pl.BlockSpec((d, tm), lambda i: (0, i))
    vec = pl.BlockSpec((1, d), lambda i: (0, 0))
    out_specs = [row, row]
    out_shape = [jax.ShapeDtypeStruct((m, d), F32), jax.ShapeDtypeStruct((m, d), BF16)]
    if also_transposed:
        out_specs.append(col)
        out_shape.append(jax.ShapeDtypeStruct((d, m), BF16))
    return pl.pallas_call(
        functools.partial(_ln_kernel, alpha=alpha, y_transposed=y_transposed),
        grid=(m // tm,),
        in_specs=[row, col if y_transposed else row, vec, vec],
        out_specs=out_specs,
        out_shape=out_shape,
        compiler_params=_params("parallel"),
        name="deepnorm_ln",
    )(x, y, g.reshape(1, d), b.reshape(1, d))


def _gla_kernel(*refs, c, nseq, heads, scale, has_init, n_prev):
    refs = list(refs)
    q_ref, k_ref, v_ref, r_ref, alr_ref, wa2_ref, ba_ref, g_ref = refs[:8]
    rest = refs[8:]
    s0_ref = rest.pop(0) if has_init else None
    o_ref, s_ref = rest[n_prev:]
    dk = q_ref.shape[-1] // heads
    dv = v_ref.shape[-1] // heads
    kpad = max(c, LANES)
    qpad = max(c, 16)

    @pl.when(pl.program_id(1) == 0)
    def _():
        if has_init:
            s_ref[...] = s0_ref[...]
        else:
            s_ref[...] = jnp.zeros_like(s_ref)

    row = lax.broadcasted_iota(jnp.int32, (c, c), 0)
    col = lax.broadcasted_iota(jnp.int32, (c, c), 1)
    tri_cc = (row >= col).astype(F32)
    rowp = lax.broadcasted_iota(jnp.int32, (qpad, kpad), 0)
    colp = lax.broadcasted_iota(jnp.int32, (qpad, kpad), 1)
    causal = rowp >= colp
    eye = lax.broadcasted_iota(jnp.int32, (dk, dk), 0) == lax.broadcasted_iota(jnp.int32, (dk, dk), 1)
    eye_bf = eye.astype(BF16)

    def pad_rows(x, rows):
        if x.shape[0] == rows:
            return x
        return jnp.concatenate([x, jnp.zeros((rows - x.shape[0], x.shape[1]), x.dtype)], axis=0)

    outs = []
    for s in range(nseq):
        rows = slice(s * c, (s + 1) * c)
        z = jnp.dot(alr_ref[rows, :], wa2_ref[...], precision=HIGHEST, preferred_element_type=F32) + ba_ref[...]
        log_a = (jnp.minimum(z, 0.0) - jnp.log1p(jnp.exp(-jnp.abs(z)))) * (1.0 / GLA_TAU)
        if c >= LANES // 2:
            b = jnp.dot(tri_cc, log_a, precision=HIGHEST, preferred_element_type=F32)
        else:
            ridx = lax.broadcasted_iota(jnp.int32, log_a.shape, 0)
            b = jnp.zeros_like(log_a)
            for t in range(c):
                b = b + jnp.where(ridx >= t, log_a[t:t + 1, :], 0.0)
        b_last = b[c - 1:c, :]
        k = k_ref[rows, :]
        q_dec_all = (q_ref[rows, :] * scale * jnp.exp(b)).astype(BF16)
        k_inv_all = (k * jnp.exp(-b)).astype(BF16)
        k_end_all = (k * jnp.exp(b_last - b)).astype(BF16)
        v_all = v_ref[rows, :].astype(BF16)
        r = r_ref[rows, :]
        gate = r * jax.nn.sigmoid(r)
        o_heads = []
        for h in range(heads):
            kc = slice(h * dk, (h + 1) * dk)
            vc = slice(h * dv, (h + 1) * dv)
            q_dec = pad_rows(q_dec_all[:, kc], qpad)
            k_inv = pad_rows(k_inv_all[:, kc], kpad)
            k_end = pad_rows(k_end_all[:, kc], kpad)
            v_bf = pad_rows(v_all[:, vc], kpad)
            scores = jnp.where(causal, _dot_nt(q_dec, k_inv), 0.0).astype(BF16)
            state = s_ref[s, h]
            o = (_dot(q_dec, state.astype(BF16)) + _dot(scores, v_bf))[:c]
            k_end_t = _dot_nt(eye_bf, k_end).astype(BF16)
            decay_col = jnp.exp(jnp.sum(jnp.where(eye, b_last[:, kc], 0.0), axis=1, keepdims=True))
            s_ref[s, h] = decay_col * state + _dot(k_end_t, v_bf)
            o = o * lax.rsqrt(jnp.mean(o * o, axis=-1, keepdims=True) + RMS_EPS) * g_ref[...]
            o_heads.append(o * gate[:, vc])
        outs.append(jnp.concatenate(o_heads, axis=1))
    o_all = outs[0] if nseq == 1 else jnp.concatenate(outs, axis=0)
    o_ref[...] = o_all.astype(o_ref.dtype)


def _gla(proj, wa2, ba, norm_g, s0, layer, s_prev, o_prev, *, depth, row0, n_rows, nb, seq_len, c, nseq, cols):
    heads = GLA_HEADS
    dk_tot = wa2.shape[1]
    dk = dk_tot // heads
    dv = norm_g.shape[0]
    dv_tot = heads * dv
    nchunk = seq_len // c
    assert nseq == 1 or nchunk == 1
    rb = nseq * c
    blk0 = row0 // rb
    q_col, k_col, v_col, r_col, lr_col = cols

    def rows_map(col_blk):
        return lambda b, j: (blk0 + b * nchunk + j, col_blk)

    in_specs = [
        pl.BlockSpec((rb, dk_tot), rows_map(q_col // dk_tot)),
        pl.BlockSpec((rb, dk_tot), rows_map(k_col // dk_tot)),
        pl.BlockSpec((rb, dv_tot), rows_map(v_col // dv_tot)),
        pl.BlockSpec((rb, dv_tot), rows_map(r_col // dv_tot)),
        pl.BlockSpec((rb, LANES), rows_map(lr_col // LANES)),
        pl.BlockSpec((LANES, dk_tot), lambda b, j: (0, 0)),
        pl.BlockSpec((1, dk_tot), lambda b, j: (0, 0)),
        pl.BlockSpec((1, dv), lambda b, j: (0, 0)),
    ]
    args = [proj, proj, proj, proj, proj, wa2, ba.reshape(1, -1), norm_g.reshape(1, dv)]
    state_spec = pl.BlockSpec((None, nseq, heads, dk, dv), lambda b, j: (layer, b, 0, 0, 0))
    if s0 is not None:
        in_specs.append(state_spec)
        args.append(s0)
    aliases = {}
    for prev, out_idx in ((s_prev, 1), (o_prev, 0)):
        if prev is not None:
            aliases[len(args)] = out_idx
            in_specs.append(pl.BlockSpec(memory_space=pl.ANY))
            args.append(prev)
    kern = functools.partial(_gla_kernel, c=c, nseq=nseq, heads=heads, scale=dk ** -0.5,
                             has_init=s0 is not None, n_prev=len(aliases))
    return pl.pallas_call(
        kern,
        grid=(nb // nseq, nchunk),
        in_specs=in_specs,
        out_specs=[pl.BlockSpec((rb, dv_tot), rows_map(0)), state_spec],
        out_shape=[jax.ShapeDtypeStruct((n_rows, dv_tot), BF16),
                   jax.ShapeDtypeStruct((depth, nb, heads, dk, dv), F32)],
        input_output_aliases=aliases,
        compiler_params=_params("parallel", "arbitrary"),
        name="gla",
    )(*args)


def _s5_tables(a_re, a_im, log_dt, b_re, b_im, c_re, c_im, c):
    dt = jnp.exp(log_dt)[:, None]
    mag = jnp.exp(a_re * dt)
    ar, ai = mag * jnp.cos(a_im * dt), mag * jnp.sin(a_im * dt)
    den = a_re * a_re + a_im * a_im
    cr = ((ar - 1.0) * a_re + ai * a_im) / den
    ci = (ai * a_re - (ar - 1.0) * a_im) / den
    bbr = cr[..., None] * b_re - ci[..., None] * b_im
    bbi = cr[..., None] * b_im + ci[..., None] * b_re
    tau = jnp.arange(c + 1, dtype=F32)[:, None, None]
    pmag = jnp.exp(tau * (a_re * dt))
    pw_r, pw_i = pmag * jnp.cos(tau * (a_im * dt)), pmag * jnp.sin(tau * (a_im * dt))
    ca_r = c_re[None] * pw_r[:, :, None, :] - c_im[None] * pw_i[:, :, None, :]
    ca_i = c_re[None] * pw_i[:, :, None, :] + c_im[None] * pw_r[:, :, None, :]
    kern = (jnp.einsum('tgmp,gpn->tgmn', ca_r[:c], bbr, precision=HIGHEST)
            - jnp.einsum('tgmp,gpn->tgmn', ca_i[:c], bbi, precision=HIGHEST))
    g, m, n = kern.shape[1:]
    p = a_re.shape[1]
    gpb = LANES // n
    nb = g // gpb
    eye = jnp.eye(gpb, dtype=F32)

    def block_diag(x, g_axis, h_axis):
        x = jnp.expand_dims(x, h_axis)
        shape = [1] * x.ndim
        shape[g_axis], shape[h_axis] = gpb, gpb
        return x * eye.reshape(shape)

    kern_z = jnp.concatenate([kern, jnp.zeros((1, g, m, n), F32)], axis=0)
    s_idx = jnp.arange(c)[:, None]
    t_idx = jnp.arange(c)[None, :]
    lag = jnp.where(t_idx >= s_idx, t_idx - s_idx, c)
    toep = kern_z[lag].reshape(c, c, nb, gpb, m, n).transpose(2, 0, 3, 5, 1, 4)
    toep = block_diag(toep, 2, 5).reshape(nb, c * LANES, c * LANES)
    back = (c - 1.0) - tau[:c]
    rev_mag = jnp.exp(back * (a_re * dt))
    rev_r, rev_i = rev_mag * jnp.cos(back * (a_im * dt)), rev_mag * jnp.sin(back * (a_im * dt))
    inj_r = rev_r[..., None] * bbr[None] - rev_i[..., None] * bbi[None]
    inj_i = rev_r[..., None] * bbi[None] + rev_i[..., None] * bbr[None]
    inj = jnp.concatenate(
        [block_diag(x.reshape(c, nb, gpb, p, n).transpose(1, 0, 2, 4, 3), 2, 4).reshape(nb, c * LANES, gpb * p)
         for x in (inj_r, inj_i)], axis=2)
    read = jnp.concatenate(
        [block_diag(x.reshape(c, nb, gpb, m, p).transpose(1, 2, 4, 0, 3), 1, 4).reshape(nb, gpb * p, c * LANES)
         for x in (ca_r[1:], -ca_i[1:])], axis=1)
    step_r, step_i = pw_r[c].reshape(nb, 1, gpb * p), pw_i[c].reshape(nb, 1, gpb * p)
    step_same = jnp.concatenate([step_r, step_r], axis=-1)
    step_swap = jnp.concatenate([-step_i, step_i], axis=-1)
    return toep.astype(BF16), inj.astype(BF16), read.astype(BF16), step_same, step_swap


def _chunk_rows(u_ref, c):
    r = u_ref.shape[0] // c
    return jnp.concatenate([u_ref[pl.ds(s, r, stride=c), :].astype(BF16) for s in range(c)], axis=1)


def _s5_inject_kernel(u_ref, w_ref, e_ref, *, c):
    e_ref[...] = _dot(_chunk_rows(u_ref, c), w_ref[...])


def _s5_inject(proj, u_col, inj, c):
    t_tokens = proj.shape[0]
    nb, k, w = inj.shape
    return pl.pallas_call(
        functools.partial(_s5_inject_kernel, c=c),
        grid=(nb,),
        in_specs=[pl.BlockSpec((t_tokens, LANES), lambda b: (0, u_col // LANES + b)),
                  pl.BlockSpec((None, k, w), lambda b: (b, 0, 0))],
        out_specs=pl.BlockSpec((None, t_tokens // c, w), lambda b: (b, 0, 0)),
        out_shape=jax.ShapeDtypeStruct((nb, t_tokens // c, w), F32),
        compiler_params=_params("parallel"),
        name="s5_inject",
    )(proj, inj)


def _s5_scan_kernel(e_ref, x0_ref, same_ref, swap_ref, xprev_ref, xfin_p_ref, xfin_s_ref, *, n_seq, n_chunk):
    half = e_ref.shape[-1] // 2
    same = same_ref[...]
    swap = swap_ref[...]

    def advance(x, e):
        x_swapped = jnp.concatenate([x[:, half:], x[:, :half]], axis=1)
        return same * x + swap * x_swapped + e

    def body(j, xs):
        new = []
        for q, x in enumerate(xs):
            row = q * n_chunk + j
            xprev_ref[pl.ds(row, 1), :] = x
            new.append(advance(x, e_ref[pl.ds(row, 1), :]))
        return tuple(new)

    zero = jnp.zeros((1, e_ref.shape[-1]), F32)
    xs = lax.fori_loop(0, n_chunk, body, (zero,) * n_seq)
    xfin_p_ref[...] = jnp.zeros_like(xfin_p_ref)
    for q, x in enumerate(xs):
        xfin_p_ref[q:q + 1, :] = x
    rp = n_seq * n_chunk
    x0 = x0_ref[...]
    xprev_ref[rp:, :] = x0
    xfin_s_ref[...] = advance(x0, e_ref[rp:, :])


def _s5_scan(e, x0_s, same, swap, n_seq, n_chunk):
    nb, r, w = e.shape
    ns = x0_s.shape[1]
    assert r == n_seq * n_chunk + ns
    pad_seq = -(-n_seq // 8) * 8
    vec = pl.BlockSpec((None, 1, w), lambda b: (b, 0, 0))
    return pl.pallas_call(
        functools.partial(_s5_scan_kernel, n_seq=n_seq, n_chunk=n_chunk),
        grid=(nb,),
        in_specs=[pl.BlockSpec((None, r, w), lambda b: (b, 0, 0)),
                  pl.BlockSpec((None, ns, w), lambda b: (b, 0, 0)), vec, vec],
        out_specs=[pl.BlockSpec((None, r, w), lambda b: (b, 0, 0)),
                   pl.BlockSpec((None, pad_seq, w), lambda b: (b, 0, 0)),
                   pl.BlockSpec((None, ns, w), lambda b: (b, 0, 0))],
        out_shape=[jax.ShapeDtypeStruct((nb, r, w), F32), jax.ShapeDtypeStruct((nb, pad_seq, w), F32),
                   jax.ShapeDtypeStruct((nb, ns, w), F32)],
        compiler_params=_params("parallel"),
        name="s5_scan",
    )(e, x0_s, same, swap)


def _s5_out_kernel(u_ref, xp_ref, toep_ref, read_ref, d_ref, y_ref, *, c):
    r = u_ref.shape[0] // c
    y = _dot(_chunk_rows(u_ref, c), toep_ref[...]) + _dot(xp_ref[...].astype(BF16), read_ref[...])
    for t in range(c):
        rows = pl.ds(t, r, stride=c)
        y_ref[rows, :] = jax.nn.gelu(y[:, t * LANES:(t + 1) * LANES] + d_ref[...] * u_ref[rows, :])


def _s5_out(proj, u_col, xprev, toep, read, d_skip, c):
    t_tokens = proj.shape[0]
    nb, r, w = xprev.shape
    k = toep.shape[1]
    u_blk = u_col // LANES
    return pl.pallas_call(
        functools.partial(_s5_out_kernel, c=c),
        grid=(nb,),
        in_specs=[pl.BlockSpec((t_tokens, LANES), lambda b: (0, u_blk + b)),
                  pl.BlockSpec((None, r, w), lambda b: (b, 0, 0)),
                  pl.BlockSpec((None, k, k), lambda b: (b, 0, 0)),
                  pl.BlockSpec((None, w, k), lambda b: (b, 0, 0)),
                  pl.BlockSpec((1, LANES), lambda b: (0, b))],
        out_specs=pl.BlockSpec((t_tokens, LANES), lambda b: (0, b)),
        out_shape=jax.ShapeDtypeStruct((t_tokens, nb * LANES), F32),
        compiler_params=_params("parallel"),
        name="s5_out",
    )(proj, xprev, toep, read, d_skip.reshape(1, -1))


def _s5_mixer(proj, u_col, tables, d_skip, st_re, st_im, n_prompt_seq, prompt_len):
    toep, inj, read, step_same, step_swap = tables
    c = S5_CHUNK
    nb = toep.shape[0]
    ns, groups, p = st_re.shape
    gpb = groups // nb

    def to_blocks(st):
        return st.reshape(ns, nb, gpb * p).transpose(1, 0, 2)

    def from_blocks(x):
        return x.transpose(1, 0, 2).reshape(x.shape[1], groups, p)

    e = _s5_inject(proj, u_col, inj, c)
    x0_s = jnp.concatenate([to_blocks(st_re), to_blocks(st_im)], axis=-1)
    xprev, xfin_p, xfin_s = _s5_scan(e, x0_s, step_same, step_swap, n_prompt_seq, prompt_len // c)
    y = _s5_out(proj, u_col, xprev, toep, read, d_skip, c)
    half = gpb * p
    xfin_p = xfin_p[:, :n_prompt_seq]
    return (y, (from_blocks(xfin_p[..., :half]), from_blocks(xfin_p[..., half:])),
            (from_blocks(xfin_s[..., :half]), from_blocks(xfin_s[..., half:])))


def _peer_route_kernel(q_ref, k1_ref, k2_ref, s1_ref, w1_ref, s2_ref, e2_ref, tau_ref):
    half = k1_ref.shape[1]
    neg = jnp.float32(-jnp.inf)

    def top_values(x):
        vals = []
        for _ in range(PEER_TOPK):
            m = jnp.max(x, axis=0, keepdims=True)
            vals.append(m)
            x = jnp.where(x == m, neg, x)
        return vals

    for h in range(PEER_HEADS):
        q1 = q_ref[:, (2 * h) * half:(2 * h + 1) * half]
        q2 = q_ref[:, (2 * h + 1) * half:(2 * h + 2) * half]
        s1 = lax.dot_general(k1_ref[...], q1, (((1,), (1,)), ((), ())), precision=HIGHEST,
                             preferred_element_type=F32)
        s2 = lax.dot_general(k2_ref[...], q2, (((1,), (1,)), ((), ())), precision=HIGHEST,
                             preferred_element_type=F32)
        a = top_values(s1)
        b = jnp.concatenate(top_values(s2), axis=0)
        b0 = b[0:1]
        cand = [a[i] + b[:PEER_TOPK // (i + 1)] for i in range(PEER_TOPK)]
        n_cand = sum(x.shape[0] for x in cand)
        cand.append(jnp.full((-n_cand % 8, s1.shape[1]), neg, F32))
        top = top_values(jnp.concatenate(cand, axis=0))
        zsum = top[0] * 0.0
        for m in top:
            zsum = zsum + jnp.exp(m - top[0])
        s1_ref[h] = s1
        s2_ref[h] = s2
        w1_ref[h] = jnp.exp(s1 - a[0]) / zsum
        e2_ref[h] = jnp.exp(s2 - b0)
        tau_ref[h] = top[-1]


def _peer_route(q, key1, key2, tm):
    t_tokens = q.shape[0]
    nk, half = key1.shape
    big = pl.BlockSpec((PEER_HEADS, nk, tm), lambda i: (0, 0, i))
    big_shape = jax.ShapeDtypeStruct((PEER_HEADS, nk, t_tokens), F32)
    return pl.pallas_call(
        _peer_route_kernel,
        grid=(t_tokens // tm,),
        in_specs=[pl.BlockSpec((tm, q.shape[1]), lambda i: (i, 0)),
                  pl.BlockSpec((nk, half), lambda i: (0, 0)),
                  pl.BlockSpec((nk, half), lambda i: (0, 0))],
        out_specs=[big, big, big, big, pl.BlockSpec((PEER_HEADS, 1, tm), lambda i: (0, 0, i))],
        out_shape=[big_shape, big_shape, big_shape, big_shape,
                   jax.ShapeDtypeStruct((PEER_HEADS, 1, t_tokens), F32)],
        compiler_params=_params("parallel"),
        name="peer_route",
    )(q, key1, key2)


def _peer_ffn_kernel(xt_ref, u_ref, vt_ref, s1_ref, w1_ref, s2_ref, e2_ref, tau_ref, o_ref, p_ref, h_ref, g_ref,
                     *, ni, n_et):
    step = pl.program_id(1)
    et = jnp.minimum(step, n_et - 1)
    slot = step % 2

    @pl.when(step == 0)
    def _():
        o_ref[...] = jnp.zeros_like(o_ref)
        p_ref[...] = jnp.zeros_like(p_ref)

    nk = s2_ref.shape[1]
    te, tm = h_ref.shape
    d = o_ref.shape[0]
    chunks = [(ii, tc) for ii in range(ni) for tc in range(tm // LANES)]

    def block(ii, tc):
        return slice(ii * nk, (ii + 1) * nk), slice(tc * LANES, (tc + 1) * LANES)

    def gate_chunk(ii, tc):
        rows, lanes = block(ii, tc)
        i = et * ni + ii
        g = None
        for h in range(PEER_HEADS):
            s1_row = s1_ref[h, pl.ds(i, 1), :][:, lanes]
            w1_row = w1_ref[h, pl.ds(i, 1), :][:, lanes]
            hit = (s1_row + s2_ref[h, :, lanes]) >= tau_ref[h, :, lanes]
            term = jnp.where(hit, e2_ref[h, :, lanes] * w1_row, 0.0)
            g = term if g is None else g + term
        g_ref[rows, lanes] = g

    def finish_chunk(ii, tc):
        rows, lanes = block(ii, tc)
        p_ref[slot, rows, lanes] = (g_ref[rows, lanes] * jax.nn.gelu(h_ref[rows, lanes])).astype(BF16)

    quarter = len(chunks) // 4
    half = tm // 2
    for c in chunks[:quarter]:
        gate_chunk(*c)
    h_ref[:, :half] = _dot(u_ref[...], xt_ref[:, :half])
    for c in chunks[quarter:2 * quarter]:
        gate_chunk(*c)
    h_ref[:, half:] = _dot(u_ref[...], xt_ref[:, half:])
    p_prev = p_ref[1 - slot]
    vector_work = [[(gate_chunk, c) for c in chunks[2 * quarter:3 * quarter]],
                   [(gate_chunk, c) for c in chunks[3 * quarter:]],
                   [(finish_chunk, c) for c in chunks[:2 * quarter]],
                   [(finish_chunk, c) for c in chunks[2 * quarter:]]]
    dq = d // len(vector_work)
    for k, work in enumerate(vector_work):
        for fn, c in work:
            fn(*c)
        rows = slice(k * dq, (k + 1) * dq)
        o_ref[rows, :] += _dot(vt_ref[rows, :], p_prev)


def _peer_ffn(xt_bf, u_tab, vt_tab, layer, route, tm, te):
    s1, w1, s2, e2, tau = route
    d, t_tokens = xt_bf.shape
    n_exp = u_tab.shape[1]
    nk = s2.shape[1]
    ni = te // nk
    n_et = n_exp // te
    once = pl.Buffered(1)
    tok = pl.BlockSpec((PEER_HEADS, nk, tm), lambda i, j: (0, 0, i), pipeline_mode=once)
    return pl.pallas_call(
        functools.partial(_peer_ffn_kernel, ni=ni, n_et=n_et),
        grid=(t_tokens // tm, n_et + 1),
        in_specs=[pl.BlockSpec((d, tm), lambda i, j: (0, i), pipeline_mode=once),
                  pl.BlockSpec((None, te, d), lambda i, j: (layer, jnp.minimum(j, n_et - 1), 0)),
                  pl.BlockSpec((None, d, te), lambda i, j: (layer, 0, jnp.maximum(j - 1, 0))),
                  tok, tok, tok, tok,
                  pl.BlockSpec((PEER_HEADS, 1, tm), lambda i, j: (0, 0, i), pipeline_mode=once)],
        out_specs=pl.BlockSpec((d, tm), lambda i, j: (0, i)),
        out_shape=jax.ShapeDtypeStruct((d, t_tokens), F32),
        scratch_shapes=[pltpu.VMEM((2, te, tm), BF16), pltpu.VMEM((te, tm), F32), pltpu.VMEM((te, tm), F32)],
        compiler_params=_params("parallel", "arbitrary"),
        name="peer_ffn",
    )(xt_bf, u_tab, vt_tab, s1, w1, s2, e2, tau)


def kernel(x_prompt, x_sample, state_gla, state_s5_re, state_s5_im, ln1_g, ln1_b, w_in, gla_w_a2, gla_b_a, gla_norm_g, s5_a_re, s5_a_im, s5_log_dt, s5_b_re, s5_b_im, s5_c_re, s5_c_im, s5_d, s5_w_glu, s5_b_glu, w_branch, w_out, ln2_g, ln2_b, peer_w_q, peer_key1, peer_key2, peer_u, peer_v):
    depth = w_in.shape[0]
    pb, plen, d = x_prompt.shape
    sb, slen, _ = x_sample.shape
    tp, ts = pb * plen, sb * slen
    t_all = tp + ts
    dk_tot = gla_w_a2.shape[2]
    dv_tot = GLA_HEADS * gla_norm_g.shape[1]
    s5_w = s5_d.shape[1]
    alpha = (2 * depth) ** 0.25
    widths = (dk_tot, dk_tot, dv_tot, dv_tot, GLA_RANK, s5_w, d, d)
    off = [0]
    for wdt in widths:
        off.append(off[-1] + wdt)
    lr0, lr1 = off[4], off[5]
    x = jnp.concatenate([x_prompt.reshape(tp, d), x_sample.reshape(ts, d)], axis=0)
    x_bf = x.astype(BF16)
    a_cols = lr0 + LANES
    w_a = w_in[:, :, :a_cols].astype(BF16)
    w_b = w_in[:, :, lr1:].astype(BF16)
    wa2 = jnp.pad(gla_w_a2, ((0, 0), (0, LANES - GLA_RANK), (0, 0)))
    w_glu_bf = s5_w_glu.astype(BF16)
    w_br_bf = w_branch.astype(BF16)
    w_out_bf = w_out.astype(BF16)
    w_q_bf = peer_w_q.astype(BF16)
    u_bf = peer_u.astype(BF16)
    vt_bf = peer_v.astype(BF16).transpose(0, 2, 1)
    cols = (off[0], off[1], off[2], off[3], lr0)
    u_col, zg_col, zs_col = 0, s5_w, s5_w + d
    gla_p = gla_s = None
    re_p, im_p, re_s, im_s = [], [], [], []
    for l in range(depth):
        proj_a = _mm(x_bf, w_a, l, 1024, _col_tile(a_cols))
        proj_b = _mm(x_bf, w_b, l, 1024, 1024)

        o_gla, gla_p = _gla(proj_a, wa2[l], gla_b_a[l], gla_norm_g[l], None, l, gla_p, None, depth=depth, row0=0,
                            n_rows=t_all, nb=pb, seq_len=plen, c=min(GLA_CHUNK, plen), nseq=1, cols=cols)
        o_gla, gla_s = _gla(proj_a, wa2[l], gla_b_a[l], gla_norm_g[l], state_gla, l, gla_s, o_gla, depth=depth,
                            row0=tp, n_rows=t_all, nb=sb, seq_len=slen, c=min(GLA_CHUNK, slen), nseq=2, cols=cols)

        tables = _s5_tables(s5_a_re[l], s5_a_im[l], s5_log_dt[l], s5_b_re[l], s5_b_im[l], s5_c_re[l], s5_c_im[l],
                            S5_CHUNK)
        y_s5, (r_p, i_p), (r_s, i_s) = _s5_mixer(proj_b, u_col, tables, s5_d[l], state_s5_re[l], state_s5_im[l],
                                                 pb, plen)
        o_s5 = _glu(y_s5, w_glu_bf, l, s5_b_glu[l], 1024, 1024)

        merged = _branch_merge(o_gla, o_s5, w_br_bf, l, proj_b, zg_col, zs_col, 1024, 512)
        mix = _mm(merged, w_out_bf, l, 1024, 1024)
        x, x_bf, xt_bf = _deepnorm_ln(x, mix, ln1_g[l], ln1_b[l], alpha, 256, also_transposed=True)

        q = _mm(x_bf, w_q_bf, l, 1024, 1024)
        route = _peer_route(q, peer_key1[l], peer_key2[l], 256)
        ffn_t = _peer_ffn(xt_bf, u_bf, vt_bf, l, route, 512, 512)
        x, x_bf = _deepnorm_ln(x, ffn_t, ln2_g[l], ln2_b[l], alpha, 256, y_transposed=True)

        re_p.append(r_p)
        im_p.append(i_p)
        re_s.append(r_s)
        im_s.append(i_s)

    y_prompt = x[:tp].reshape(pb, plen, d)
    y_sample = x[tp:].reshape(sb, slen, d)
    return (y_prompt, y_sample, gla_p, jnp.stack(re_p), jnp.stack(im_p),
            gla_s, jnp.stack(re_s), jnp.stack(im_s))
```

```python
import functools

import jax
import jax.numpy as jnp
from jax import lax
from jax.experimental import pallas as pl
from jax.experimental.pallas import tpu as pltpu

F32 = jnp.float32
BF16 = jnp.bfloat16
HIGHEST = lax.Precision.HIGHEST

V7X_VMEM_BYTES = 64 * 1024 * 1024
VMEM_LIMIT_BYTES = V7X_VMEM_BYTES - 8 * 1024 * 1024
LANES = 128

GLA_HEADS = 4
GLA_RANK = 16
GLA_TAU = 16.0
GLA_CHUNK = 64
RMS_EPS = 1e-6
S5_CHUNK = 8
PEER_HEADS = 8
PEER_TOPK = 16
LN_EPS = 1e-5


def _params(*semantics, flags=None):
    return pltpu.CompilerParams(dimension_semantics=semantics, vmem_limit_bytes=VMEM_LIMIT_BYTES, flags=flags)


def _dot(a, b):
    return jnp.dot(a, b, preferred_element_type=F32)


def _dot_nt(a, b):
    return lax.dot_general(a, b, (((1,), (1,)), ((), ())), preferred_element_type=F32)


def _mm_kernel(a_ref, b_ref, o_ref):
    o_ref[...] = _dot(a_ref[...], b_ref[...].astype(BF16)).astype(o_ref.dtype)


def _layer_cols(w, layer, tn, row0=0):
    def spec(rows):
        return pl.BlockSpec((None, rows, tn), lambda i, j: (layer, row0, j))
    return spec


def _mm(a, w, layer, tm, tn, n=None, out_dtype=F32):
    m, k = a.shape
    n = n or w.shape[2]
    return pl.pallas_call(
        _mm_kernel,
        grid=(m // tm, n // tn),
        in_specs=[pl.BlockSpec((tm, k), lambda i, j: (i, 0)),
                  _layer_cols(w, layer, tn)(k)],
        out_specs=pl.BlockSpec((tm, tn), lambda i, j: (i, j)),
        out_shape=jax.ShapeDtypeStruct((m, n), out_dtype),
        compiler_params=_params("parallel", "parallel"),
        name="mm",
    )(a, w)


def _glu_kernel(y_ref, w_ref, b_ref, ytile_ref, o_ref):
    acc = _dot(y_ref[...].astype(BF16), w_ref[...]) + b_ref[...]
    o_ref[...] = (ytile_ref[...] * jax.nn.sigmoid(acc)).astype(o_ref.dtype)


def _glu(y, w, layer, b, tm, tn):
    m, k = y.shape
    n = w.shape[2]
    return pl.pallas_call(
        _glu_kernel,
        grid=(m // tm, n // tn),
        in_specs=[pl.BlockSpec((tm, k), lambda i, j: (i, 0)),
                  _layer_cols(w, layer, tn)(k),
                  pl.BlockSpec((1, tn), lambda i, j: (0, j)),
                  pl.BlockSpec((tm, tn), lambda i, j: (i, j))],
        out_specs=pl.BlockSpec((tm, tn), lambda i, j: (i, j)),
        out_shape=jax.ShapeDtypeStruct((m, n), BF16),
        compiler_params=_params("parallel", "parallel"),
        name="s5_glu",
    )(y, w, b.reshape(1, n), y)


def _branch_kernel(og_ref, os_ref, w1_ref, w2_ref, zg_ref, zs_ref, o_ref):
    m1 = _dot(og_ref[...], w1_ref[...])
    m2 = _dot(os_ref[...], w2_ref[...])
    o_ref[...] = (jax.nn.sigmoid(zg_ref[...]) * m1 + jax.nn.sigmoid(zs_ref[...]) * m2).astype(o_ref.dtype)


def _branch_merge(o_gla, o_s5, w, layer, proj, zg_col, zs_col, tm, tn):
    m, k = o_gla.shape
    assert o_s5.shape[1] == k and w.shape[1] == 2 * k
    n = w.shape[2]
    zg_blk, zs_blk = zg_col // tn, zs_col // tn
    return pl.pallas_call(
        _branch_kernel,
        grid=(m // tm, n // tn),
        in_specs=[pl.BlockSpec((tm, k), lambda i, j: (i, 0)),
                  pl.BlockSpec((tm, k), lambda i, j: (i, 0)),
                  _layer_cols(w, layer, tn, 0)(k),
                  _layer_cols(w, layer, tn, 1)(k),
                  pl.BlockSpec((tm, tn), lambda i, j: (i, zg_blk + j)),
                  pl.BlockSpec((tm, tn), lambda i, j: (i, zs_blk + j))],
        out_specs=pl.BlockSpec((tm, tn), lambda i, j: (i, j)),
        out_shape=jax.ShapeDtypeStruct((m, n), BF16),
        compiler_params=_params("parallel", "parallel"),
        name="branch_merge",
    )(o_gla, o_s5, w, w, proj, proj)


def _ln_kernel(x_ref, y_ref, g_ref, b_ref, o_ref, obf_ref, *maybe_obf_t_ref, alpha, y_transposed):
    y = y_ref[...].T if y_transposed else y_ref[...]
    h = alpha * x_ref[...] + y
    mu = jnp.mean(h, axis=-1, keepdims=True)
    d = h - mu
    var = jnp.mean(d * d, axis=-1, keepdims=True)
    o = d * lax.rsqrt(var + LN_EPS) * g_ref[...] + b_ref[...]
    o_ref[...] = o
    obf_ref[...] = o.astype(BF16)
    for obf_t_ref in maybe_obf_t_ref:
        obf_t_ref[...] = o.T.astype(BF16)


def _deepnorm_ln(x, y, g, b, alpha, tm, y_transposed=False, also_transposed=False):
    m, d = x.shape
    row = pl.BlockSpec((tm, d), lambda i: (i, 0))
    col = pl.BlockSpec((d, tm), lambda i: (0, i))
    vec = pl.BlockSpec((1, d), lambda i: (0, 0))
    out_specs = [row, row]
    out_shape = [jax.ShapeDtypeStruct((m, d), F32), jax.ShapeDtypeStruct((m, d), BF16)]
    if also_transposed:
        out_specs.append(col)
        out_shape.append(jax.ShapeDtypeStruct((d, m), BF16))
    return pl.pallas_call(
        functools.partial(_ln_kernel, alpha=alpha, y_transposed=y_transposed),
        grid=(m // tm,),
        in_specs=[row, col if y_transposed else row, vec, vec],
        out_specs=out_specs,
        out_shape=out_shape,
        compiler_params=_params("parallel"),
        name="deepnorm_ln",
    )(x, y, g.reshape(1, d), b.reshape(1, d))


def _gla_kernel(*refs, c, nseq, heads, scale, has_init, n_prev):
    refs = list(refs)
    q_ref, k_ref, v_ref, r_ref, alr_ref, wa2_ref, ba_ref, g_ref = refs[:8]
    rest = refs[8:]
    s0_ref = rest.pop(0) if has_init else None
    o_ref, s_ref = rest[n_prev:]
    dk = q_ref.shape[-1] // heads
    dv = v_ref.shape[-1] // heads
    kpad = max(c, LANES)
    qpad = max(c, 16)

    @pl.when(pl.program_id(1) == 0)
    def _():
        if has_init:
            s_ref[...] = s0_ref[...]
        else:
            s_ref[...] = jnp.zeros_like(s_ref)

    row = lax.broadcasted_iota(jnp.int32, (c, c), 0)
    col = lax.broadcasted_iota(jnp.int32, (c, c), 1)
    tri_cc = (row >= col).astype(F32)
    rowp = lax.broadcasted_iota(jnp.int32, (qpad, kpad), 0)
    colp = lax.broadcasted_iota(jnp.int32, (qpad, kpad), 1)
    causal = rowp >= colp
    eye = lax.broadcasted_iota(jnp.int32, (dk, dk), 0) == lax.broadcasted_iota(jnp.int32, (dk, dk), 1)
    eye_bf = eye.astype(BF16)

    def pad_rows(x, rows):
        if x.shape[0] == rows:
            return x
        return jnp.concatenate([x, jnp.zeros((rows - x.shape[0], x.shape[1]), x.dtype)], axis=0)

    outs = []
    for s in range(nseq):
        rows = slice(s * c, (s + 1) * c)
        z = jnp.dot(alr_ref[rows, :], wa2_ref[...], precision=HIGHEST, preferred_element_type=F32) + ba_ref[...]
        log_a = (jnp.minimum(z, 0.0) - jnp.log1p(jnp.exp(-jnp.abs(z)))) * (1.0 / GLA_TAU)
        if c >= LANES // 2:
            b = jnp.dot(tri_cc, log_a, precision=HIGHEST, preferred_element_type=F32)
        else:
            ridx = lax.broadcasted_iota(jnp.int32, log_a.shape, 0)
            b = jnp.zeros_like(log_a)
            for t in range(c):
                b = b + jnp.where(ridx >= t, log_a[t:t + 1, :], 0.0)
        b_last = b[c - 1:c, :]
        k = k_ref[rows, :]
        q_dec_all = (q_ref[rows, :] * scale * jnp.exp(b)).astype(BF16)
        k_inv_all = (k * jnp.exp(-b)).astype(BF16)
        k_end_all = (k * jnp.exp(b_last - b)).astype(BF16)
        v_all = v_ref[rows, :].astype(BF16)
        r = r_ref[rows, :]
        gate = r * jax.nn.sigmoid(r)
        o_heads = []
        for h in range(heads):
            kc = slice(h * dk, (h + 1) * dk)
            vc = slice(h * dv, (h + 1) * dv)
            q_dec = pad_rows(q_dec_all[:, kc], qpad)
            k_inv = pad_rows(k_inv_all[:, kc], kpad)
            k_end = pad_rows(k_end_all[:, kc], kpad)
            v_bf = pad_rows(v_all[:, vc], kpad)
            scores = jnp.where(causal, _dot_nt(q_dec, k_inv), 0.0).astype(BF16)
            state = s_ref[s, h]
            o = (_dot(q_dec, state.astype(BF16)) + _dot(scores, v_bf))[:c]
            k_end_t = _dot_nt(eye_bf, k_end).astype(BF16)
            decay_col = jnp.exp(jnp.sum(jnp.where(eye, b_last[:, kc], 0.0), axis=1, keepdims=True))
            s_ref[s, h] = decay_col * state + _dot(k_end_t, v_bf)
            o = o * lax.rsqrt(jnp.mean(o * o, axis=-1, keepdims=True) + RMS_EPS) * g_ref[...]
            o_heads.append(o * gate[:, vc])
        outs.append(jnp.concatenate(o_heads, axis=1))
    o_all = outs[0] if nseq == 1 else jnp.concatenate(outs, axis=0)
    o_ref[...] = o_all.astype(o_ref.dtype)


def _gla(proj, wa2, ba, norm_g, s0, layer, s_prev, o_prev, *, depth, row0, n_rows, nb, seq_len, c, nseq, cols):
    heads = GLA_HEADS
    dk_tot = wa2.shape[1]
    dk = dk_tot // heads
    dv = norm_g.shape[0]
    dv_tot = heads * dv
    nchunk = seq_len // c
    assert nseq == 1 or nchunk == 1
    rb = nseq * c
    blk0 = row0 // rb
    q_col, k_col, v_col, r_col, lr_col = cols

    def rows_map(col_blk):
        return lambda b, j: (blk0 + b * nchunk + j, col_blk)

    in_specs = [
        pl.BlockSpec((rb, dk_tot), rows_map(q_col // dk_tot)),
        pl.BlockSpec((rb, dk_tot), rows_map(k_col // dk_tot)),
        pl.BlockSpec((rb, dv_tot), rows_map(v_col // dv_tot)),
        pl.BlockSpec((rb, dv_tot), rows_map(r_col // dv_tot)),
        pl.BlockSpec((rb, LANES), rows_map(lr_col // LANES)),
        pl.BlockSpec((LANES, dk_tot), lambda b, j: (0, 0)),
        pl.BlockSpec((1, dk_tot), lambda b, j: (0, 0)),
        pl.BlockSpec((1, dv), lambda b, j: (0, 0)),
    ]
    args = [proj, proj, proj, proj, proj, wa2, ba.reshape(1, -1), norm_g.reshape(1, dv)]
    state_spec = pl.BlockSpec((None, nseq, heads, dk, dv), lambda b, j: (layer, b, 0, 0, 0))
    if s0 is not None:
        in_specs.append(state_spec)
        args.append(s0)
    aliases = {}
    for prev, out_idx in ((s_prev, 1), (o_prev, 0)):
        if prev is not None:
            aliases[len(args)] = out_idx
            in_specs.append(pl.BlockSpec(memory_space=pl.ANY))
            args.append(prev)
    kern = functools.partial(_gla_kernel, c=c, nseq=nseq, heads=heads, scale=dk ** -0.5,
                             has_init=s0 is not None, n_prev=len(aliases))
    return pl.pallas_call(
        kern,
        grid=(nb // nseq, nchunk),
        in_specs=in_specs,
        out_specs=[pl.BlockSpec((rb, dv_tot), rows_map(0)), state_spec],
        out_shape=[jax.ShapeDtypeStruct((n_rows, dv_tot), BF16),
                   jax.ShapeDtypeStruct((depth, nb, heads, dk, dv), F32)],
        input_output_aliases=aliases,
        compiler_params=_params("parallel", "arbitrary"),
        name="gla",
    )(*args)


def _s5_tables(a_re, a_im, log_dt, b_re, b_im, c_re, c_im, c):
    dt = jnp.exp(log_dt)[:, None]
    mag = jnp.exp(a_re * dt)
    ar, ai = mag * jnp.cos(a_im * dt), mag * jnp.sin(a_im * dt)
    den = a_re * a_re + a_im * a_im
    cr = ((ar - 1.0) * a_re + ai * a_im) / den
    ci = (ai * a_re - (ar - 1.0) * a_im) / den
    bbr = cr[..., None] * b_re - ci[..., None] * b_im
    bbi = cr[..., None] * b_im + ci[..., None] * b_re
    tau = jnp.arange(c + 1, dtype=F32)[:, None, None]
    pmag = jnp.exp(tau * (a_re * dt))
    pw_r, pw_i = pmag * jnp.cos(tau * (a_im * dt)), pmag * jnp.sin(tau * (a_im * dt))
    ca_r = c_re[None] * pw_r[:, :, None, :] - c_im[None] * pw_i[:, :, None, :]
    ca_i = c_re[None] * pw_i[:, :, None, :] + c_im[None] * pw_r[:, :, None, :]
    kern = (jnp.einsum('tgmp,gpn->tgmn', ca_r[:c], bbr, precision=HIGHEST)
            - jnp.einsum('tgmp,gpn->tgmn', ca_i[:c], bbi, precision=HIGHEST))
    g, m, n = kern.shape[1:]
    p = a_re.shape[1]
    gpb = LANES // n
    nb = g // gpb

    def block_diag(x):
        lead, _, r, w = x.shape
        copies = jnp.tile(jnp.eye(w, dtype=F32), (1, gpb))
        wide = jnp.dot(x.reshape(-1, w), copies, precision=HIGHEST).reshape(lead, nb, gpb * r, gpb * w)
        return wide * jnp.kron(jnp.eye(gpb, dtype=F32), jnp.ones((r, w), F32))

    lag_blocks = block_diag(kern.transpose(0, 1, 3, 2))
    zero_block = jnp.zeros_like(lag_blocks[0])
    toep = jnp.concatenate(
        [jnp.concatenate([lag_blocks[t - s] if t >= s else zero_block for s in range(c)], axis=1)
         for t in range(c)], axis=2)
    back = (c - 1.0) - tau[:c]
    rev_mag = jnp.exp(back * (a_re * dt))
    rev_r, rev_i = rev_mag * jnp.cos(back * (a_im * dt)), rev_mag * jnp.sin(back * (a_im * dt))
    inj_r = rev_r[..., None] * bbr[None] - rev_i[..., None] * bbi[None]
    inj_i = rev_r[..., None] * bbi[None] + rev_i[..., None] * bbr[None]
    inj = jnp.concatenate([block_diag(x.transpose(0, 1, 3, 2)) for x in (inj_r, inj_i)], axis=3)
    inj = inj.transpose(1, 0, 2, 3).reshape(nb, c * LANES, 2 * gpb * p)
    read = jnp.concatenate(
        [jnp.concatenate(list(block_diag(x.transpose(0, 1, 3, 2))), axis=2)
         for x in (ca_r[1:], -ca_i[1:])], axis=1)
    step_r, step_i = pw_r[c].reshape(nb, 1, gpb * p), pw_i[c].reshape(nb, 1, gpb * p)
    step_same = jnp.concatenate([step_r, step_r], axis=-1)
    step_swap = jnp.concatenate([-step_i, step_i], axis=-1)
    return toep.astype(BF16), inj.astype(BF16), read.astype(BF16), step_same, step_swap


def _chunk_rows(u_ref, c):
    r = u_ref.shape[0] // c
    return jnp.concatenate([u_ref[pl.ds(s, r, stride=c), :].astype(BF16) for s in range(c)], axis=1)


def _s5_inject_kernel(u_ref, w_ref, e_ref, *, c):
    e_ref[...] = _dot(_chunk_rows(u_ref, c), w_ref[...])


def _s5_inject(proj, u_col, inj, c):
    t_tokens = proj.shape[0]
    nb, k, w = inj.shape
    return pl.pallas_call(
        functools.partial(_s5_inject_kernel, c=c),
        grid=(nb,),
        in_specs=[pl.BlockSpec((t_tokens, LANES), lambda b: (0, u_col // LANES + b)),
                  pl.BlockSpec((None, k, w), lambda b: (b, 0, 0))],
        out_specs=pl.BlockSpec((None, t_tokens // c, w), lambda b: (b, 0, 0)),
        out_shape=jax.ShapeDtypeStruct((nb, t_tokens // c, w), F32),
        compiler_params=_params("parallel"),
        name="s5_inject",
    )(proj, inj)


def _s5_scan_kernel(e_ref, x0_ref, same_ref, swap_ref, xprev_ref, xfin_p_ref, xfin_s_ref, *, n_seq, n_chunk):
    half = e_ref.shape[-1] // 2
    same = same_ref[...]
    swap = swap_ref[...]

    def advance(x, e):
        x_swapped = jnp.concatenate([x[:, half:], x[:, :half]], axis=1)
        return same * x + swap * x_swapped + e

    def body(j, xs):
        new = []
        for q, x in enumerate(xs):
            row = q * n_chunk + j
            xprev_ref[pl.ds(row, 1), :] = x
            new.append(advance(x, e_ref[pl.ds(row, 1), :]))
        return tuple(new)

    zero = jnp.zeros((1, e_ref.shape[-1]), F32)
    xs = lax.fori_loop(0, n_chunk, body, (zero,) * n_seq)
    xfin_p_ref[...] = jnp.zeros_like(xfin_p_ref)
    for q, x in enumerate(xs):
        xfin_p_ref[q:q + 1, :] = x
    rp = n_seq * n_chunk
    x0 = x0_ref[...]
    xprev_ref[rp:, :] = x0
    xfin_s_ref[...] = advance(x0, e_ref[rp:, :])


def _s5_scan(e, x0_s, same, swap, n_seq, n_chunk):
    nb, r, w = e.shape
    ns = x0_s.shape[1]
    assert r == n_seq * n_chunk + ns
    pad_seq = -(-n_seq // 8) * 8
    vec = pl.BlockSpec((None, 1, w), lambda b: (b, 0, 0))
    return pl.pallas_call(
        functools.partial(_s5_scan_kernel, n_seq=n_seq, n_chunk=n_chunk),
        grid=(nb,),
        in_specs=[pl.BlockSpec((None, r, w), lambda b: (b, 0, 0)),
                  pl.BlockSpec((None, ns, w), lambda b: (b, 0, 0)), vec, vec],
        out_specs=[pl.BlockSpec((None, r, w), lambda b: (b, 0, 0)),
                   pl.BlockSpec((None, pad_seq, w), lambda b: (b, 0, 0)),
                   pl.BlockSpec((None, ns, w), lambda b: (b, 0, 0))],
        out_shape=[jax.ShapeDtypeStruct((nb, r, w), F32), jax.ShapeDtypeStruct((nb, pad_seq, w), F32),
                   jax.ShapeDtypeStruct((nb, ns, w), F32)],
        compiler_params=_params("parallel"),
        name="s5_scan",
    )(e, x0_s, same, swap)


def _s5_out_kernel(u_ref, xp_ref, toep_ref, read_ref, d_ref, y_ref, *, c):
    r = u_ref.shape[0] // c
    y = _dot(_chunk_rows(u_ref, c), toep_ref[...]) + _dot(xp_ref[...].astype(BF16), read_ref[...])
    for t in range(c):
        rows = pl.ds(t, r, stride=c)
        y_ref[rows, :] = jax.nn.gelu(y[:, t * LANES:(t + 1) * LANES] + d_ref[...] * u_ref[rows, :])


def _s5_out(proj, u_col, xprev, toep, read, d_skip, c):
    t_tokens = proj.shape[0]
    nb, r, w = xprev.shape
    k = toep.shape[1]
    u_blk = u_col // LANES
    return pl.pallas_call(
        functools.partial(_s5_out_kernel, c=c),
        grid=(nb,),
        in_specs=[pl.BlockSpec((t_tokens, LANES), lambda b: (0, u_blk + b)),
                  pl.BlockSpec((None, r, w), lambda b: (b, 0, 0)),
                  pl.BlockSpec((None, k, k), lambda b: (b, 0, 0)),
                  pl.BlockSpec((None, w, k), lambda b: (b, 0, 0)),
                  pl.BlockSpec((1, LANES), lambda b: (0, b))],
        out_specs=pl.BlockSpec((t_tokens, LANES), lambda b: (0, b)),
        out_shape=jax.ShapeDtypeStruct((t_tokens, nb * LANES), F32),
        compiler_params=_params("parallel"),
        name="s5_out",
    )(proj, xprev, toep, read, d_skip.reshape(1, -1))


def _s5_mixer(proj, u_col, tables, d_skip, st_re, st_im, n_prompt_seq, prompt_len):
    toep, inj, read, step_same, step_swap = tables
    c = S5_CHUNK
    nb = toep.shape[0]
    ns, groups, p = st_re.shape
    gpb = groups // nb

    def to_blocks(st):
        return st.reshape(ns, nb, gpb * p).transpose(1, 0, 2)

    def from_blocks(x):
        return x.transpose(1, 0, 2).reshape(x.shape[1], groups, p)

    e = _s5_inject(proj, u_col, inj, c)
    x0_s = jnp.concatenate([to_blocks(st_re), to_blocks(st_im)], axis=-1)
    xprev, xfin_p, xfin_s = _s5_scan(e, x0_s, step_same, step_swap, n_prompt_seq, prompt_len // c)
    y = _s5_out(proj, u_col, xprev, toep, read, d_skip, c)
    half = gpb * p
    xfin_p = xfin_p[:, :n_prompt_seq]
    return (y, (from_blocks(xfin_p[..., :half]), from_blocks(xfin_p[..., half:])),
            (from_blocks(xfin_s[..., :half]), from_blocks(xfin_s[..., half:])))


def _peer_route_kernel(q_ref, k1_ref, k2_ref, s1_ref, w1_ref, s2_ref, e2_ref, tau_ref):
    half = k1_ref.shape[1]
    neg = jnp.float32(-jnp.inf)

    def top_values(x):
        vals = []
        for _ in range(PEER_TOPK):
            m = jnp.max(x, axis=0, keepdims=True)
            vals.append(m)
            x = jnp.where(x == m, neg, x)
        return vals

    for h in range(PEER_HEADS):
        q1 = q_ref[:, (2 * h) * half:(2 * h + 1) * half]
        q2 = q_ref[:, (2 * h + 1) * half:(2 * h + 2) * half]
        s1 = lax.dot_general(k1_ref[...], q1, (((1,), (1,)), ((), ())), precision=HIGHEST,
                             preferred_element_type=F32)
        s2 = lax.dot_general(k2_ref[...], q2, (((1,), (1,)), ((), ())), precision=HIGHEST,
                             preferred_element_type=F32)
        a = top_values(s1)
        b = jnp.concatenate(top_values(s2), axis=0)
        b0 = b[0:1]
        cand = [a[i] + b[:PEER_TOPK // (i + 1)] for i in range(PEER_TOPK)]
        n_cand = sum(x.shape[0] for x in cand)
        cand.append(jnp.full((-n_cand % 8, s1.shape[1]), neg, F32))
        top = top_values(jnp.concatenate(cand, axis=0))
        zsum = top[0] * 0.0
        for m in top:
            zsum = zsum + jnp.exp(m - top[0])
        s1_ref[h] = s1
        s2_ref[h] = s2
        w1_ref[h] = jnp.exp(s1 - a[0]) / zsum
        e2_ref[h] = jnp.exp(s2 - b0)
        tau_ref[h] = top[-1]


def _peer_route(q, key1, key2, tm):
    t_tokens = q.shape[0]
    nk, half = key1.shape
    big = pl.BlockSpec((PEER_HEADS, nk, tm), lambda i: (0, 0, i))
    big_shape = jax.ShapeDtypeStruct((PEER_HEADS, nk, t_tokens), F32)
    return pl.pallas_call(
        _peer_route_kernel,
        grid=(t_tokens // tm,),
        in_specs=[pl.BlockSpec((tm, q.shape[1]), lambda i: (i, 0)),
                  pl.BlockSpec((nk, half), lambda i: (0, 0)),
                  pl.BlockSpec((nk, half), lambda i: (0, 0))],
        out_specs=[big, big, big, big, pl.BlockSpec((PEER_HEADS, 1, tm), lambda i: (0, 0, i))],
        out_shape=[big_shape, big_shape, big_shape, big_shape,
                   jax.ShapeDtypeStruct((PEER_HEADS, 1, t_tokens), F32)],
        compiler_params=_params("parallel"),
        name="peer_route",
    )(q, key1, key2)


def _peer_ffn_kernel(xt_ref, u_ref, vt_ref, s1_ref, w1_ref, s2_ref, e2_ref, tau_ref, o_ref, p_ref, h_ref, g_ref,
                     *, ni, n_et):
    step = pl.program_id(1)
    et = jnp.minimum(step, n_et - 1)
    slot = step % 2

    @pl.when(step == 0)
    def _():
        o_ref[...] = jnp.zeros_like(o_ref)
        p_ref[...] = jnp.zeros_like(p_ref)

    nk = s2_ref.shape[1]
    te, tm = h_ref.shape
    d = o_ref.shape[0]
    chunks = [(ii, tc) for ii in range(ni) for tc in range(tm // LANES)]

    def block(ii, tc):
        return slice(ii * nk, (ii + 1) * nk), slice(tc * LANES, (tc + 1) * LANES)

    def gate_chunk(ii, tc):
        rows, lanes = block(ii, tc)
        i = et * ni + ii
        g = None
        for h in range(PEER_HEADS):
            s1_row = s1_ref[h, pl.ds(i, 1), :][:, lanes]
            w1_row = w1_ref[h, pl.ds(i, 1), :][:, lanes]
            hit = (s1_row + s2_ref[h, :, lanes]) >= tau_ref[h, :, lanes]
            term = jnp.where(hit, e2_ref[h, :, lanes] * w1_row, 0.0)
            g = term if g is None else g + term
        g_ref[rows, lanes] = g

    def finish_chunk(ii, tc):
        rows, lanes = block(ii, tc)
        p_ref[slot, rows, lanes] = (g_ref[rows, lanes] * jax.nn.gelu(h_ref[rows, lanes])).astype(BF16)

    quarter = len(chunks) // 4
    half = tm // 2
    for c in chunks[:quarter]:
        gate_chunk(*c)
    h_ref[:, :half] = _dot(u_ref[...], xt_ref[:, :half])
    for c in chunks[quarter:2 * quarter]:
        gate_chunk(*c)
    h_ref[:, half:] = _dot(u_ref[...], xt_ref[:, half:])
    p_prev = p_ref[1 - slot]
    vector_work = [[(gate_chunk, c) for c in chunks[2 * quarter:3 * quarter]],
                   [(gate_chunk, c) for c in chunks[3 * quarter:]],
                   [(finish_chunk, c) for c in chunks[:2 * quarter]],
                   [(finish_chunk, c) for c in chunks[2 * quarter:]]]
    dq = d // len(vector_work)
    for k, work in enumerate(vector_work):
        for fn, c in work:
            fn(*c)
        rows = slice(k * dq, (k + 1) * dq)
        o_ref[rows, :] += _dot(vt_ref[rows, :], p_prev)


def _peer_ffn(xt_bf, u_tab, vt_tab, layer, route, tm, te):
    s1, w1, s2, e2, tau = route
    d, t_tokens = xt_bf.shape
    n_exp = u_tab.shape[1]
    nk = s2.shape[1]
    ni = te // nk
    n_et = n_exp // te
    once = pl.Buffered(1)
    tok = pl.BlockSpec((PEER_HEADS, nk, tm), lambda i, j: (0, 0, i), pipeline_mode=once)
    return pl.pallas_call(
        functools.partial(_peer_ffn_kernel, ni=ni, n_et=n_et),
        grid=(t_tokens // tm, n_et + 1),
        in_specs=[pl.BlockSpec((d, tm), lambda i, j: (0, i), pipeline_mode=once),
                  pl.BlockSpec((None, te, d), lambda i, j: (layer, jnp.minimum(j, n_et - 1), 0)),
                  pl.BlockSpec((None, d, te), lambda i, j: (layer, 0, jnp.maximum(j - 1, 0))),
                  tok, tok, tok, tok,
                  pl.BlockSpec((PEER_HEADS, 1, tm), lambda i, j: (0, 0, i), pipeline_mode=once)],
        out_specs=pl.BlockSpec((d, tm), lambda i, j: (0, i)),
        out_shape=jax.ShapeDtypeStruct((d, t_tokens), F32),
        scratch_shapes=[pltpu.VMEM((2, te, tm), BF16), pltpu.VMEM((te, tm), F32), pltpu.VMEM((te, tm), F32)],
        compiler_params=_params("parallel", "arbitrary"),
        name="peer_ffn",
    )(xt_bf, u_tab, vt_tab, s1, w1, s2, e2, tau)


def kernel(x_prompt, x_sample, state_gla, state_s5_re, state_s5_im, ln1_g, ln1_b, w_in, gla_w_a2, gla_b_a, gla_norm_g, s5_a_re, s5_a_im, s5_log_dt, s5_b_re, s5_b_im, s5_c_re, s5_c_im, s5_d, s5_w_glu, s5_b_glu, w_branch, w_out, ln2_g, ln2_b, peer_w_q, peer_key1, peer_key2, peer_u, peer_v):
    depth = w_in.shape[0]
    pb, plen, d = x_prompt.shape
    sb, slen, _ = x_sample.shape
    tp, ts = pb * plen, sb * slen
    t_all = tp + ts
    dk_tot = gla_w_a2.shape[2]
    dv_tot = GLA_HEADS * gla_norm_g.shape[1]
    s5_w = s5_d.shape[1]
    alpha = (2 * depth) ** 0.25
    widths = (dk_tot, dk_tot, dv_tot, dv_tot, GLA_RANK, s5_w, d, d)
    off = [0]
    for wdt in widths:
        off.append(off[-1] + wdt)
    lr0, lr1 = off[4], off[5]
    x = jnp.concatenate([x_prompt.reshape(tp, d), x_sample.reshape(ts, d)], axis=0)
    x_bf = x.astype(BF16)
    a_tile = 4 * LANES
    a_cols = -(-(lr0 + LANES) // a_tile) * a_tile
    w_b = w_in[:, :, lr1:].astype(BF16)
    wa2 = jnp.pad(gla_w_a2, ((0, 0), (0, LANES - GLA_RANK), (0, 0)))
    w_glu_bf = s5_w_glu.astype(BF16)
    w_br_bf = w_branch.astype(BF16)
    w_out_bf = w_out.astype(BF16)
    w_q_bf = peer_w_q.astype(BF16)
    u_bf = peer_u.astype(BF16)
    vt_bf = peer_v.astype(BF16).transpose(0, 2, 1)
    cols = (off[0], off[1], off[2], off[3], lr0)
    u_col, zg_col, zs_col = 0, s5_w, s5_w + d
    gla_p = gla_s = None
    re_p, im_p, re_s, im_s = [], [], [], []
    for l in range(depth):
        proj_a = _mm(x_bf, w_in, l, 1024, a_tile, n=a_cols)
        proj_b = _mm(x_bf, w_b, l, 1024, 1024)

        o_gla, gla_p = _gla(proj_a, wa2[l], gla_b_a[l], gla_norm_g[l], None, l, gla_p, None, depth=depth, row0=0,
                            n_rows=t_all, nb=pb, seq_len=plen, c=min(GLA_CHUNK, plen), nseq=1, cols=cols)
        o_gla, gla_s = _gla(proj_a, wa2[l], gla_b_a[l], gla_norm_g[l], state_gla, l, gla_s, o_gla, depth=depth,
                            row0=tp, n_rows=t_all, nb=sb, seq_len=slen, c=min(GLA_CHUNK, slen), nseq=2, cols=cols)

        tables = _s5_tables(s5_a_re[l], s5_a_im[l], s5_log_dt[l], s5_b_re[l], s5_b_im[l], s5_c_re[l], s5_c_im[l],
                            S5_CHUNK)
        y_s5, (r_p, i_p), (r_s, i_s) = _s5_mixer(proj_b, u_col, tables, s5_d[l], state_s5_re[l], state_s5_im[l],
                                                 pb, plen)
        o_s5 = _glu(y_s5, w_glu_bf, l, s5_b_glu[l], 1024, 1024)

        merged = _branch_merge(o_gla, o_s5, w_br_bf, l, proj_b, zg_col, zs_col, 1024, 512)
        mix = _mm(merged, w_out_bf, l, 1024, 1024)
        x, x_bf, xt_bf = _deepnorm_ln(x, mix, ln1_g[l], ln1_b[l], alpha, 256, also_transposed=True)

        q = _mm(x_bf, w_q_bf, l, 1024, 1024)
        route = _peer_route(q, peer_key1[l], peer_key2[l], 256)
        ffn_t = _peer_ffn(xt_bf, u_bf, vt_bf, l, route, 512, 512)
        x, x_bf = _deepnorm_ln(x, ffn_t, ln2_g[l], ln2_b[l], alpha, 256, y_transposed=True)

        re_p.append(r_p)
        im_p.append(i_p)
        re_s.append(r_s)
        im_s.append(i_s)

    y_prompt = x[:tp].reshape(pb, plen, d)
    y_sample = x[tp:].reshape(sb, slen, d)
    return (y_prompt, y_sample, gla_p, jnp.stack(re_p), jnp.stack(im_p),
            gla_s, jnp.stack(re_s), jnp.stack(im_s))
```

```python
import functools

import jax
import jax.numpy as jnp
from jax import lax
from jax.experimental import pallas as pl
from jax.experimental.pallas import tpu as pltpu

F32 = jnp.float32
BF16 = jnp.bfloat16
HIGHEST = lax.Precision.HIGHEST

V7X_VMEM_BYTES = 64 * 1024 * 1024
VMEM_LIMIT_BYTES = V7X_VMEM_BYTES - 8 * 1024 * 1024
LANES = 128

GLA_HEADS = 4
GLA_RANK = 16
GLA_TAU = 16.0
GLA_CHUNK = 64
RMS_EPS = 1e-6
S5_CHUNK = 8
PEER_HEADS = 8
PEER_TOPK = 16
LN_EPS = 1e-5


def _params(*semantics, flags=None):
    return pltpu.CompilerParams(dimension_semantics=semantics, vmem_limit_bytes=VMEM_LIMIT_BYTES, flags=flags)


def _dot(a, b):
    return jnp.dot(a, b, preferred_element_type=F32)


def _dot_nt(a, b):
    return lax.dot_general(a, b, (((1,), (1,)), ((), ())), preferred_element_type=F32)


def _mm_kernel(a_ref, b_ref, o_ref):
    o_ref[...] = _dot(a_ref[...], b_ref[...].astype(BF16)).astype(o_ref.dtype)


def _layer_cols(w, layer, tn, row0=0):
    def spec(rows):
        return pl.BlockSpec((None, rows, tn), lambda i, j: (layer, row0, j))
    return spec


def _mm(a, w, layer, tm, tn, n=None, out_dtype=F32):
    m, k = a.shape
    n = n or w.shape[2]
    return pl.pallas_call(
        _mm_kernel,
        grid=(m // tm, n // tn),
        in_specs=[pl.BlockSpec((tm, k), lambda i, j: (i, 0)),
                  _layer_cols(w, layer, tn)(k)],
        out_specs=pl.BlockSpec((tm, tn), lambda i, j: (i, j)),
        out_shape=jax.ShapeDtypeStruct((m, n), out_dtype),
        compiler_params=_params("parallel", "parallel"),
        name="mm",
    )(a, w)


def _glu_kernel(y_ref, w_ref, b_ref, ytile_ref, o_ref):
    acc = _dot(y_ref[...].astype(BF16), w_ref[...].astype(BF16)) + b_ref[...]
    o_ref[...] = (ytile_ref[...] * jax.nn.sigmoid(acc)).astype(o_ref.dtype)


def _glu(y, w, layer, b, tm, tn):
    m, k = y.shape
    n = w.shape[2]
    return pl.pallas_call(
        _glu_kernel,
        grid=(m // tm, n // tn),
        in_specs=[pl.BlockSpec((tm, k), lambda i, j: (i, 0)),
                  _layer_cols(w, layer, tn)(k),
                  pl.BlockSpec((1, tn), lambda i, j: (0, j)),
                  pl.BlockSpec((tm, tn), lambda i, j: (i, j))],
        out_specs=pl.BlockSpec((tm, tn), lambda i, j: (i, j)),
        out_shape=jax.ShapeDtypeStruct((m, n), BF16),
        compiler_params=_params("parallel", "parallel"),
        name="s5_glu",
    )(y, w, b.reshape(1, n), y)


def _lane_shifted(a, b, shift):
    if shift == 0:
        return a
    return jnp.concatenate([a[:, shift:], b[:, :shift]], axis=1)


def _branch_kernel(og_ref, os_ref, w1_ref, w2_ref, zg_ref, zg_next_ref, zs_ref, zs_next_ref, o_ref, *, shift):
    m1 = _dot(og_ref[...], w1_ref[...].astype(BF16))
    m2 = _dot(os_ref[...], w2_ref[...].astype(BF16))
    zg = _lane_shifted(zg_ref[...], zg_next_ref[...], shift)
    zs = _lane_shifted(zs_ref[...], zs_next_ref[...], shift)
    o_ref[...] = (jax.nn.sigmoid(zg) * m1 + jax.nn.sigmoid(zs) * m2).astype(o_ref.dtype)


def _branch_merge(o_gla, o_s5, w, layer, proj, zg_col, zs_col, tm, tn):
    m, k = o_gla.shape
    assert o_s5.shape[1] == k and w.shape[1] == 2 * k
    n = w.shape[2]
    shift = zg_col % LANES
    assert zs_col % LANES == shift and (zg_col - shift) % tn == 0 and (zs_col - shift) % tn == 0

    def gate_specs(col):
        blk, nxt, per = (col - shift) // tn, (col - shift) // LANES, tn // LANES
        return [pl.BlockSpec((tm, tn), lambda i, j: (i, blk + j)),
                pl.BlockSpec((tm, LANES), lambda i, j: (i, nxt + per * (j + 1)))]

    return pl.pallas_call(
        functools.partial(_branch_kernel, shift=shift),
        grid=(m // tm, n // tn),
        in_specs=[pl.BlockSpec((tm, k), lambda i, j: (i, 0)),
                  pl.BlockSpec((tm, k), lambda i, j: (i, 0)),
                  _layer_cols(w, layer, tn, 0)(k),
                  _layer_cols(w, layer, tn, 1)(k)] + gate_specs(zg_col) + gate_specs(zs_col),
        out_specs=pl.BlockSpec((tm, tn), lambda i, j: (i, j)),
        out_shape=jax.ShapeDtypeStruct((m, n), BF16),
        compiler_params=_params("parallel", "parallel"),
        name="branch_merge",
    )(o_gla, o_s5, w, w, proj, proj, proj, proj)


def _ln_kernel(x_ref, y_ref, g_ref, b_ref, o_ref, obf_ref, *maybe_obf_t_ref, alpha, y_transposed):
    y = y_ref[...].T if y_transposed else y_ref[...]
    h = alpha * x_ref[...] + y
    mu = jnp.mean(h, axis=-1, keepdims=True)
    d = h - mu
    var = jnp.mean(d * d, axis=-1, keepdims=True)
    o = d * lax.rsqrt(var + LN_EPS) * g_ref[...] + b_ref[...]
    o_ref[...] = o
    obf_ref[...] = o.astype(BF16)
    for obf_t_ref in maybe_obf_t_ref:
        obf_t_ref[...] = o.T.astype(BF16)


def _deepnorm_ln(x, y, g, b, alpha, tm, y_transposed=False, also_transposed=False):
    m, d = x.shape
    row = pl.BlockSpec((tm, d), lambda i: (i, 0))
    col = pl.BlockSpec((d, tm), lambda i: (0, i))
    vec = pl.BlockSpec((1, d), lambda i: (0, 0))
    out_specs = [row, row]
    out_shape = [jax.ShapeDtypeStruct((m, d), F32), jax.ShapeDtypeStruct((m, d), BF16)]
    if also_transposed:
        out_specs.append(col)
        out_shape.append(jax.ShapeDtypeStruct((d, m), BF16))
    return pl.pallas_call(
        functools.partial(_ln_kernel, alpha=alpha, y_transposed=y_transposed),
        grid=(m // tm,),
        in_specs=[row, col if y_transposed else row, vec, vec],
        out_specs=out_specs,
        out_shape=out_shape,
        compiler_params=_params("parallel"),
        name="deepnorm_ln",
    )(x, y, g.reshape(1, d), b.reshape(1, d))


def _gla_kernel(*refs, c, nseq, heads, scale, has_init, n_prev):
    refs = list(refs)
    q_ref, k_ref, v_ref, r_ref, alr_ref, wa2_ref, ba_ref, g_ref = refs[:8]
    rest = refs[8:]
    s0_ref = rest.pop(0) if has_init else None
    o_ref, s_ref = rest[n_prev:]
    dk = q_ref.shape[-1] // heads
    dv = v_ref.shape[-1] // heads
    kpad = max(c, LANES)
    qpad = max(c, 16)

    @pl.when(pl.program_id(1) == 0)
    def _():
        if has_init:
            s_ref[...] = s0_ref[...]
        else:
            s_ref[...] = jnp.zeros_like(s_ref)

    row = lax.broadcasted_iota(jnp.int32, (c, c), 0)
    col = lax.broadcasted_iota(jnp.int32, (c, c), 1)
    tri_cc = (row >= col).astype(F32)
    rowp = lax.broadcasted_iota(jnp.int32, (qpad, kpad), 0)
    colp = lax.broadcasted_iota(jnp.int32, (qpad, kpad), 1)
    causal = rowp >= colp
    eye = lax.broadcasted_iota(jnp.int32, (dk, dk), 0) == lax.broadcasted_iota(jnp.int32, (dk, dk), 1)
    eye_bf = eye.astype(BF16)

    def pad_rows(x, rows):
        if x.shape[0] == rows:
            return x
        return jnp.concatenate([x, jnp.zeros((rows - x.shape[0], x.shape[1]), x.dtype)], axis=0)

    outs = []
    for s in range(nseq):
        rows = slice(s * c, (s + 1) * c)
        z = jnp.dot(alr_ref[rows, :], wa2_ref[...], precision=HIGHEST, preferred_element_type=F32) + ba_ref[...]
        log_a = (jnp.minimum(z, 0.0) - jnp.log1p(jnp.exp(-jnp.abs(z)))) * (1.0 / GLA_TAU)
        if c >= LANES // 2:
            b = jnp.dot(tri_cc, log_a, precision=HIGHEST, preferred_element_type=F32)
        else:
            ridx = lax.broadcasted_iota(jnp.int32, log_a.shape, 0)
            b = jnp.zeros_like(log_a)
            for t in range(c):
                b = b + jnp.where(ridx >= t, log_a[t:t + 1, :], 0.0)
        b_last = b[c - 1:c, :]
        k = k_ref[rows, :]
        q_dec_all = (q_ref[rows, :] * scale * jnp.exp(b)).astype(BF16)
        k_inv_all = (k * jnp.exp(-b)).astype(BF16)
        k_end_all = (k * jnp.exp(b_last - b)).astype(BF16)
        v_all = v_ref[rows, :].astype(BF16)
        r = r_ref[rows, :]
        gate = r * jax.nn.sigmoid(r)
        o_heads = []
        for h in range(heads):
            kc = slice(h * dk, (h + 1) * dk)
            vc = slice(h * dv, (h + 1) * dv)
            q_dec = pad_rows(q_dec_all[:, kc], qpad)
            k_inv = pad_rows(k_inv_all[:, kc], kpad)
            k_end = pad_rows(k_end_all[:, kc], kpad)
            v_bf = pad_rows(v_all[:, vc], kpad)
            scores = jnp.where(causal, _dot_nt(q_dec, k_inv), 0.0).astype(BF16)
            state = s_ref[s, h]
            o = (_dot(q_dec, state.astype(BF16)) + _dot(scores, v_bf))[:c]
            k_end_t = _dot_nt(eye_bf, k_end).astype(BF16)
            decay_col = jnp.exp(jnp.sum(jnp.where(eye, b_last[:, kc], 0.0), axis=1, keepdims=True))
            s_ref[s, h] = decay_col * state + _dot(k_end_t, v_bf)
            o = o * lax.rsqrt(jnp.mean(o * o, axis=-1, keepdims=True) + RMS_EPS) * g_ref[...]
            o_heads.append(o * gate[:, vc])
        outs.append(jnp.concatenate(o_heads, axis=1))
    o_all = outs[0] if nseq == 1 else jnp.concatenate(outs, axis=0)
    o_ref[...] = o_all.astype(o_ref.dtype)


def _gla(proj, wa2, ba, norm_g, s0, layer, s_prev, o_prev, *, depth, row0, n_rows, nb, seq_len, c, nseq, cols):
    heads = GLA_HEADS
    dk_tot = wa2.shape[1]
    dk = dk_tot // heads
    dv = norm_g.shape[0]
    dv_tot = heads * dv
    nchunk = seq_len // c
    assert nseq == 1 or nchunk == 1
    rb = nseq * c
    blk0 = row0 // rb
    q_col, k_col, v_col, r_col, lr_col = cols

    def rows_map(col_blk):
        return lambda b, j: (blk0 + b * nchunk + j, col_blk)

    in_specs = [
        pl.BlockSpec((rb, dk_tot), rows_map(q_col // dk_tot)),
        pl.BlockSpec((rb, dk_tot), rows_map(k_col // dk_tot)),
        pl.BlockSpec((rb, dv_tot), rows_map(v_col // dv_tot)),
        pl.BlockSpec((rb, dv_tot), rows_map(r_col // dv_tot)),
        pl.BlockSpec((rb, LANES), rows_map(lr_col // LANES)),
        pl.BlockSpec((LANES, dk_tot), lambda b, j: (0, 0)),
        pl.BlockSpec((1, dk_tot), lambda b, j: (0, 0)),
        pl.BlockSpec((1, dv), lambda b, j: (0, 0)),
    ]
    args = [proj, proj, proj, proj, proj, wa2, ba.reshape(1, -1), norm_g.reshape(1, dv)]
    state_spec = pl.BlockSpec((None, nseq, heads, dk, dv), lambda b, j: (layer, b, 0, 0, 0))
    if s0 is not None:
        in_specs.append(state_spec)
        args.append(s0)
    aliases = {}
    for prev, out_idx in ((s_prev, 1), (o_prev, 0)):
        if prev is not None:
            aliases[len(args)] = out_idx
            in_specs.append(pl.BlockSpec(memory_space=pl.ANY))
            args.append(prev)
    kern = functools.partial(_gla_kernel, c=c, nseq=nseq, heads=heads, scale=dk ** -0.5,
                             has_init=s0 is not None, n_prev=len(aliases))
    return pl.pallas_call(
        kern,
        grid=(nb // nseq, nchunk),
        in_specs=in_specs,
        out_specs=[pl.BlockSpec((rb, dv_tot), rows_map(0)), state_spec],
        out_shape=[jax.ShapeDtypeStruct((n_rows, dv_tot), BF16),
                   jax.ShapeDtypeStruct((depth, nb, heads, dk, dv), F32)],
        input_output_aliases=aliases,
        compiler_params=_params("parallel", "arbitrary"),
        name="gla",
    )(*args)


def _s5_tables(a_re, a_im, log_dt, b_re, b_im, c_re, c_im, c):
    dt = jnp.exp(log_dt)[:, None]
    mag = jnp.exp(a_re * dt)
    ar, ai = mag * jnp.cos(a_im * dt), mag * jnp.sin(a_im * dt)
    den = a_re * a_re + a_im * a_im
    cr = ((ar - 1.0) * a_re + ai * a_im) / den
    ci = (ai * a_re - (ar - 1.0) * a_im) / den
    bbr = cr[..., None] * b_re - ci[..., None] * b_im
    bbi = cr[..., None] * b_im + ci[..., None] * b_re
    tau = jnp.arange(c + 1, dtype=F32)[:, None, None]
    pmag = jnp.exp(tau * (a_re * dt))
    pw_r, pw_i = pmag * jnp.cos(tau * (a_im * dt)), pmag * jnp.sin(tau * (a_im * dt))
    ca_r = c_re[None] * pw_r[:, :, None, :] - c_im[None] * pw_i[:, :, None, :]
    ca_i = c_re[None] * pw_i[:, :, None, :] + c_im[None] * pw_r[:, :, None, :]
    kern = (jnp.einsum('tgmp,gpn->tgmn', ca_r[:c], bbr, precision=HIGHEST)
            - jnp.einsum('tgmp,gpn->tgmn', ca_i[:c], bbi, precision=HIGHEST))
    g, m, n = kern.shape[1:]
    p = a_re.shape[1]
    gpb = LANES // n
    nb = g // gpb

    def block_diag(x):
        lead, _, r, w = x.shape
        copies = jnp.tile(jnp.eye(w, dtype=F32), (1, gpb))
        wide = jnp.dot(x.reshape(-1, w), copies, precision=HIGHEST).reshape(lead, nb, gpb * r, gpb * w)
        return wide * jnp.kron(jnp.eye(gpb, dtype=F32), jnp.ones((r, w), F32))

    lag_blocks = block_diag(kern.transpose(0, 1, 3, 2))
    zero_block = jnp.zeros_like(lag_blocks[0])
    toep = jnp.concatenate(
        [jnp.concatenate([lag_blocks[t - s] if t >= s else zero_block for s in range(c)], axis=1)
         for t in range(c)], axis=2)
    back = (c - 1.0) - tau[:c]
    rev_mag = jnp.exp(back * (a_re * dt))
    rev_r, rev_i = rev_mag * jnp.cos(back * (a_im * dt)), rev_mag * jnp.sin(back * (a_im * dt))
    inj_r = rev_r[..., None] * bbr[None] - rev_i[..., None] * bbi[None]
    inj_i = rev_r[..., None] * bbi[None] + rev_i[..., None] * bbr[None]
    inj = jnp.concatenate([block_diag(x.transpose(0, 1, 3, 2)) for x in (inj_r, inj_i)], axis=3)
    inj = inj.transpose(1, 0, 2, 3).reshape(nb, c * LANES, 2 * gpb * p)
    read = jnp.concatenate(
        [jnp.concatenate(list(block_diag(x.transpose(0, 1, 3, 2))), axis=2)
         for x in (ca_r[1:], -ca_i[1:])], axis=1)
    step_r, step_i = pw_r[c].reshape(nb, 1, gpb * p), pw_i[c].reshape(nb, 1, gpb * p)
    step_same = jnp.concatenate([step_r, step_r], axis=-1)
    step_swap = jnp.concatenate([-step_i, step_i], axis=-1)
    return toep.astype(BF16), inj.astype(BF16), read.astype(BF16), step_same, step_swap


def _chunk_tokens(u_ref, u_next_ref, c, shift):
    r = u_ref.shape[0] // c
    return [_lane_shifted(u_ref[pl.ds(s, r, stride=c), :], u_next_ref[pl.ds(s, r, stride=c), :], shift)
            for s in range(c)]


def _chunk_rows(tokens):
    return jnp.concatenate([x.astype(BF16) for x in tokens], axis=1)


def _u_specs(t_tokens, u_col):
    blk = u_col // LANES
    return [pl.BlockSpec((t_tokens, LANES), lambda b: (0, blk + b)),
            pl.BlockSpec((t_tokens, LANES), lambda b: (0, blk + b + 1))]


def _s5_inject_kernel(u_ref, u_next_ref, w_ref, e_ref, *, c, shift):
    e_ref[...] = _dot(_chunk_rows(_chunk_tokens(u_ref, u_next_ref, c, shift)), w_ref[...])


def _s5_inject(proj, u_col, inj, c):
    t_tokens = proj.shape[0]
    nb, k, w = inj.shape
    return pl.pallas_call(
        functools.partial(_s5_inject_kernel, c=c, shift=u_col % LANES),
        grid=(nb,),
        in_specs=_u_specs(t_tokens, u_col) + [pl.BlockSpec((None, k, w), lambda b: (b, 0, 0))],
        out_specs=pl.BlockSpec((None, t_tokens // c, w), lambda b: (b, 0, 0)),
        out_shape=jax.ShapeDtypeStruct((nb, t_tokens // c, w), F32),
        compiler_params=_params("parallel"),
        name="s5_inject",
    )(proj, proj, inj)


def _s5_scan_kernel(e_ref, x0_ref, same_ref, swap_ref, xprev_ref, xfin_p_ref, xfin_s_ref, *, n_seq, n_chunk):
    half = e_ref.shape[-1] // 2
    same = same_ref[...]
    swap = swap_ref[...]

    def advance(x, e):
        x_swapped = jnp.concatenate([x[:, half:], x[:, :half]], axis=1)
        return same * x + swap * x_swapped + e

    def body(j, xs):
        new = []
        for q, x in enumerate(xs):
            row = q * n_chunk + j
            xprev_ref[pl.ds(row, 1), :] = x
            new.append(advance(x, e_ref[pl.ds(row, 1), :]))
        return tuple(new)

    zero = jnp.zeros((1, e_ref.shape[-1]), F32)
    xs = lax.fori_loop(0, n_chunk, body, (zero,) * n_seq)
    xfin_p_ref[...] = jnp.zeros_like(xfin_p_ref)
    for q, x in enumerate(xs):
        xfin_p_ref[q:q + 1, :] = x
    rp = n_seq * n_chunk
    x0 = x0_ref[...]
    xprev_ref[rp:, :] = x0
    xfin_s_ref[...] = advance(x0, e_ref[rp:, :])


def _s5_scan(e, x0_s, same, swap, n_seq, n_chunk):
    nb, r, w = e.shape
    ns = x0_s.shape[1]
    assert r == n_seq * n_chunk + ns
    pad_seq = -(-n_seq // 8) * 8
    vec = pl.BlockSpec((None, 1, w), lambda b: (b, 0, 0))
    return pl.pallas_call(
        functools.partial(_s5_scan_kernel, n_seq=n_seq, n_chunk=n_chunk),
        grid=(nb,),
        in_specs=[pl.BlockSpec((None, r, w), lambda b: (b, 0, 0)),
                  pl.BlockSpec((None, ns, w), lambda b: (b, 0, 0)), vec, vec],
        out_specs=[pl.BlockSpec((None, r, w), lambda b: (b, 0, 0)),
                   pl.BlockSpec((None, pad_seq, w), lambda b: (b, 0, 0)),
                   pl.BlockSpec((None, ns, w), lambda b: (b, 0, 0))],
        out_shape=[jax.ShapeDtypeStruct((nb, r, w), F32), jax.ShapeDtypeStruct((nb, pad_seq, w), F32),
                   jax.ShapeDtypeStruct((nb, ns, w), F32)],
        compiler_params=_params("parallel"),
        name="s5_scan",
    )(e, x0_s, same, swap)


def _s5_out_kernel(u_ref, u_next_ref, xp_ref, toep_ref, read_ref, d_ref, y_ref, *, c, shift):
    r = u_ref.shape[0] // c
    tokens = _chunk_tokens(u_ref, u_next_ref, c, shift)
    y = _dot(_chunk_rows(tokens), toep_ref[...]) + _dot(xp_ref[...].astype(BF16), read_ref[...])
    for t in range(c):
        y_ref[pl.ds(t, r, stride=c), :] = jax.nn.gelu(y[:, t * LANES:(t + 1) * LANES] + d_ref[...] * tokens[t])


def _s5_out(proj, u_col, xprev, toep, read, d_skip, c):
    t_tokens = proj.shape[0]
    nb, r, w = xprev.shape
    k = toep.shape[1]
    return pl.pallas_call(
        functools.partial(_s5_out_kernel, c=c, shift=u_col % LANES),
        grid=(nb,),
        in_specs=_u_specs(t_tokens, u_col) + [
            pl.BlockSpec((None, r, w), lambda b: (b, 0, 0)),
            pl.BlockSpec((None, k, k), lambda b: (b, 0, 0)),
            pl.BlockSpec((None, w, k), lambda b: (b, 0, 0)),
            pl.BlockSpec((1, LANES), lambda b: (0, b))],
        out_specs=pl.BlockSpec((t_tokens, LANES), lambda b: (0, b)),
        out_shape=jax.ShapeDtypeStruct((t_tokens, nb * LANES), F32),
        compiler_params=_params("parallel"),
        name="s5_out",
    )(proj, proj, xprev, toep, read, d_skip.reshape(1, -1))


def _s5_mixer(proj, u_col, tables, d_skip, st_re, st_im, n_prompt_seq, prompt_len):
    toep, inj, read, step_same, step_swap = tables
    c = S5_CHUNK
    nb = toep.shape[0]
    ns, groups, p = st_re.shape
    gpb = groups // nb

    def to_blocks(st):
        return st.reshape(ns, nb, gpb * p).transpose(1, 0, 2)

    def from_blocks(x):
        return x.transpose(1, 0, 2).reshape(x.shape[1], groups, p)

    e = _s5_inject(proj, u_col, inj, c)
    x0_s = jnp.concatenate([to_blocks(st_re), to_blocks(st_im)], axis=-1)
    xprev, xfin_p, xfin_s = _s5_scan(e, x0_s, step_same, step_swap, n_prompt_seq, prompt_len // c)
    y = _s5_out(proj, u_col, xprev, toep, read, d_skip, c)
    half = gpb * p
    xfin_p = xfin_p[:, :n_prompt_seq]
    return (y, (from_blocks(xfin_p[..., :half]), from_blocks(xfin_p[..., half:])),
            (from_blocks(xfin_s[..., :half]), from_blocks(xfin_s[..., half:])))


def _peer_route_kernel(q_ref, k1_ref, k2_ref, s1_ref, w1_ref, s2_ref, e2_ref, tau_ref):
    half = k1_ref.shape[1]
    neg = jnp.float32(-jnp.inf)

    def top_values(x):
        vals = []
        for _ in range(PEER_TOPK):
            m = jnp.max(x, axis=0, keepdims=True)
            vals.append(m)
            x = jnp.where(x == m, neg, x)
        return vals

    for h in range(PEER_HEADS):
        q1 = q_ref[:, (2 * h) * half:(2 * h + 1) * half]
        q2 = q_ref[:, (2 * h + 1) * half:(2 * h + 2) * half]
        s1 = lax.dot_general(k1_ref[...], q1, (((1,), (1,)), ((), ())), precision=HIGHEST,
                             preferred_element_type=F32)
        s2 = lax.dot_general(k2_ref[...], q2, (((1,), (1,)), ((), ())), precision=HIGHEST,
                             preferred_element_type=F32)
        a = top_values(s1)
        b = jnp.concatenate(top_values(s2), axis=0)
        b0 = b[0:1]
        cand = [a[i] + b[:PEER_TOPK // (i + 1)] for i in range(PEER_TOPK)]
        n_cand = sum(x.shape[0] for x in cand)
        cand.append(jnp.full((-n_cand % 8, s1.shape[1]), neg, F32))
        top = top_values(jnp.concatenate(cand, axis=0))
        zsum = top[0] * 0.0
        for m in top:
            zsum = zsum + jnp.exp(m - top[0])
        s1_ref[h] = s1
        s2_ref[h] = s2
        w1_ref[h] = jnp.exp(s1 - a[0]) / zsum
        e2_ref[h] = jnp.exp(s2 - b0)
        tau_ref[h] = top[-1]


def _peer_route(q, key1, key2, tm):
    t_tokens = q.shape[0]
    nk, half = key1.shape
    big = pl.BlockSpec((PEER_HEADS, nk, tm), lambda i: (0, 0, i))
    big_shape = jax.ShapeDtypeStruct((PEER_HEADS, nk, t_tokens), F32)
    return pl.pallas_call(
        _peer_route_kernel,
        grid=(t_tokens // tm,),
        in_specs=[pl.BlockSpec((tm, q.shape[1]), lambda i: (i, 0)),
                  pl.BlockSpec((nk, half), lambda i: (0, 0)),
                  pl.BlockSpec((nk, half), lambda i: (0, 0))],
        out_specs=[big, big, big, big, pl.BlockSpec((PEER_HEADS, 1, tm), lambda i: (0, 0, i))],
        out_shape=[big_shape, big_shape, big_shape, big_shape,
                   jax.ShapeDtypeStruct((PEER_HEADS, 1, t_tokens), F32)],
        compiler_params=_params("parallel"),
        name="peer_route",
    )(q, key1, key2)


def _peer_ffn_kernel(xt_ref, u_ref, vt_ref, s1_ref, w1_ref, s2_ref, e2_ref, tau_ref, o_ref, p_ref, h_ref, g_ref,
                     *, ni, n_et):
    step = pl.program_id(1)
    et = jnp.minimum(step, n_et - 1)
    slot = step % 2

    @pl.when(step == 0)
    def _():
        o_ref[...] = jnp.zeros_like(o_ref)
        p_ref[...] = jnp.zeros_like(p_ref)

    nk = s2_ref.shape[1]
    te, tm = h_ref.shape
    d = o_ref.shape[0]
    chunks = [(ii, tc) for ii in range(ni) for tc in range(tm // LANES)]

    def block(ii, tc):
        return slice(ii * nk, (ii + 1) * nk), slice(tc * LANES, (tc + 1) * LANES)

    def gate_chunk(ii, tc):
        rows, lanes = block(ii, tc)
        i = et * ni + ii
        g = None
        for h in range(PEER_HEADS):
            s1_row = s1_ref[h, pl.ds(i, 1), :][:, lanes]
            w1_row = w1_ref[h, pl.ds(i, 1), :][:, lanes]
            hit = (s1_row + s2_ref[h, :, lanes]) >= tau_ref[h, :, lanes]
            term = jnp.where(hit, e2_ref[h, :, lanes] * w1_row, 0.0)
            g = term if g is None else g + term
        g_ref[rows, lanes] = g

    def finish_chunk(ii, tc):
        rows, lanes = block(ii, tc)
        p_ref[slot, rows, lanes] = (g_ref[rows, lanes] * jax.nn.gelu(h_ref[rows, lanes])).astype(BF16)

    quarter = len(chunks) // 4
    half = tm // 2
    for c in chunks[:quarter]:
        gate_chunk(*c)
    h_ref[:, :half] = _dot(u_ref[...], xt_ref[:, :half])
    for c in chunks[quarter:2 * quarter]:
        gate_chunk(*c)
    h_ref[:, half:] = _dot(u_ref[...], xt_ref[:, half:])
    p_prev = p_ref[1 - slot]
    vector_work = [[(gate_chunk, c) for c in chunks[2 * quarter:3 * quarter]],
                   [(gate_chunk, c) for c in chunks[3 * quarter:]],
                   [(finish_chunk, c) for c in chunks[:2 * quarter]],
                   [(finish_chunk, c) for c in chunks[2 * quarter:]]]
    dq = d // len(vector_work)
    for k, work in enumerate(vector_work):
        for fn, c in work:
            fn(*c)
        rows = slice(k * dq, (k + 1) * dq)
        o_ref[rows, :] += _dot(vt_ref[rows, :], p_prev)


def _peer_ffn(xt_bf, u_tab, vt_tab, layer, route, tm, te):
    s1, w1, s2, e2, tau = route
    d, t_tokens = xt_bf.shape
    n_exp = u_tab.shape[1]
    nk = s2.shape[1]
    ni = te // nk
    n_et = n_exp // te
    once = pl.Buffered(1)
    tok = pl.BlockSpec((PEER_HEADS, nk, tm), lambda i, j: (0, 0, i), pipeline_mode=once)
    return pl.pallas_call(
        functools.partial(_peer_ffn_kernel, ni=ni, n_et=n_et),
        grid=(t_tokens // tm, n_et + 1),
        in_specs=[pl.BlockSpec((d, tm), lambda i, j: (0, i), pipeline_mode=once),
                  pl.BlockSpec((None, te, d), lambda i, j: (layer, jnp.minimum(j, n_et - 1), 0)),
                  pl.BlockSpec((None, d, te), lambda i, j: (layer, 0, jnp.maximum(j - 1, 0))),
                  tok, tok, tok, tok,
                  pl.BlockSpec((PEER_HEADS, 1, tm), lambda i, j: (0, 0, i), pipeline_mode=once)],
        out_specs=pl.BlockSpec((d, tm), lambda i, j: (0, i)),
        out_shape=jax.ShapeDtypeStruct((d, t_tokens), F32),
        scratch_shapes=[pltpu.VMEM((2, te, tm), BF16), pltpu.VMEM((te, tm), F32), pltpu.VMEM((te, tm), F32)],
        compiler_params=_params("parallel", "arbitrary"),
        name="peer_ffn",
    )(xt_bf, u_tab, vt_tab, s1, w1, s2, e2, tau)


def kernel(x_prompt, x_sample, state_gla, state_s5_re, state_s5_im, ln1_g, ln1_b, w_in, gla_w_a2, gla_b_a, gla_norm_g, s5_a_re, s5_a_im, s5_log_dt, s5_b_re, s5_b_im, s5_c_re, s5_c_im, s5_d, s5_w_glu, s5_b_glu, w_branch, w_out, ln2_g, ln2_b, peer_w_q, peer_key1, peer_key2, peer_u, peer_v):
    depth = w_in.shape[0]
    pb, plen, d = x_prompt.shape
    sb, slen, _ = x_sample.shape
    tp, ts = pb * plen, sb * slen
    t_all = tp + ts
    dk_tot = gla_w_a2.shape[2]
    dv_tot = GLA_HEADS * gla_norm_g.shape[1]
    s5_w = s5_d.shape[1]
    alpha = (2 * depth) ** 0.25
    widths = (dk_tot, dk_tot, dv_tot, dv_tot, GLA_RANK, s5_w, d, d)
    off = [0]
    for wdt in widths:
        off.append(off[-1] + wdt)
    lr0, lr1 = off[4], off[5]
    x = jnp.concatenate([x_prompt.reshape(tp, d), x_sample.reshape(ts, d)], axis=0)
    x_bf = x.astype(BF16)
    proj_tile = 4 * LANES
    proj_cols = -(-off[-1] // proj_tile) * proj_tile
    wa2 = jnp.pad(gla_w_a2, ((0, 0), (0, LANES - GLA_RANK), (0, 0)))
    u_bf = peer_u.astype(BF16)
    vt_bf = peer_v.astype(BF16).transpose(0, 2, 1)
    cols = (off[0], off[1], off[2], off[3], lr0)
    u_col, zg_col, zs_col = off[5], off[6], off[7]
    gla_p = gla_s = None
    re_p, im_p, re_s, im_s = [], [], [], []
    for l in range(depth):
        proj = _mm(x_bf, w_in, l, 1024, proj_tile, n=proj_cols)

        o_gla, gla_p = _gla(proj, wa2[l], gla_b_a[l], gla_norm_g[l], None, l, gla_p, None, depth=depth, row0=0,
                            n_rows=t_all, nb=pb, seq_len=plen, c=min(GLA_CHUNK, plen), nseq=1, cols=cols)
        o_gla, gla_s = _gla(proj, wa2[l], gla_b_a[l], gla_norm_g[l], state_gla, l, gla_s, o_gla, depth=depth,
                            row0=tp, n_rows=t_all, nb=sb, seq_len=slen, c=min(GLA_CHUNK, slen), nseq=2, cols=cols)

        tables = _s5_tables(s5_a_re[l], s5_a_im[l], s5_log_dt[l], s5_b_re[l], s5_b_im[l], s5_c_re[l], s5_c_im[l],
                            S5_CHUNK)
        y_s5, (r_p, i_p), (r_s, i_s) = _s5_mixer(proj, u_col, tables, s5_d[l], state_s5_re[l], state_s5_im[l],
                                                 pb, plen)
        o_s5 = _glu(y_s5, s5_w_glu, l, s5_b_glu[l], 1024, 1024)

        merged = _branch_merge(o_gla, o_s5, w_branch, l, proj, zg_col, zs_col, 1024, 512)
        mix = _mm(merged, w_out, l, 1024, proj_tile)
        x, x_bf, xt_bf = _deepnorm_ln(x, mix, ln1_g[l], ln1_b[l], alpha, 256, also_transposed=True)

        q = _mm(x_bf, peer_w_q, l, 1024, proj_tile)
        route = _peer_route(q, peer_key1[l], peer_key2[l], 256)
        ffn_t = _peer_ffn(xt_bf, u_bf, vt_bf, l, route, 512, 512)
        x, x_bf = _deepnorm_ln(x, ffn_t, ln2_g[l], ln2_b[l], alpha, 256, y_transposed=True)

        re_p.append(r_p)
        im_p.append(i_p)
        re_s.append(r_s)
        im_s.append(i_s)

    y_prompt = x[:tp].reshape(pb, plen, d)
    y_sample = x[tp:].reshape(sb, slen, d)
    return (y_prompt, y_sample, gla_p, jnp.stack(re_p), jnp.stack(im_p),
            gla_s, jnp.stack(re_s), jnp.stack(im_s))
```

```python
import functools

import jax
import jax.numpy as jnp
from jax import lax
from jax.experimental import pallas as pl
from jax.experimental.pallas import tpu as pltpu

F32 = jnp.float32
BF16 = jnp.bfloat16
HIGHEST = lax.Precision.HIGHEST

V7X_VMEM_BYTES = 64 * 1024 * 1024
VMEM_LIMIT_BYTES = V7X_VMEM_BYTES - 8 * 1024 * 1024
LANES = 128

GLA_HEADS = 4
GLA_RANK = 16
GLA_TAU = 16.0
GLA_CHUNK = 64
RMS_EPS = 1e-6
S5_CHUNK = 8
PEER_HEADS = 8
PEER_TOPK = 16
LN_EPS = 1e-5


def _params(*semantics, flags=None):
    return pltpu.CompilerParams(dimension_semantics=semantics, vmem_limit_bytes=VMEM_LIMIT_BYTES, flags=flags)


def _dot(a, b):
    return jnp.dot(a, b, preferred_element_type=F32)


def _dot_nt(a, b):
    return lax.dot_general(a, b, (((1,), (1,)), ((), ())), preferred_element_type=F32)


def _mm_kernel(a_ref, b_ref, o_ref):
    o_ref[...] = _dot(a_ref[...], b_ref[...].astype(BF16)).astype(o_ref.dtype)


def _layer_cols(w, layer, tn, row0=0):
    def spec(rows):
        return pl.BlockSpec((None, rows, tn), lambda i, j: (layer, row0, j))
    return spec


def _mm_nt_kernel(a_ref, bt_ref, o_ref):
    o_ref[...] = _dot_nt(a_ref[...], bt_ref[...].astype(BF16)).astype(o_ref.dtype)


def _mm(a, w, layer, tm, tn, n=None, out_dtype=F32, w_transposed=False):
    m, k = a.shape
    n = n or w.shape[2]
    w_spec = (pl.BlockSpec((None, tn, k), lambda i, j: (layer, j, 0)) if w_transposed
              else _layer_cols(w, layer, tn)(k))
    return pl.pallas_call(
        _mm_nt_kernel if w_transposed else _mm_kernel,
        grid=(m // tm, n // tn),
        in_specs=[pl.BlockSpec((tm, k), lambda i, j: (i, 0)), w_spec],
        out_specs=pl.BlockSpec((tm, tn), lambda i, j: (i, j)),
        out_shape=jax.ShapeDtypeStruct((m, n), out_dtype),
        compiler_params=_params("parallel", "parallel"),
        name="mm",
    )(a, w)


def _glu_kernel(y_ref, w_ref, b_ref, ytile_ref, o_ref):
    acc = _dot(y_ref[...].astype(BF16), w_ref[...].astype(BF16)) + b_ref[...]
    o_ref[...] = (ytile_ref[...] * jax.nn.sigmoid(acc)).astype(o_ref.dtype)


def _glu(y, w, layer, b, tm, tn):
    m, k = y.shape
    n = w.shape[2]
    return pl.pallas_call(
        _glu_kernel,
        grid=(m // tm, n // tn),
        in_specs=[pl.BlockSpec((tm, k), lambda i, j: (i, 0)),
                  _layer_cols(w, layer, tn)(k),
                  pl.BlockSpec((1, tn), lambda i, j: (0, j)),
                  pl.BlockSpec((tm, tn), lambda i, j: (i, j))],
        out_specs=pl.BlockSpec((tm, tn), lambda i, j: (i, j)),
        out_shape=jax.ShapeDtypeStruct((m, n), BF16),
        compiler_params=_params("parallel", "parallel"),
        name="s5_glu",
    )(y, w, b.reshape(1, n), y)


def _lane_shifted(a, b, shift):
    if shift == 0:
        return a
    return jnp.concatenate([a[:, shift:], b[:, :shift]], axis=1)


def _branch_kernel(og_ref, os_ref, w1_ref, w2_ref, zg_ref, zg_next_ref, zs_ref, zs_next_ref, o_ref, *, shift):
    m1 = _dot(og_ref[...], w1_ref[...].astype(BF16))
    m2 = _dot(os_ref[...], w2_ref[...].astype(BF16))
    zg = _lane_shifted(zg_ref[...], zg_next_ref[...], shift)
    zs = _lane_shifted(zs_ref[...], zs_next_ref[...], shift)
    o_ref[...] = (jax.nn.sigmoid(zg) * m1 + jax.nn.sigmoid(zs) * m2).astype(o_ref.dtype)


def _branch_merge(o_gla, o_s5, w, layer, proj, zg_col, zs_col, tm, tn):
    m, k = o_gla.shape
    assert o_s5.shape[1] == k and w.shape[1] == 2 * k
    n = w.shape[2]
    shift = zg_col % LANES
    assert zs_col % LANES == shift and (zg_col - shift) % tn == 0 and (zs_col - shift) % tn == 0

    def gate_specs(col):
        blk, nxt, per = (col - shift) // tn, (col - shift) // LANES, tn // LANES
        return [pl.BlockSpec((tm, tn), lambda i, j: (i, blk + j)),
                pl.BlockSpec((tm, LANES), lambda i, j: (i, nxt + per * (j + 1)))]

    return pl.pallas_call(
        functools.partial(_branch_kernel, shift=shift),
        grid=(m // tm, n // tn),
        in_specs=[pl.BlockSpec((tm, k), lambda i, j: (i, 0)),
                  pl.BlockSpec((tm, k), lambda i, j: (i, 0)),
                  _layer_cols(w, layer, tn, 0)(k),
                  _layer_cols(w, layer, tn, 1)(k)] + gate_specs(zg_col) + gate_specs(zs_col),
        out_specs=pl.BlockSpec((tm, tn), lambda i, j: (i, j)),
        out_shape=jax.ShapeDtypeStruct((m, n), BF16),
        compiler_params=_params("parallel", "parallel"),
        name="branch_merge",
    )(o_gla, o_s5, w, w, proj, proj, proj, proj)


def _ln_kernel(x_ref, y_ref, g_ref, b_ref, o_ref, obf_ref, *maybe_obf_t_ref, alpha, y_transposed):
    y = y_ref[...].T if y_transposed else y_ref[...]
    h = alpha * x_ref[...] + y
    mu = jnp.mean(h, axis=-1, keepdims=True)
    d = h - mu
    var = jnp.mean(d * d, axis=-1, keepdims=True)
    o = d * lax.rsqrt(var + LN_EPS) * g_ref[...] + b_ref[...]
    o_ref[...] = o
    obf_ref[...] = o.astype(BF16)
    for obf_t_ref in maybe_obf_t_ref:
        obf_t_ref[...] = o.T.astype(BF16)


def _deepnorm_ln(x, y, g, b, alpha, tm, y_transposed=False, also_transposed=False):
    m, d = x.shape
    row = pl.BlockSpec((tm, d), lambda i: (i, 0))
    col = pl.BlockSpec((d, tm), lambda i: (0, i))
    vec = pl.BlockSpec((1, d), lambda i: (0, 0))
    out_specs = [row, row]
    out_shape = [jax.ShapeDtypeStruct((m, d), F32), jax.ShapeDtypeStruct((m, d), BF16)]
    if also_transposed:
        out_specs.append(col)
        out_shape.append(jax.ShapeDtypeStruct((d, m), BF16))
    return pl.pallas_call(
        functools.partial(_ln_kernel, alpha=alpha, y_transposed=y_transposed),
        grid=(m // tm,),
        in_specs=[row, col if y_transposed else row, vec, vec],
        out_specs=out_specs,
        out_shape=out_shape,
        compiler_params=_params("parallel"),
        name="deepnorm_ln",
    )(x, y, g.reshape(1, d), b.reshape(1, d))


def _gla_kernel(*refs, c, nseq, heads, scale, has_init, n_prev):
    refs = list(refs)
    q_ref, k_ref, v_ref, r_ref, alr_ref, wa2_ref, ba_ref, g_ref = refs[:8]
    rest = refs[8:]
    s0_ref = rest.pop(0) if has_init else None
    o_ref, s_ref = rest[n_prev:]
    dk = q_ref.shape[-1] // heads
    dv = v_ref.shape[-1] // heads
    kpad = max(c, LANES)
    qpad = max(c, 16)

    @pl.when(pl.program_id(1) == 0)
    def _():
        if has_init:
            s_ref[...] = s0_ref[...]
        else:
            s_ref[...] = jnp.zeros_like(s_ref)

    row = lax.broadcasted_iota(jnp.int32, (c, c), 0)
    col = lax.broadcasted_iota(jnp.int32, (c, c), 1)
    tri_cc = (row >= col).astype(F32)
    rowp = lax.broadcasted_iota(jnp.int32, (qpad, kpad), 0)
    colp = lax.broadcasted_iota(jnp.int32, (qpad, kpad), 1)
    causal = rowp >= colp
    eye = lax.broadcasted_iota(jnp.int32, (dk, dk), 0) == lax.broadcasted_iota(jnp.int32, (dk, dk), 1)
    eye_bf = eye.astype(BF16)

    def pad_rows(x, rows):
        if x.shape[0] == rows:
            return x
        return jnp.concatenate([x, jnp.zeros((rows - x.shape[0], x.shape[1]), x.dtype)], axis=0)

    outs = []
    for s in range(nseq):
        rows = slice(s * c, (s + 1) * c)
        z = jnp.dot(alr_ref[rows, :], wa2_ref[...], precision=HIGHEST, preferred_element_type=F32) + ba_ref[...]
        log_a = (jnp.minimum(z, 0.0) - jnp.log1p(jnp.exp(-jnp.abs(z)))) * (1.0 / GLA_TAU)
        if c >= LANES // 2:
            b = jnp.dot(tri_cc, log_a, precision=HIGHEST, preferred_element_type=F32)
        else:
            ridx = lax.broadcasted_iota(jnp.int32, log_a.shape, 0)
            b = jnp.zeros_like(log_a)
            for t in range(c):
                b = b + jnp.where(ridx >= t, log_a[t:t + 1, :], 0.0)
        b_last = b[c - 1:c, :]
        k = k_ref[rows, :]
        q_dec_all = (q_ref[rows, :] * scale * jnp.exp(b)).astype(BF16)
        k_inv_all = (k * jnp.exp(-b)).astype(BF16)
        k_end_all = (k * jnp.exp(b_last - b)).astype(BF16)
        v_all = v_ref[rows, :].astype(BF16)
        r = r_ref[rows, :]
        gate = r * jax.nn.sigmoid(r)
        o_heads = []
        for h in range(heads):
            kc = slice(h * dk, (h + 1) * dk)
            vc = slice(h * dv, (h + 1) * dv)
            q_dec = pad_rows(q_dec_all[:, kc], qpad)
            k_inv = pad_rows(k_inv_all[:, kc], kpad)
            k_end = pad_rows(k_end_all[:, kc], kpad)
            v_bf = pad_rows(v_all[:, vc], kpad)
            scores = jnp.where(causal, _dot_nt(q_dec, k_inv), 0.0).astype(BF16)
            state = s_ref[s, h]
            o = (_dot(q_dec, state.astype(BF16)) + _dot(scores, v_bf))[:c]
            k_end_t = _dot_nt(eye_bf, k_end).astype(BF16)
            decay_col = jnp.exp(jnp.sum(jnp.where(eye, b_last[:, kc], 0.0), axis=1, keepdims=True))
            s_ref[s, h] = decay_col * state + _dot(k_end_t, v_bf)
            o = o * lax.rsqrt(jnp.mean(o * o, axis=-1, keepdims=True) + RMS_EPS) * g_ref[...]
            o_heads.append(o * gate[:, vc])
        outs.append(jnp.concatenate(o_heads, axis=1))
    o_all = outs[0] if nseq == 1 else jnp.concatenate(outs, axis=0)
    o_ref[...] = o_all.astype(o_ref.dtype)


def _gla(proj, wa2, ba, norm_g, s0, layer, s_prev, o_prev, *, depth, row0, n_rows, nb, seq_len, c, nseq, cols):
    heads = GLA_HEADS
    dk_tot = wa2.shape[1]
    dk = dk_tot // heads
    dv = norm_g.shape[0]
    dv_tot = heads * dv
    nchunk = seq_len // c
    assert nseq == 1 or nchunk == 1
    rb = nseq * c
    blk0 = row0 // rb
    q_col, k_col, v_col, r_col, lr_col = cols

    def rows_map(col_blk):
        return lambda b, j: (blk0 + b * nchunk + j, col_blk)

    in_specs = [
        pl.BlockSpec((rb, dk_tot), rows_map(q_col // dk_tot)),
        pl.BlockSpec((rb, dk_tot), rows_map(k_col // dk_tot)),
        pl.BlockSpec((rb, dv_tot), rows_map(v_col // dv_tot)),
        pl.BlockSpec((rb, dv_tot), rows_map(r_col // dv_tot)),
        pl.BlockSpec((rb, LANES), rows_map(lr_col // LANES)),
        pl.BlockSpec((LANES, dk_tot), lambda b, j: (0, 0)),
        pl.BlockSpec((1, dk_tot), lambda b, j: (0, 0)),
        pl.BlockSpec((1, dv), lambda b, j: (0, 0)),
    ]
    args = [proj, proj, proj, proj, proj, wa2, ba.reshape(1, -1), norm_g.reshape(1, dv)]
    state_spec = pl.BlockSpec((None, nseq, heads, dk, dv), lambda b, j: (layer, b, 0, 0, 0))
    if s0 is not None:
        in_specs.append(state_spec)
        args.append(s0)
    aliases = {}
    for prev, out_idx in ((s_prev, 1), (o_prev, 0)):
        if prev is not None:
            aliases[len(args)] = out_idx
            in_specs.append(pl.BlockSpec(memory_space=pl.ANY))
            args.append(prev)
    kern = functools.partial(_gla_kernel, c=c, nseq=nseq, heads=heads, scale=dk ** -0.5,
                             has_init=s0 is not None, n_prev=len(aliases))
    return pl.pallas_call(
        kern,
        grid=(nb // nseq, nchunk),
        in_specs=in_specs,
        out_specs=[pl.BlockSpec((rb, dv_tot), rows_map(0)), state_spec],
        out_shape=[jax.ShapeDtypeStruct((n_rows, dv_tot), BF16),
                   jax.ShapeDtypeStruct((depth, nb, heads, dk, dv), F32)],
        input_output_aliases=aliases,
        compiler_params=_params("parallel", "arbitrary"),
        name="gla",
    )(*args)


def _s5_tables(a_re, a_im, log_dt, b_re, b_im, c_re, c_im, c):
    dt = jnp.exp(log_dt)[:, None]
    mag = jnp.exp(a_re * dt)
    ar, ai = mag * jnp.cos(a_im * dt), mag * jnp.sin(a_im * dt)
    den = a_re * a_re + a_im * a_im
    cr = ((ar - 1.0) * a_re + ai * a_im) / den
    ci = (ai * a_re - (ar - 1.0) * a_im) / den
    bbr = cr[..., None] * b_re - ci[..., None] * b_im
    bbi = cr[..., None] * b_im + ci[..., None] * b_re
    tau = jnp.arange(c + 1, dtype=F32)[:, None, None]
    pmag = jnp.exp(tau * (a_re * dt))
    pw_r, pw_i = pmag * jnp.cos(tau * (a_im * dt)), pmag * jnp.sin(tau * (a_im * dt))
    ca_r = c_re[None] * pw_r[:, :, None, :] - c_im[None] * pw_i[:, :, None, :]
    ca_i = c_re[None] * pw_i[:, :, None, :] + c_im[None] * pw_r[:, :, None, :]
    kern = (jnp.einsum('tgmp,gpn->tgmn', ca_r[:c], bbr, precision=HIGHEST)
            - jnp.einsum('tgmp,gpn->tgmn', ca_i[:c], bbi, precision=HIGHEST))
    g, m, n = kern.shape[1:]
    p = a_re.shape[1]
    gpb = LANES // n
    nb = g // gpb

    def block_diag(x):
        lead, _, r, w = x.shape
        copies = jnp.tile(jnp.eye(w, dtype=F32), (1, gpb))
        wide = jnp.dot(x.reshape(-1, w), copies, precision=HIGHEST).reshape(lead, nb, gpb * r, gpb * w)
        return wide * jnp.kron(jnp.eye(gpb, dtype=F32), jnp.ones((r, w), F32))

    lag_blocks = block_diag(kern.transpose(0, 1, 3, 2))
    zero_block = jnp.zeros_like(lag_blocks[0])
    toep = jnp.concatenate(
        [jnp.concatenate([lag_blocks[t - s] if t >= s else zero_block for s in range(c)], axis=1)
         for t in range(c)], axis=2)
    back = (c - 1.0) - tau[:c]
    rev_mag = jnp.exp(back * (a_re * dt))
    rev_r, rev_i = rev_mag * jnp.cos(back * (a_im * dt)), rev_mag * jnp.sin(back * (a_im * dt))
    inj_r = rev_r[..., None] * bbr[None] - rev_i[..., None] * bbi[None]
    inj_i = rev_r[..., None] * bbi[None] + rev_i[..., None] * bbr[None]
    inj = jnp.concatenate([block_diag(x.transpose(0, 1, 3, 2)) for x in (inj_r, inj_i)], axis=3)
    inj = inj.transpose(1, 0, 2, 3).reshape(nb, c * LANES, 2 * gpb * p)
    read = jnp.concatenate(
        [jnp.concatenate(list(block_diag(x.transpose(0, 1, 3, 2))), axis=2)
         for x in (ca_r[1:], -ca_i[1:])], axis=1)
    step_r, step_i = pw_r[c].reshape(nb, 1, gpb * p), pw_i[c].reshape(nb, 1, gpb * p)
    step_same = jnp.concatenate([step_r, step_r], axis=-1)
    step_swap = jnp.concatenate([-step_i, step_i], axis=-1)
    return toep.astype(BF16), inj.astype(BF16), read.astype(BF16), step_same, step_swap


def _chunk_tokens(u_ref, u_next_ref, c, shift):
    r = u_ref.shape[0] // c
    return [_lane_shifted(u_ref[pl.ds(s, r, stride=c), :], u_next_ref[pl.ds(s, r, stride=c), :], shift)
            for s in range(c)]


def _chunk_rows(tokens):
    return jnp.concatenate([x.astype(BF16) for x in tokens], axis=1)


def _u_specs(t_tokens, u_col):
    blk = u_col // LANES
    return [pl.BlockSpec((t_tokens, LANES), lambda b: (0, blk + b)),
            pl.BlockSpec((t_tokens, LANES), lambda b: (0, blk + b + 1))]


def _s5_inject_kernel(u_ref, u_next_ref, w_ref, e_ref, *, c, shift):
    e_ref[...] = _dot(_chunk_rows(_chunk_tokens(u_ref, u_next_ref, c, shift)), w_ref[...])


def _s5_inject(proj, u_col, inj, c):
    t_tokens = proj.shape[0]
    nb, k, w = inj.shape
    return pl.pallas_call(
        functools.partial(_s5_inject_kernel, c=c, shift=u_col % LANES),
        grid=(nb,),
        in_specs=_u_specs(t_tokens, u_col) + [pl.BlockSpec((None, k, w), lambda b: (b, 0, 0))],
        out_specs=pl.BlockSpec((None, t_tokens // c, w), lambda b: (b, 0, 0)),
        out_shape=jax.ShapeDtypeStruct((nb, t_tokens // c, w), F32),
        compiler_params=_params("parallel"),
        name="s5_inject",
    )(proj, proj, inj)


def _s5_scan_kernel(e_ref, x0_ref, same_ref, swap_ref, xprev_ref, xfin_p_ref, xfin_s_ref, *, n_seq, n_chunk):
    half = e_ref.shape[-1] // 2
    same = same_ref[...]
    swap = swap_ref[...]

    def advance(x, e):
        x_swapped = jnp.concatenate([x[:, half:], x[:, :half]], axis=1)
        return same * x + swap * x_swapped + e

    def body(j, xs):
        new = []
        for q, x in enumerate(xs):
            row = q * n_chunk + j
            xprev_ref[pl.ds(row, 1), :] = x
            new.append(advance(x, e_ref[pl.ds(row, 1), :]))
        return tuple(new)

    zero = jnp.zeros((1, e_ref.shape[-1]), F32)
    xs = lax.fori_loop(0, n_chunk, body, (zero,) * n_seq)
    xfin_p_ref[...] = jnp.zeros_like(xfin_p_ref)
    for q, x in enumerate(xs):
        xfin_p_ref[q:q + 1, :] = x
    rp = n_seq * n_chunk
    x0 = x0_ref[...]
    xprev_ref[rp:, :] = x0
    xfin_s_ref[...] = advance(x0, e_ref[rp:, :])


def _s5_scan(e, x0_s, same, swap, n_seq, n_chunk):
    nb, r, w = e.shape
    ns = x0_s.shape[1]
    assert r == n_seq * n_chunk + ns
    pad_seq = -(-n_seq // 8) * 8
    vec = pl.BlockSpec((None, 1, w), lambda b: (b, 0, 0))
    return pl.pallas_call(
        functools.partial(_s5_scan_kernel, n_seq=n_seq, n_chunk=n_chunk),
        grid=(nb,),
        in_specs=[pl.BlockSpec((None, r, w), lambda b: (b, 0, 0)),
                  pl.BlockSpec((None, ns, w), lambda b: (b, 0, 0)), vec, vec],
        out_specs=[pl.BlockSpec((None, r, w), lambda b: (b, 0, 0)),
                   pl.BlockSpec((None, pad_seq, w), lambda b: (b, 0, 0)),
                   pl.BlockSpec((None, ns, w), lambda b: (b, 0, 0))],
        out_shape=[jax.ShapeDtypeStruct((nb, r, w), F32), jax.ShapeDtypeStruct((nb, pad_seq, w), F32),
                   jax.ShapeDtypeStruct((nb, ns, w), F32)],
        compiler_params=_params("parallel"),
        name="s5_scan",
    )(e, x0_s, same, swap)


def _s5_out_kernel(u_ref, u_next_ref, xp_ref, toep_ref, read_ref, d_ref, y_ref, *, c, shift):
    r = u_ref.shape[0] // c
    tokens = _chunk_tokens(u_ref, u_next_ref, c, shift)
    y = _dot(_chunk_rows(tokens), toep_ref[...]) + _dot(xp_ref[...].astype(BF16), read_ref[...])
    for t in range(c):
        y_ref[pl.ds(t, r, stride=c), :] = jax.nn.gelu(y[:, t * LANES:(t + 1) * LANES] + d_ref[...] * tokens[t])


def _s5_out(proj, u_col, xprev, toep, read, d_skip, c):
    t_tokens = proj.shape[0]
    nb, r, w = xprev.shape
    k = toep.shape[1]
    return pl.pallas_call(
        functools.partial(_s5_out_kernel, c=c, shift=u_col % LANES),
        grid=(nb,),
        in_specs=_u_specs(t_tokens, u_col) + [
            pl.BlockSpec((None, r, w), lambda b: (b, 0, 0)),
            pl.BlockSpec((None, k, k), lambda b: (b, 0, 0)),
            pl.BlockSpec((None, w, k), lambda b: (b, 0, 0)),
            pl.BlockSpec((1, LANES), lambda b: (0, b))],
        out_specs=pl.BlockSpec((t_tokens, LANES), lambda b: (0, b)),
        out_shape=jax.ShapeDtypeStruct((t_tokens, nb * LANES), F32),
        compiler_params=_params("parallel"),
        name="s5_out",
    )(proj, proj, xprev, toep, read, d_skip.reshape(1, -1))


def _s5_mixer(proj, u_col, tables, d_skip, st_re, st_im, n_prompt_seq, prompt_len):
    toep, inj, read, step_same, step_swap = tables
    c = S5_CHUNK
    nb = toep.shape[0]
    ns, groups, p = st_re.shape
    gpb = groups // nb

    def to_blocks(st):
        return st.reshape(ns, nb, gpb * p).transpose(1, 0, 2)

    def from_blocks(x):
        return x.transpose(1, 0, 2).reshape(x.shape[1], groups, p)

    e = _s5_inject(proj, u_col, inj, c)
    x0_s = jnp.concatenate([to_blocks(st_re), to_blocks(st_im)], axis=-1)
    xprev, xfin_p, xfin_s = _s5_scan(e, x0_s, step_same, step_swap, n_prompt_seq, prompt_len // c)
    y = _s5_out(proj, u_col, xprev, toep, read, d_skip, c)
    half = gpb * p
    xfin_p = xfin_p[:, :n_prompt_seq]
    return (y, (from_blocks(xfin_p[..., :half]), from_blocks(xfin_p[..., half:])),
            (from_blocks(xfin_s[..., :half]), from_blocks(xfin_s[..., half:])))


def _peer_route_kernel(q_ref, k1_ref, k2_ref, s1_ref, w1_ref, s2_ref, e2_ref, tau_ref):
    half = k1_ref.shape[1]
    neg = jnp.float32(-jnp.inf)

    def top_values(x):
        vals = []
        for _ in range(PEER_TOPK):
            m = jnp.max(x, axis=0, keepdims=True)
            vals.append(m)
            x = jnp.where(x == m, neg, x)
        return vals

    for h in range(PEER_HEADS):
        q1 = q_ref[:, (2 * h) * half:(2 * h + 1) * half]
        q2 = q_ref[:, (2 * h + 1) * half:(2 * h + 2) * half]
        s1 = lax.dot_general(k1_ref[...], q1, (((1,), (1,)), ((), ())), precision=HIGHEST,
                             preferred_element_type=F32)
        s2 = lax.dot_general(k2_ref[...], q2, (((1,), (1,)), ((), ())), precision=HIGHEST,
                             preferred_element_type=F32)
        a = top_values(s1)
        b = jnp.concatenate(top_values(s2), axis=0)
        b0 = b[0:1]
        cand = [a[i] + b[:PEER_TOPK // (i + 1)] for i in range(PEER_TOPK)]
        n_cand = sum(x.shape[0] for x in cand)
        cand.append(jnp.full((-n_cand % 8, s1.shape[1]), neg, F32))
        top = top_values(jnp.concatenate(cand, axis=0))
        zsum = top[0] * 0.0
        for m in top:
            zsum = zsum + jnp.exp(m - top[0])
        s1_ref[h] = s1
        s2_ref[h] = s2
        w1_ref[h] = jnp.exp(s1 - a[0]) / zsum
        e2_ref[h] = jnp.exp(s2 - b0)
        tau_ref[h] = top[-1]


def _peer_route(q, key1, key2, tm):
    t_tokens = q.shape[0]
    nk, half = key1.shape
    big = pl.BlockSpec((PEER_HEADS, nk, tm), lambda i: (0, 0, i))
    big_shape = jax.ShapeDtypeStruct((PEER_HEADS, nk, t_tokens), F32)
    return pl.pallas_call(
        _peer_route_kernel,
        grid=(t_tokens // tm,),
        in_specs=[pl.BlockSpec((tm, q.shape[1]), lambda i: (i, 0)),
                  pl.BlockSpec((nk, half), lambda i: (0, 0)),
                  pl.BlockSpec((nk, half), lambda i: (0, 0))],
        out_specs=[big, big, big, big, pl.BlockSpec((PEER_HEADS, 1, tm), lambda i: (0, 0, i))],
        out_shape=[big_shape, big_shape, big_shape, big_shape,
                   jax.ShapeDtypeStruct((PEER_HEADS, 1, t_tokens), F32)],
        compiler_params=_params("parallel"),
        name="peer_route",
    )(q, key1, key2)


def _peer_ffn_kernel(xt_ref, u_ref, vt_ref, s1_ref, w1_ref, s2_ref, e2_ref, tau_ref, o_ref, p_ref, h_ref, g_ref,
                     *, ni, n_et):
    step = pl.program_id(1)
    et = jnp.minimum(step, n_et - 1)
    slot = step % 2

    @pl.when(step == 0)
    def _():
        o_ref[...] = jnp.zeros_like(o_ref)
        p_ref[...] = jnp.zeros_like(p_ref)

    nk = s2_ref.shape[1]
    te, tm = h_ref.shape
    d = o_ref.shape[0]
    chunks = [(ii, tc) for ii in range(ni) for tc in range(tm // LANES)]

    def block(ii, tc):
        return slice(ii * nk, (ii + 1) * nk), slice(tc * LANES, (tc + 1) * LANES)

    def gate_chunk(ii, tc):
        rows, lanes = block(ii, tc)
        i = et * ni + ii
        g = None
        for h in range(PEER_HEADS):
            s1_row = s1_ref[h, pl.ds(i, 1), :][:, lanes]
            w1_row = w1_ref[h, pl.ds(i, 1), :][:, lanes]
            hit = (s1_row + s2_ref[h, :, lanes]) >= tau_ref[h, :, lanes]
            term = jnp.where(hit, e2_ref[h, :, lanes] * w1_row, 0.0)
            g = term if g is None else g + term
        g_ref[rows, lanes] = g

    def finish_chunk(ii, tc):
        rows, lanes = block(ii, tc)
        p_ref[slot, rows, lanes] = (g_ref[rows, lanes] * jax.nn.gelu(h_ref[rows, lanes])).astype(BF16)

    quarter = len(chunks) // 4
    half = tm // 2
    for c in chunks[:quarter]:
        gate_chunk(*c)
    h_ref[:, :half] = _dot(u_ref[...], xt_ref[:, :half])
    for c in chunks[quarter:2 * quarter]:
        gate_chunk(*c)
    h_ref[:, half:] = _dot(u_ref[...], xt_ref[:, half:])
    p_prev = p_ref[1 - slot]
    vector_work = [[(gate_chunk, c) for c in chunks[2 * quarter:3 * quarter]],
                   [(gate_chunk, c) for c in chunks[3 * quarter:]],
                   [(finish_chunk, c) for c in chunks[:2 * quarter]],
                   [(finish_chunk, c) for c in chunks[2 * quarter:]]]
    dq = d // len(vector_work)
    for k, work in enumerate(vector_work):
        for fn, c in work:
            fn(*c)
        rows = slice(k * dq, (k + 1) * dq)
        o_ref[rows, :] += _dot(vt_ref[rows, :], p_prev)


def _peer_ffn(xt_bf, u_tab, vt_tab, layer, route, tm, te):
    s1, w1, s2, e2, tau = route
    d, t_tokens = xt_bf.shape
    n_exp = u_tab.shape[1]
    nk = s2.shape[1]
    ni = te // nk
    n_et = n_exp // te
    once = pl.Buffered(1)
    tok = pl.BlockSpec((PEER_HEADS, nk, tm), lambda i, j: (0, 0, i), pipeline_mode=once)
    return pl.pallas_call(
        functools.partial(_peer_ffn_kernel, ni=ni, n_et=n_et),
        grid=(t_tokens // tm, n_et + 1),
        in_specs=[pl.BlockSpec((d, tm), lambda i, j: (0, i), pipeline_mode=once),
                  pl.BlockSpec((None, te, d), lambda i, j: (layer, jnp.minimum(j, n_et - 1), 0)),
                  pl.BlockSpec((None, d, te), lambda i, j: (layer, 0, jnp.maximum(j - 1, 0))),
                  tok, tok, tok, tok,
                  pl.BlockSpec((PEER_HEADS, 1, tm), lambda i, j: (0, 0, i), pipeline_mode=once)],
        out_specs=pl.BlockSpec((d, tm), lambda i, j: (0, i)),
        out_shape=jax.ShapeDtypeStruct((d, t_tokens), F32),
        scratch_shapes=[pltpu.VMEM((2, te, tm), BF16), pltpu.VMEM((te, tm), F32), pltpu.VMEM((te, tm), F32)],
        compiler_params=_params("parallel", "arbitrary"),
        name="peer_ffn",
    )(xt_bf, u_tab, vt_tab, s1, w1, s2, e2, tau)


def kernel(x_prompt, x_sample, state_gla, state_s5_re, state_s5_im, ln1_g, ln1_b, w_in, gla_w_a2, gla_b_a, gla_norm_g, s5_a_re, s5_a_im, s5_log_dt, s5_b_re, s5_b_im, s5_c_re, s5_c_im, s5_d, s5_w_glu, s5_b_glu, w_branch, w_out, ln2_g, ln2_b, peer_w_q, peer_key1, peer_key2, peer_u, peer_v):
    depth = w_in.shape[0]
    pb, plen, d = x_prompt.shape
    sb, slen, _ = x_sample.shape
    tp, ts = pb * plen, sb * slen
    t_all = tp + ts
    dk_tot = gla_w_a2.shape[2]
    dv_tot = GLA_HEADS * gla_norm_g.shape[1]
    s5_w = s5_d.shape[1]
    alpha = (2 * depth) ** 0.25
    widths = (dk_tot, dk_tot, dv_tot, dv_tot, GLA_RANK, s5_w, d, d)
    off = [0]
    for wdt in widths:
        off.append(off[-1] + wdt)
    lr0, lr1 = off[4], off[5]
    x = jnp.concatenate([x_prompt.reshape(tp, d), x_sample.reshape(ts, d)], axis=0)
    x_bf = x.astype(BF16)
    proj_tile = 4 * LANES
    proj_cols = -(-off[-1] // proj_tile) * proj_tile
    w_in_t = w_in.transpose(0, 2, 1)
    wa2 = jnp.pad(gla_w_a2, ((0, 0), (0, LANES - GLA_RANK), (0, 0)))
    u_bf = peer_u.astype(BF16)
    vt_bf = peer_v.astype(BF16).transpose(0, 2, 1)
    cols = (off[0], off[1], off[2], off[3], lr0)
    u_col, zg_col, zs_col = off[5], off[6], off[7]
    gla_p = gla_s = None
    re_p, im_p, re_s, im_s = [], [], [], []
    for l in range(depth):
        proj = _mm(x_bf, w_in_t, l, 1024, proj_tile, n=proj_cols, w_transposed=True)

        o_gla, gla_p = _gla(proj, wa2[l], gla_b_a[l], gla_norm_g[l], None, l, gla_p, None, depth=depth, row0=0,
                            n_rows=t_all, nb=pb, seq_len=plen, c=min(GLA_CHUNK, plen), nseq=1, cols=cols)
        o_gla, gla_s = _gla(proj, wa2[l], gla_b_a[l], gla_norm_g[l], state_gla, l, gla_s, o_gla, depth=depth,
                            row0=tp, n_rows=t_all, nb=sb, seq_len=slen, c=min(GLA_CHUNK, slen), nseq=2, cols=cols)

        tables = _s5_tables(s5_a_re[l], s5_a_im[l], s5_log_dt[l], s5_b_re[l], s5_b_im[l], s5_c_re[l], s5_c_im[l],
                            S5_CHUNK)
        y_s5, (r_p, i_p), (r_s, i_s) = _s5_mixer(proj, u_col, tables, s5_d[l], state_s5_re[l], state_s5_im[l],
                                                 pb, plen)
        o_s5 = _glu(y_s5, s5_w_glu, l, s5_b_glu[l], 1024, 1024)

        merged = _branch_merge(o_gla, o_s5, w_branch, l, proj, zg_col, zs_col, 1024, 512)
        mix = _mm(merged, w_out, l, 1024, proj_tile)
        x, x_bf, xt_bf = _deepnorm_ln(x, mix, ln1_g[l], ln1_b[l], alpha, 256, also_transposed=True)

        q = _mm(x_bf, peer_w_q, l, 1024, proj_tile)
        route = _peer_route(q, peer_key1[l], peer_key2[l], 256)
        ffn_t = _peer_ffn(xt_bf, u_bf, vt_bf, l, route, 512, 512)
        x, x_bf = _deepnorm_ln(x, ffn_t, ln2_g[l], ln2_b[l], alpha, 256, y_transposed=True)

        re_p.append(r_p)
        im_p.append(i_p)
        re_s.append(r_s)
        im_s.append(i_s)

    y_prompt = x[:tp].reshape(pb, plen, d)
    y_sample = x[tp:].reshape(sb, slen, d)
    return (y_prompt, y_sample, gla_p, jnp.stack(re_p), jnp.stack(im_p),
            gla_s, jnp.stack(re_s), jnp.stack(im_s))
```

```python
import functools

import jax
import jax.numpy as jnp
from jax import lax
from jax.experimental import pallas as pl
from jax.experimental.pallas import tpu as pltpu

F32 = jnp.float32
BF16 = jnp.bfloat16
HIGHEST = lax.Precision.HIGHEST

V7X_VMEM_BYTES = 64 * 1024 * 1024
VMEM_LIMIT_BYTES = V7X_VMEM_BYTES - 8 * 1024 * 1024
LANES = 128

GLA_HEADS = 4
GLA_RANK = 16
GLA_TAU = 16.0
GLA_CHUNK = 64
RMS_EPS = 1e-6
S5_CHUNK = 8
PEER_HEADS = 8
PEER_TOPK = 16
LN_EPS = 1e-5


def _params(*semantics, flags=None):
    return pltpu.CompilerParams(dimension_semantics=semantics, vmem_limit_bytes=VMEM_LIMIT_BYTES, flags=flags)


def _dot(a, b):
    return jnp.dot(a, b, preferred_element_type=F32)


def _dot_nt(a, b):
    return lax.dot_general(a, b, (((1,), (1,)), ((), ())), preferred_element_type=F32)


def _mm_kernel(a_ref, b_ref, o_ref):
    o_ref[...] = _dot(a_ref[...], b_ref[...].astype(BF16)).astype(o_ref.dtype)


def _layer_cols(w, layer, tn, row0=0):
    def spec(rows):
        return pl.BlockSpec((None, rows, tn), lambda i, j: (layer, row0, j))
    return spec


def _mm_nt_kernel(a_ref, bt_ref, o_ref):
    o_ref[...] = _dot_nt(a_ref[...], bt_ref[...].astype(BF16)).astype(o_ref.dtype)


def _mm(a, w, layer, tm, tn, n=None, out_dtype=F32, w_transposed=False):
    m, k = a.shape
    n = n or w.shape[2]
    w_spec = (pl.BlockSpec((None, tn, k), lambda i, j: (layer, j, 0)) if w_transposed
              else _layer_cols(w, layer, tn)(k))
    return pl.pallas_call(
        _mm_nt_kernel if w_transposed else _mm_kernel,
        grid=(m // tm, n // tn),
        in_specs=[pl.BlockSpec((tm, k), lambda i, j: (i, 0)), w_spec],
        out_specs=pl.BlockSpec((tm, tn), lambda i, j: (i, j)),
        out_shape=jax.ShapeDtypeStruct((m, n), out_dtype),
        compiler_params=_params("parallel", "parallel"),
        name="mm",
    )(a, w)


def _glu_kernel(y_ref, w_ref, b_ref, ytile_ref, o_ref):
    acc = _dot(y_ref[...].astype(BF16), w_ref[...].astype(BF16)) + b_ref[...]
    o_ref[...] = (ytile_ref[...] * jax.nn.sigmoid(acc)).astype(o_ref.dtype)


def _glu(y, w, layer, b, tm, tn):
    m, k = y.shape
    n = w.shape[2]
    return pl.pallas_call(
        _glu_kernel,
        grid=(m // tm, n // tn),
        in_specs=[pl.BlockSpec((tm, k), lambda i, j: (i, 0)),
                  _layer_cols(w, layer, tn)(k),
                  pl.BlockSpec((1, tn), lambda i, j: (0, j)),
                  pl.BlockSpec((tm, tn), lambda i, j: (i, j))],
        out_specs=pl.BlockSpec((tm, tn), lambda i, j: (i, j)),
        out_shape=jax.ShapeDtypeStruct((m, n), BF16),
        compiler_params=_params("parallel", "parallel"),
        name="s5_glu",
    )(y, w, b.reshape(1, n), y)


def _lane_shifted(a, b, shift):
    if shift == 0:
        return a
    return jnp.concatenate([a[:, shift:], b[:, :shift]], axis=1)


def _branch_kernel(og_ref, os_ref, w1_ref, w2_ref, zg_ref, zg_next_ref, zs_ref, zs_next_ref, o_ref, *, shift):
    m1 = _dot(og_ref[...], w1_ref[...].astype(BF16))
    m2 = _dot(os_ref[...], w2_ref[...].astype(BF16))
    zg = _lane_shifted(zg_ref[...], zg_next_ref[...], shift)
    zs = _lane_shifted(zs_ref[...], zs_next_ref[...], shift)
    o_ref[...] = (jax.nn.sigmoid(zg) * m1 + jax.nn.sigmoid(zs) * m2).astype(o_ref.dtype)


def _branch_merge(o_gla, o_s5, w, layer, proj, zg_col, zs_col, tm, tn):
    m, k = o_gla.shape
    assert o_s5.shape[1] == k and w.shape[1] == 2 * k
    n = w.shape[2]
    shift = zg_col % LANES
    assert zs_col % LANES == shift and (zg_col - shift) % tn == 0 and (zs_col - shift) % tn == 0

    def gate_specs(col):
        blk, nxt, per = (col - shift) // tn, (col - shift) // LANES, tn // LANES
        return [pl.BlockSpec((tm, tn), lambda i, j: (i, blk + j)),
                pl.BlockSpec((tm, LANES), lambda i, j: (i, nxt + per * (j + 1)))]

    return pl.pallas_call(
        functools.partial(_branch_kernel, shift=shift),
        grid=(m // tm, n // tn),
        in_specs=[pl.BlockSpec((tm, k), lambda i, j: (i, 0)),
                  pl.BlockSpec((tm, k), lambda i, j: (i, 0)),
                  _layer_cols(w, layer, tn, 0)(k),
                  _layer_cols(w, layer, tn, 1)(k)] + gate_specs(zg_col) + gate_specs(zs_col),
        out_specs=pl.BlockSpec((tm, tn), lambda i, j: (i, j)),
        out_shape=jax.ShapeDtypeStruct((m, n), BF16),
        compiler_params=_params("parallel", "parallel"),
        name="branch_merge",
    )(o_gla, o_s5, w, w, proj, proj, proj, proj)


def _ln_kernel(*refs, alpha, y_transposed, n_head, two_in, two_out, emit_bf16, emit_transposed):
    refs = list(refs)
    step = pl.program_id(0)
    if two_in:
        x = jnp.where(step < n_head, refs[0][...], refs[1][...])
        refs = refs[2:]
    else:
        x = refs[0][...]
        refs = refs[1:]
    y_ref, g_ref, b_ref = refs[:3]
    outs = refs[3:]
    y = y_ref[...].T if y_transposed else y_ref[...]
    h = alpha * x + y
    mu = jnp.mean(h, axis=-1, keepdims=True)
    d = h - mu
    var = jnp.mean(d * d, axis=-1, keepdims=True)
    o = d * lax.rsqrt(var + LN_EPS) * g_ref[...] + b_ref[...]
    if two_out:
        head_ref, tail_ref = outs[:2]
        outs = outs[2:]

        @pl.when(step < n_head)
        def _():
            head_ref[...] = o

        @pl.when(step >= n_head)
        def _():
            tail_ref[...] = o
    else:
        outs[0][...] = o
        outs = outs[1:]
    if emit_bf16:
        outs[0][...] = o.astype(BF16)
        outs = outs[1:]
    if emit_transposed:
        outs[0][...] = o.T.astype(BF16)


def _deepnorm_ln(x, y, g, b, alpha, tm, y_transposed=False, emit_bf16=True, emit_transposed=False, split_out=None):
    xs = x if isinstance(x, (tuple, list)) else (x,)
    d = xs[0].shape[1]
    m = sum(a.shape[0] for a in xs)
    two_in, two_out = len(xs) == 2, split_out is not None
    n_head = (xs[0].shape[0] if two_in else split_out if two_out else m) // tm
    assert not (two_in and two_out) or xs[0].shape[0] == split_out
    row = pl.BlockSpec((tm, d), lambda i: (i, 0))
    head = pl.BlockSpec((tm, d), lambda i: (jnp.minimum(i, n_head - 1), 0))
    tail = pl.BlockSpec((tm, d), lambda i: (jnp.maximum(i - n_head, 0), 0))
    col = pl.BlockSpec((d, tm), lambda i: (0, i))
    vec = pl.BlockSpec((1, d), lambda i: (0, 0))
    out_specs = [head, tail] if two_out else [row]
    out_shape = ([jax.ShapeDtypeStruct((split_out, d), F32), jax.ShapeDtypeStruct((m - split_out, d), F32)]
                 if two_out else [jax.ShapeDtypeStruct((m, d), F32)])
    if emit_bf16:
        out_specs.append(row)
        out_shape.append(jax.ShapeDtypeStruct((m, d), BF16))
    if emit_transposed:
        out_specs.append(col)
        out_shape.append(jax.ShapeDtypeStruct((d, m), BF16))
    kern = functools.partial(_ln_kernel, alpha=alpha, y_transposed=y_transposed, n_head=n_head, two_in=two_in,
                             two_out=two_out, emit_bf16=emit_bf16, emit_transposed=emit_transposed)
    return pl.pallas_call(
        kern,
        grid=(m // tm,),
        in_specs=([head, tail] if two_in else [row]) + [col if y_transposed else row, vec, vec],
        out_specs=out_specs,
        out_shape=out_shape,
        compiler_params=_params("arbitrary"),
        name="deepnorm_ln",
    )(*xs, y, g.reshape(1, d), b.reshape(1, d))


def _gla_kernel(*refs, c, nseq, heads, scale, has_init, n_prev):
    refs = list(refs)
    q_ref, k_ref, v_ref, r_ref, alr_ref, wa2_ref, ba_ref, g_ref = refs[:8]
    rest = refs[8:]
    s0_ref = rest.pop(0) if has_init else None
    o_ref, s_ref = rest[n_prev:]
    dk = q_ref.shape[-1] // heads
    dv = v_ref.shape[-1] // heads
    kpad = max(c, LANES)
    qpad = max(c, 16)

    @pl.when(pl.program_id(1) == 0)
    def _():
        if has_init:
            s_ref[...] = s0_ref[...]
        else:
            s_ref[...] = jnp.zeros_like(s_ref)

    row = lax.broadcasted_iota(jnp.int32, (c, c), 0)
    col = lax.broadcasted_iota(jnp.int32, (c, c), 1)
    tri_cc = (row >= col).astype(F32)
    rowp = lax.broadcasted_iota(jnp.int32, (qpad, kpad), 0)
    colp = lax.broadcasted_iota(jnp.int32, (qpad, kpad), 1)
    causal = rowp >= colp
    eye = lax.broadcasted_iota(jnp.int32, (dk, dk), 0) == lax.broadcasted_iota(jnp.int32, (dk, dk), 1)
    eye_bf = eye.astype(BF16)

    def pad_rows(x, rows):
        if x.shape[0] == rows:
            return x
        return jnp.concatenate([x, jnp.zeros((rows - x.shape[0], x.shape[1]), x.dtype)], axis=0)

    outs = []
    for s in range(nseq):
        rows = slice(s * c, (s + 1) * c)
        z = jnp.dot(alr_ref[rows, :], wa2_ref[...], precision=HIGHEST, preferred_element_type=F32) + ba_ref[...]
        log_a = (jnp.minimum(z, 0.0) - jnp.log1p(jnp.exp(-jnp.abs(z)))) * (1.0 / GLA_TAU)
        if c >= LANES // 2:
            b = jnp.dot(tri_cc, log_a, precision=HIGHEST, preferred_element_type=F32)
        else:
            ridx = lax.broadcasted_iota(jnp.int32, log_a.shape, 0)
            b = jnp.zeros_like(log_a)
            for t in range(c):
                b = b + jnp.where(ridx >= t, log_a[t:t + 1, :], 0.0)
        b_last = b[c - 1:c, :]
        k = k_ref[rows, :]
        q_dec_all = (q_ref[rows, :] * scale * jnp.exp(b)).astype(BF16)
        k_inv_all = (k * jnp.exp(-b)).astype(BF16)
        k_end_all = (k * jnp.exp(b_last - b)).astype(BF16)
        v_all = v_ref[rows, :].astype(BF16)
        r = r_ref[rows, :]
        gate = r * jax.nn.sigmoid(r)
        o_heads = []
        for h in range(heads):
            kc = slice(h * dk, (h + 1) * dk)
            vc = slice(h * dv, (h + 1) * dv)
            q_dec = pad_rows(q_dec_all[:, kc], qpad)
            k_inv = pad_rows(k_inv_all[:, kc], kpad)
            k_end = pad_rows(k_end_all[:, kc], kpad)
            v_bf = pad_rows(v_all[:, vc], kpad)
            scores = jnp.where(causal, _dot_nt(q_dec, k_inv), 0.0).astype(BF16)
            state = s_ref[s, h]
            o = (_dot(q_dec, state.astype(BF16)) + _dot(scores, v_bf))[:c]
            k_end_t = _dot_nt(eye_bf, k_end).astype(BF16)
            decay_col = jnp.exp(jnp.sum(jnp.where(eye, b_last[:, kc], 0.0), axis=1, keepdims=True))
            s_ref[s, h] = decay_col * state + _dot(k_end_t, v_bf)
            o = o * lax.rsqrt(jnp.mean(o * o, axis=-1, keepdims=True) + RMS_EPS) * g_ref[...]
            o_heads.append(o * gate[:, vc])
        outs.append(jnp.concatenate(o_heads, axis=1))
    o_all = outs[0] if nseq == 1 else jnp.concatenate(outs, axis=0)
    o_ref[...] = o_all.astype(o_ref.dtype)


def _gla(proj, wa2, ba, norm_g, s0, layer, s_prev, o_prev, *, depth, row0, n_rows, nb, seq_len, c, nseq, cols):
    heads = GLA_HEADS
    dk_tot = wa2.shape[1]
    dk = dk_tot // heads
    dv = norm_g.shape[0]
    dv_tot = heads * dv
    nchunk = seq_len // c
    assert nseq == 1 or nchunk == 1
    rb = nseq * c
    blk0 = row0 // rb
    q_col, k_col, v_col, r_col, lr_col = cols

    def rows_map(col_blk):
        return lambda b, j: (blk0 + b * nchunk + j, col_blk)

    in_specs = [
        pl.BlockSpec((rb, dk_tot), rows_map(q_col // dk_tot)),
        pl.BlockSpec((rb, dk_tot), rows_map(k_col // dk_tot)),
        pl.BlockSpec((rb, dv_tot), rows_map(v_col // dv_tot)),
        pl.BlockSpec((rb, dv_tot), rows_map(r_col // dv_tot)),
        pl.BlockSpec((rb, LANES), rows_map(lr_col // LANES)),
        pl.BlockSpec((LANES, dk_tot), lambda b, j: (0, 0)),
        pl.BlockSpec((1, dk_tot), lambda b, j: (0, 0)),
        pl.BlockSpec((1, dv), lambda b, j: (0, 0)),
    ]
    args = [proj, proj, proj, proj, proj, wa2, ba.reshape(1, -1), norm_g.reshape(1, dv)]
    state_spec = pl.BlockSpec((None, nseq, heads, dk, dv), lambda b, j: (layer, b, 0, 0, 0))
    if s0 is not None:
        in_specs.append(state_spec)
        args.append(s0)
    aliases = {}
    for prev, out_idx in ((s_prev, 1), (o_prev, 0)):
        if prev is not None:
            aliases[len(args)] = out_idx
            in_specs.append(pl.BlockSpec(memory_space=pl.ANY))
            args.append(prev)
    kern = functools.partial(_gla_kernel, c=c, nseq=nseq, heads=heads, scale=dk ** -0.5,
                             has_init=s0 is not None, n_prev=len(aliases))
    return pl.pallas_call(
        kern,
        grid=(nb // nseq, nchunk),
        in_specs=in_specs,
        out_specs=[pl.BlockSpec((rb, dv_tot), rows_map(0)), state_spec],
        out_shape=[jax.ShapeDtypeStruct((n_rows, dv_tot), BF16),
                   jax.ShapeDtypeStruct((depth, nb, heads, dk, dv), F32)],
        input_output_aliases=aliases,
        compiler_params=_params("parallel", "arbitrary"),
        name="gla",
    )(*args)


def _s5_tables(a_re, a_im, log_dt, b_re, b_im, c_re, c_im, c):
    dt = jnp.exp(log_dt)[:, None]
    mag = jnp.exp(a_re * dt)
    ar, ai = mag * jnp.cos(a_im * dt), mag * jnp.sin(a_im * dt)
    den = a_re * a_re + a_im * a_im
    cr = ((ar - 1.0) * a_re + ai * a_im) / den
    ci = (ai * a_re - (ar - 1.0) * a_im) / den
    bbr = cr[..., None] * b_re - ci[..., None] * b_im
    bbi = cr[..., None] * b_im + ci[..., None] * b_re
    tau = jnp.arange(c + 1, dtype=F32)[:, None, None]
    pmag = jnp.exp(tau * (a_re * dt))
    pw_r, pw_i = pmag * jnp.cos(tau * (a_im * dt)), pmag * jnp.sin(tau * (a_im * dt))
    ca_r = c_re[None] * pw_r[:, :, None, :] - c_im[None] * pw_i[:, :, None, :]
    ca_i = c_re[None] * pw_i[:, :, None, :] + c_im[None] * pw_r[:, :, None, :]
    kern = (jnp.einsum('tgmp,gpn->tgmn', ca_r[:c], bbr, precision=HIGHEST)
            - jnp.einsum('tgmp,gpn->tgmn', ca_i[:c], bbi, precision=HIGHEST))
    g, m, n = kern.shape[1:]
    p = a_re.shape[1]
    gpb = LANES // n
    nb = g // gpb

    def block_diag(x):
        lead, _, r, w = x.shape
        copies = jnp.tile(jnp.eye(w, dtype=F32), (1, gpb))
        wide = jnp.dot(x.reshape(-1, w), copies, precision=HIGHEST).reshape(lead, nb, gpb * r, gpb * w)
        return wide * jnp.kron(jnp.eye(gpb, dtype=F32), jnp.ones((r, w), F32))

    lag_blocks = block_diag(kern.transpose(0, 1, 3, 2))
    zero_block = jnp.zeros_like(lag_blocks[0])
    toep = jnp.concatenate(
        [jnp.concatenate([lag_blocks[t - s] if t >= s else zero_block for s in range(c)], axis=1)
         for t in range(c)], axis=2)
    back = (c - 1.0) - tau[:c]
    rev_mag = jnp.exp(back * (a_re * dt))
    rev_r, rev_i = rev_mag * jnp.cos(back * (a_im * dt)), rev_mag * jnp.sin(back * (a_im * dt))
    inj_r = rev_r[..., None] * bbr[None] - rev_i[..., None] * bbi[None]
    inj_i = rev_r[..., None] * bbi[None] + rev_i[..., None] * bbr[None]
    inj = jnp.concatenate([block_diag(x.transpose(0, 1, 3, 2)) for x in (inj_r, inj_i)], axis=3)
    inj = inj.transpose(1, 0, 2, 3).reshape(nb, c * LANES, 2 * gpb * p)
    read = jnp.concatenate(
        [jnp.concatenate(list(block_diag(x.transpose(0, 1, 3, 2))), axis=2)
         for x in (ca_r[1:], -ca_i[1:])], axis=1)
    step_r, step_i = pw_r[c].reshape(nb, 1, gpb * p), pw_i[c].reshape(nb, 1, gpb * p)
    step_same = jnp.concatenate([step_r, step_r], axis=-1)
    step_swap = jnp.concatenate([-step_i, step_i], axis=-1)
    return toep.astype(BF16), inj.astype(BF16), read.astype(BF16), step_same, step_swap


def _chunk_tokens(u_ref, u_next_ref, c, shift):
    r = u_ref.shape[0] // c
    return [_lane_shifted(u_ref[pl.ds(s, r, stride=c), :], u_next_ref[pl.ds(s, r, stride=c), :], shift)
            for s in range(c)]


def _chunk_rows(tokens):
    return jnp.concatenate([x.astype(BF16) for x in tokens], axis=1)


def _u_specs(t_tokens, u_col):
    blk = u_col // LANES
    return [pl.BlockSpec((t_tokens, LANES), lambda b: (0, blk + b)),
            pl.BlockSpec((t_tokens, LANES), lambda b: (0, blk + b + 1))]


def _s5_inject_kernel(u_ref, u_next_ref, w_ref, e_ref, *, c, shift):
    e_ref[...] = _dot(_chunk_rows(_chunk_tokens(u_ref, u_next_ref, c, shift)), w_ref[...])


def _s5_inject(proj, u_col, inj, c):
    t_tokens = proj.shape[0]
    nb, k, w = inj.shape
    return pl.pallas_call(
        functools.partial(_s5_inject_kernel, c=c, shift=u_col % LANES),
        grid=(nb,),
        in_specs=_u_specs(t_tokens, u_col) + [pl.BlockSpec((None, k, w), lambda b: (b, 0, 0))],
        out_specs=pl.BlockSpec((None, t_tokens // c, w), lambda b: (b, 0, 0)),
        out_shape=jax.ShapeDtypeStruct((nb, t_tokens // c, w), F32),
        compiler_params=_params("parallel"),
        name="s5_inject",
    )(proj, proj, inj)


def _s5_scan_kernel(e_ref, x0_ref, same_ref, swap_ref, xprev_ref, xfin_p_ref, xfin_s_ref, *, n_seq, n_chunk):
    half = e_ref.shape[-1] // 2
    same = same_ref[...]
    swap = swap_ref[...]

    def advance(x, e):
        x_swapped = jnp.concatenate([x[:, half:], x[:, :half]], axis=1)
        return same * x + swap * x_swapped + e

    def body(j, xs):
        new = []
        for q, x in enumerate(xs):
            row = q * n_chunk + j
            xprev_ref[pl.ds(row, 1), :] = x
            new.append(advance(x, e_ref[pl.ds(row, 1), :]))
        return tuple(new)

    zero = jnp.zeros((1, e_ref.shape[-1]), F32)
    xs = lax.fori_loop(0, n_chunk, body, (zero,) * n_seq)
    xfin_p_ref[...] = jnp.zeros_like(xfin_p_ref)
    for q, x in enumerate(xs):
        xfin_p_ref[q:q + 1, :] = x
    rp = n_seq * n_chunk
    x0 = x0_ref[...]
    xprev_ref[rp:, :] = x0
    xfin_s_ref[...] = advance(x0, e_ref[rp:, :])


def _s5_scan(e, x0_s, same, swap, n_seq, n_chunk):
    nb, r, w = e.shape
    ns = x0_s.shape[1]
    assert r == n_seq * n_chunk + ns
    pad_seq = -(-n_seq // 8) * 8
    vec = pl.BlockSpec((None, 1, w), lambda b: (b, 0, 0))
    return pl.pallas_call(
        functools.partial(_s5_scan_kernel, n_seq=n_seq, n_chunk=n_chunk),
        grid=(nb,),
        in_specs=[pl.BlockSpec((None, r, w), lambda b: (b, 0, 0)),
                  pl.BlockSpec((None, ns, w), lambda b: (b, 0, 0)), vec, vec],
        out_specs=[pl.BlockSpec((None, r, w), lambda b: (b, 0, 0)),
                   pl.BlockSpec((None, pad_seq, w), lambda b: (b, 0, 0)),
                   pl.BlockSpec((None, ns, w), lambda b: (b, 0, 0))],
        out_shape=[jax.ShapeDtypeStruct((nb, r, w), F32), jax.ShapeDtypeStruct((nb, pad_seq, w), F32),
                   jax.ShapeDtypeStruct((nb, ns, w), F32)],
        compiler_params=_params("parallel"),
        name="s5_scan",
    )(e, x0_s, same, swap)


def _s5_out_kernel(u_ref, u_next_ref, xp_ref, toep_ref, read_ref, d_ref, y_ref, *, c, shift):
    r = u_ref.shape[0] // c
    tokens = _chunk_tokens(u_ref, u_next_ref, c, shift)
    y = _dot(_chunk_rows(tokens), toep_ref[...]) + _dot(xp_ref[...].astype(BF16), read_ref[...])
    for t in range(c):
        y_ref[pl.ds(t, r, stride=c), :] = jax.nn.gelu(y[:, t * LANES:(t + 1) * LANES] + d_ref[...] * tokens[t])


def _s5_out(proj, u_col, xprev, toep, read, d_skip, c):
    t_tokens = proj.shape[0]
    nb, r, w = xprev.shape
    k = toep.shape[1]
    return pl.pallas_call(
        functools.partial(_s5_out_kernel, c=c, shift=u_col % LANES),
        grid=(nb,),
        in_specs=_u_specs(t_tokens, u_col) + [
            pl.BlockSpec((None, r, w), lambda b: (b, 0, 0)),
            pl.BlockSpec((None, k, k), lambda b: (b, 0, 0)),
            pl.BlockSpec((None, w, k), lambda b: (b, 0, 0)),
            pl.BlockSpec((1, LANES), lambda b: (0, b))],
        out_specs=pl.BlockSpec((t_tokens, LANES), lambda b: (0, b)),
        out_shape=jax.ShapeDtypeStruct((t_tokens, nb * LANES), F32),
        compiler_params=_params("parallel"),
        name="s5_out",
    )(proj, proj, xprev, toep, read, d_skip.reshape(1, -1))


def _s5_mixer(proj, u_col, tables, d_skip, st_re, st_im, n_prompt_seq, prompt_len):
    toep, inj, read, step_same, step_swap = tables
    c = S5_CHUNK
    nb = toep.shape[0]
    ns, groups, p = st_re.shape
    gpb = groups // nb

    def to_blocks(st):
        return st.reshape(ns, nb, gpb * p).transpose(1, 0, 2)

    def from_blocks(x):
        return x.transpose(1, 0, 2).reshape(x.shape[1], groups, p)

    e = _s5_inject(proj, u_col, inj, c)
    x0_s = jnp.concatenate([to_blocks(st_re), to_blocks(st_im)], axis=-1)
    xprev, xfin_p, xfin_s = _s5_scan(e, x0_s, step_same, step_swap, n_prompt_seq, prompt_len // c)
    y = _s5_out(proj, u_col, xprev, toep, read, d_skip, c)
    half = gpb * p
    xfin_p = xfin_p[:, :n_prompt_seq]
    return (y, (from_blocks(xfin_p[..., :half]), from_blocks(xfin_p[..., half:])),
            (from_blocks(xfin_s[..., :half]), from_blocks(xfin_s[..., half:])))


def _peer_route_kernel(q_ref, k1_ref, k2_ref, s1_ref, w1_ref, s2_ref, e2_ref, tau_ref):
    half = k1_ref.shape[1]
    neg = jnp.float32(-jnp.inf)

    def top_values(x):
        vals = []
        for _ in range(PEER_TOPK):
            m = jnp.max(x, axis=0, keepdims=True)
            vals.append(m)
            x = jnp.where(x == m, neg, x)
        return vals

    for h in range(PEER_HEADS):
        q1 = q_ref[:, (2 * h) * half:(2 * h + 1) * half]
        q2 = q_ref[:, (2 * h + 1) * half:(2 * h + 2) * half]
        s1 = lax.dot_general(k1_ref[...], q1, (((1,), (1,)), ((), ())), precision=HIGHEST,
                             preferred_element_type=F32)
        s2 = lax.dot_general(k2_ref[...], q2, (((1,), (1,)), ((), ())), precision=HIGHEST,
                             preferred_element_type=F32)
        a = top_values(s1)
        b = jnp.concatenate(top_values(s2), axis=0)
        b0 = b[0:1]
        cand = [a[i] + b[:PEER_TOPK // (i + 1)] for i in range(PEER_TOPK)]
        n_cand = sum(x.shape[0] for x in cand)
        cand.append(jnp.full((-n_cand % 8, s1.shape[1]), neg, F32))
        top = top_values(jnp.concatenate(cand, axis=0))
        zsum = top[0] * 0.0
        for m in top:
            zsum = zsum + jnp.exp(m - top[0])
        s1_ref[h] = s1
        s2_ref[h] = s2
        w1_ref[h] = jnp.exp(s1 - a[0]) / zsum
        e2_ref[h] = jnp.exp(s2 - b0)
        tau_ref[h] = top[-1]


def _peer_route(q, key1, key2, tm):
    t_tokens = q.shape[0]
    nk, half = key1.shape
    big = pl.BlockSpec((PEER_HEADS, nk, tm), lambda i: (0, 0, i))
    big_shape = jax.ShapeDtypeStruct((PEER_HEADS, nk, t_tokens), F32)
    return pl.pallas_call(
        _peer_route_kernel,
        grid=(t_tokens // tm,),
        in_specs=[pl.BlockSpec((tm, q.shape[1]), lambda i: (i, 0)),
                  pl.BlockSpec((nk, half), lambda i: (0, 0)),
                  pl.BlockSpec((nk, half), lambda i: (0, 0))],
        out_specs=[big, big, big, big, pl.BlockSpec((PEER_HEADS, 1, tm), lambda i: (0, 0, i))],
        out_shape=[big_shape, big_shape, big_shape, big_shape,
                   jax.ShapeDtypeStruct((PEER_HEADS, 1, t_tokens), F32)],
        compiler_params=_params("parallel"),
        name="peer_route",
    )(q, key1, key2)


def _peer_ffn_kernel(xt_ref, u_ref, vt_ref, s1_ref, w1_ref, s2_ref, e2_ref, tau_ref, o_ref, p_ref, h_ref, g_ref,
                     *, ni, n_et):
    step = pl.program_id(1)
    et = jnp.minimum(step, n_et - 1)
    slot = step % 2

    @pl.when(step == 0)
    def _():
        o_ref[...] = jnp.zeros_like(o_ref)
        p_ref[...] = jnp.zeros_like(p_ref)

    nk = s2_ref.shape[1]
    te, tm = h_ref.shape
    d = o_ref.shape[0]
    chunks = [(ii, tc) for ii in range(ni) for tc in range(tm // LANES)]

    def block(ii, tc):
        return slice(ii * nk, (ii + 1) * nk), slice(tc * LANES, (tc + 1) * LANES)

    def gate_chunk(ii, tc):
        rows, lanes = block(ii, tc)
        i = et * ni + ii
        g = None
        for h in range(PEER_HEADS):
            s1_row = s1_ref[h, pl.ds(i, 1), :][:, lanes]
            w1_row = w1_ref[h, pl.ds(i, 1), :][:, lanes]
            hit = (s1_row + s2_ref[h, :, lanes]) >= tau_ref[h, :, lanes]
            term = jnp.where(hit, e2_ref[h, :, lanes] * w1_row, 0.0)
            g = term if g is None else g + term
        g_ref[rows, lanes] = g

    def finish_chunk(ii, tc):
        rows, lanes = block(ii, tc)
        p_ref[slot, rows, lanes] = (g_ref[rows, lanes] * jax.nn.gelu(h_ref[rows, lanes])).astype(BF16)

    quarter = len(chunks) // 4
    half = tm // 2
    for c in chunks[:quarter]:
        gate_chunk(*c)
    h_ref[:, :half] = _dot(u_ref[...], xt_ref[:, :half])
    for c in chunks[quarter:2 * quarter]:
        gate_chunk(*c)
    h_ref[:, half:] = _dot(u_ref[...], xt_ref[:, half:])
    p_prev = p_ref[1 - slot]
    vector_work = [[(gate_chunk, c) for c in chunks[2 * quarter:3 * quarter]],
                   [(gate_chunk, c) for c in chunks[3 * quarter:]],
                   [(finish_chunk, c) for c in chunks[:2 * quarter]],
                   [(finish_chunk, c) for c in chunks[2 * quarter:]]]
    dq = d // len(vector_work)
    for k, work in enumerate(vector_work):
        for fn, c in work:
            fn(*c)
        rows = slice(k * dq, (k + 1) * dq)
        o_ref[rows, :] += _dot(vt_ref[rows, :], p_prev)


def _peer_ffn(xt_bf, u_tab, vt_tab, layer, route, tm, te):
    s1, w1, s2, e2, tau = route
    d, t_tokens = xt_bf.shape
    n_exp = u_tab.shape[1]
    nk = s2.shape[1]
    ni = te // nk
    n_et = n_exp // te
    once = pl.Buffered(1)
    tok = pl.BlockSpec((PEER_HEADS, nk, tm), lambda i, j: (0, 0, i), pipeline_mode=once)
    return pl.pallas_call(
        functools.partial(_peer_ffn_kernel, ni=ni, n_et=n_et),
        grid=(t_tokens // tm, n_et + 1),
        in_specs=[pl.BlockSpec((d, tm), lambda i, j: (0, i), pipeline_mode=once),
                  pl.BlockSpec((None, te, d), lambda i, j: (layer, jnp.minimum(j, n_et - 1), 0)),
                  pl.BlockSpec((None, d, te), lambda i, j: (layer, 0, jnp.maximum(j - 1, 0))),
                  tok, tok, tok, tok,
                  pl.BlockSpec((PEER_HEADS, 1, tm), lambda i, j: (0, 0, i), pipeline_mode=once)],
        out_specs=pl.BlockSpec((d, tm), lambda i, j: (0, i)),
        out_shape=jax.ShapeDtypeStruct((d, t_tokens), F32),
        scratch_shapes=[pltpu.VMEM((2, te, tm), BF16), pltpu.VMEM((te, tm), F32), pltpu.VMEM((te, tm), F32)],
        compiler_params=_params("parallel", "arbitrary"),
        name="peer_ffn",
    )(xt_bf, u_tab, vt_tab, s1, w1, s2, e2, tau)


def kernel(x_prompt, x_sample, state_gla, state_s5_re, state_s5_im, ln1_g, ln1_b, w_in, gla_w_a2, gla_b_a, gla_norm_g, s5_a_re, s5_a_im, s5_log_dt, s5_b_re, s5_b_im, s5_c_re, s5_c_im, s5_d, s5_w_glu, s5_b_glu, w_branch, w_out, ln2_g, ln2_b, peer_w_q, peer_key1, peer_key2, peer_u, peer_v):
    depth = w_in.shape[0]
    pb, plen, d = x_prompt.shape
    sb, slen, _ = x_sample.shape
    tp, ts = pb * plen, sb * slen
    t_all = tp + ts
    dk_tot = gla_w_a2.shape[2]
    dv_tot = GLA_HEADS * gla_norm_g.shape[1]
    s5_w = s5_d.shape[1]
    alpha = (2 * depth) ** 0.25
    widths = (dk_tot, dk_tot, dv_tot, dv_tot, GLA_RANK, s5_w, d, d)
    off = [0]
    for wdt in widths:
        off.append(off[-1] + wdt)
    lr0, lr1 = off[4], off[5]
    x = (x_prompt.reshape(tp, d), x_sample.reshape(ts, d))
    x_bf = jnp.concatenate([x[0].astype(BF16), x[1].astype(BF16)], axis=0)
    proj_tile = 4 * LANES
    proj_cols = -(-off[-1] // proj_tile) * proj_tile
    w_in_t = w_in.transpose(0, 2, 1)
    wa2 = jnp.pad(gla_w_a2, ((0, 0), (0, LANES - GLA_RANK), (0, 0)))
    u_bf = peer_u.astype(BF16)
    vt_bf = peer_v.astype(BF16).transpose(0, 2, 1)
    cols = (off[0], off[1], off[2], off[3], lr0)
    u_col, zg_col, zs_col = off[5], off[6], off[7]
    gla_p = gla_s = None
    re_p, im_p, re_s, im_s = [], [], [], []
    for l in range(depth):
        proj = _mm(x_bf, w_in_t, l, 1024, proj_tile, n=proj_cols, w_transposed=True)

        o_gla, gla_p = _gla(proj, wa2[l], gla_b_a[l], gla_norm_g[l], None, l, gla_p, None, depth=depth, row0=0,
                            n_rows=t_all, nb=pb, seq_len=plen, c=min(GLA_CHUNK, plen), nseq=1, cols=cols)
        o_gla, gla_s = _gla(proj, wa2[l], gla_b_a[l], gla_norm_g[l], state_gla, l, gla_s, o_gla, depth=depth,
                            row0=tp, n_rows=t_all, nb=sb, seq_len=slen, c=min(GLA_CHUNK, slen), nseq=2, cols=cols)

        tables = _s5_tables(s5_a_re[l], s5_a_im[l], s5_log_dt[l], s5_b_re[l], s5_b_im[l], s5_c_re[l], s5_c_im[l],
                            S5_CHUNK)
        y_s5, (r_p, i_p), (r_s, i_s) = _s5_mixer(proj, u_col, tables, s5_d[l], state_s5_re[l], state_s5_im[l],
                                                 pb, plen)
        o_s5 = _glu(y_s5, s5_w_glu, l, s5_b_glu[l], 1024, 1024)

        merged = _branch_merge(o_gla, o_s5, w_branch, l, proj, zg_col, zs_col, 1024, 512)
        mix = _mm(merged, w_out, l, 1024, proj_tile)
        x, x_bf, xt_bf = _deepnorm_ln(x, mix, ln1_g[l], ln1_b[l], alpha, 256, emit_transposed=True)

        q = _mm(x_bf, peer_w_q, l, 1024, proj_tile)
        route = _peer_route(q, peer_key1[l], peer_key2[l], 256)
        ffn_t = _peer_ffn(xt_bf, u_bf, vt_bf, l, route, 512, 512)
        if l + 1 < depth:
            x, x_bf = _deepnorm_ln(x, ffn_t, ln2_g[l], ln2_b[l], alpha, 256, y_transposed=True)
        else:
            y_p, y_s = _deepnorm_ln(x, ffn_t, ln2_g[l], ln2_b[l], alpha, 256, y_transposed=True,
                                    emit_bf16=False, split_out=tp)

        re_p.append(r_p)
        im_p.append(i_p)
        re_s.append(r_s)
        im_s.append(i_s)

    return (y_p.reshape(pb, plen, d), y_s.reshape(sb, slen, d), gla_p, jnp.stack(re_p), jnp.stack(im_p),
            gla_s, jnp.stack(re_s), jnp.stack(im_s))
```

```python
import functools

import jax
import jax.numpy as jnp
from jax import lax
from jax.experimental import pallas as pl
from jax.experimental.pallas import tpu as pltpu

F32 = jnp.float32
BF16 = jnp.bfloat16
HIGHEST = lax.Precision.HIGHEST

V7X_VMEM_BYTES = 64 * 1024 * 1024
VMEM_LIMIT_BYTES = V7X_VMEM_BYTES - 8 * 1024 * 1024
LANES = 128

GLA_HEADS = 4
GLA_RANK = 16
GLA_TAU = 16.0
GLA_CHUNK = 64
RMS_EPS = 1e-6
S5_CHUNK = 8
PEER_HEADS = 8
PEER_TOPK = 16
LN_EPS = 1e-5


def _params(*semantics, flags=None):
    return pltpu.CompilerParams(dimension_semantics=semantics, vmem_limit_bytes=VMEM_LIMIT_BYTES, flags=flags)


def _dot(a, b):
    return jnp.dot(a, b, preferred_element_type=F32)


def _dot_nt(a, b):
    return lax.dot_general(a, b, (((1,), (1,)), ((), ())), preferred_element_type=F32)


def _mm_kernel(a_ref, b_ref, o_ref):
    o_ref[...] = _dot(a_ref[...], b_ref[...].astype(BF16)).astype(o_ref.dtype)


def _layer_cols(w, layer, tn, row0=0):
    def spec(rows):
        return pl.BlockSpec((None, rows, tn), lambda i, j: (layer, row0, j))
    return spec


def _mm_nt_kernel(a_ref, bt_ref, o_ref):
    o_ref[...] = _dot_nt(a_ref[...], bt_ref[...].astype(BF16)).astype(o_ref.dtype)


def _mm(a, w, layer, tm, tn, out_dtype=F32, w_transposed=False, n_out=None):
    m, k = a.shape
    n = (w.shape[1] if w_transposed else w.shape[2]) // tn * tn
    w_spec = (pl.BlockSpec((None, tn, k), lambda i, j: (layer, j, 0)) if w_transposed
              else _layer_cols(w, layer, tn)(k))
    return pl.pallas_call(
        _mm_nt_kernel if w_transposed else _mm_kernel,
        grid=(m // tm, n // tn),
        in_specs=[pl.BlockSpec((tm, k), lambda i, j: (i, 0)), w_spec],
        out_specs=pl.BlockSpec((tm, tn), lambda i, j: (i, j)),
        out_shape=jax.ShapeDtypeStruct((m, n_out or n), out_dtype),
        compiler_params=_params("parallel", "parallel"),
        name="mm",
    )(a, w)


def _mm_tail_kernel(a_ref, bt_ref, prev_ref, o_ref):
    del prev_ref
    tail = _dot_nt(a_ref[...], bt_ref[...].astype(BF16))
    o_ref[...] = jnp.concatenate([tail, jnp.zeros((tail.shape[0], o_ref.shape[1] - tail.shape[1]), F32)], axis=1)


def _mm_tail(a, w_t, layer, out, tm, col0):
    m, k = a.shape
    rows = w_t.shape[1] - col0
    assert 0 < rows < LANES and rows % 8 == 0 and col0 % rows == 0 and col0 % LANES == 0
    return pl.pallas_call(
        _mm_tail_kernel,
        grid=(m // tm,),
        in_specs=[pl.BlockSpec((tm, k), lambda i: (i, 0)),
                  pl.BlockSpec((None, rows, k), lambda i: (layer, col0 // rows, 0)),
                  pl.BlockSpec(memory_space=pl.ANY)],
        out_specs=pl.BlockSpec((tm, LANES), lambda i: (i, col0 // LANES)),
        out_shape=jax.ShapeDtypeStruct(out.shape, out.dtype),
        input_output_aliases={2: 0},
        compiler_params=_params("parallel"),
        name="mm_tail",
    )(a, w_t, out)


def _glu_kernel(y_ref, w_ref, b_ref, ytile_ref, o_ref):
    acc = _dot(y_ref[...].astype(BF16), w_ref[...].astype(BF16)) + b_ref[...]
    o_ref[...] = (ytile_ref[...] * jax.nn.sigmoid(acc)).astype(o_ref.dtype)


def _glu(y, w, layer, b, tm, tn):
    m, k = y.shape
    n = w.shape[2]
    return pl.pallas_call(
        _glu_kernel,
        grid=(m // tm, n // tn),
        in_specs=[pl.BlockSpec((tm, k), lambda i, j: (i, 0)),
                  _layer_cols(w, layer, tn)(k),
                  pl.BlockSpec((1, tn), lambda i, j: (0, j)),
                  pl.BlockSpec((tm, tn), lambda i, j: (i, j))],
        out_specs=pl.BlockSpec((tm, tn), lambda i, j: (i, j)),
        out_shape=jax.ShapeDtypeStruct((m, n), BF16),
        compiler_params=_params("parallel", "parallel"),
        name="s5_glu",
    )(y, w, b.reshape(1, n), y)


def _lane_shifted(a, b, shift):
    if shift == 0:
        return a
    return jnp.concatenate([a[:, shift:], b[:, :shift]], axis=1)


def _branch_kernel(og_ref, os_ref, w1_ref, w2_ref, zg_ref, zg_next_ref, zs_ref, zs_next_ref, o_ref, *, shift):
    m1 = _dot(og_ref[...], w1_ref[...].astype(BF16))
    m2 = _dot(os_ref[...], w2_ref[...].astype(BF16))
    zg = _lane_shifted(zg_ref[...], zg_next_ref[...], shift)
    zs = _lane_shifted(zs_ref[...], zs_next_ref[...], shift)
    o_ref[...] = (jax.nn.sigmoid(zg) * m1 + jax.nn.sigmoid(zs) * m2).astype(o_ref.dtype)


def _branch_merge(o_gla, o_s5, w, layer, proj, zg_col, zs_col, tm, tn):
    m, k = o_gla.shape
    assert o_s5.shape[1] == k and w.shape[1] == 2 * k
    n = w.shape[2]
    shift = zg_col % LANES
    assert zs_col % LANES == shift and (zg_col - shift) % tn == 0 and (zs_col - shift) % tn == 0

    def gate_specs(col):
        blk, nxt, per = (col - shift) // tn, (col - shift) // LANES, tn // LANES
        return [pl.BlockSpec((tm, tn), lambda i, j: (i, blk + j)),
                pl.BlockSpec((tm, LANES), lambda i, j: (i, nxt + per * (j + 1)))]

    return pl.pallas_call(
        functools.partial(_branch_kernel, shift=shift),
        grid=(m // tm, n // tn),
        in_specs=[pl.BlockSpec((tm, k), lambda i, j: (i, 0)),
                  pl.BlockSpec((tm, k), lambda i, j: (i, 0)),
                  _layer_cols(w, layer, tn, 0)(k),
                  _layer_cols(w, layer, tn, 1)(k)] + gate_specs(zg_col) + gate_specs(zs_col),
        out_specs=pl.BlockSpec((tm, tn), lambda i, j: (i, j)),
        out_shape=jax.ShapeDtypeStruct((m, n), BF16),
        compiler_params=_params("parallel", "parallel"),
        name="branch_merge",
    )(o_gla, o_s5, w, w, proj, proj, proj, proj)


def _ln_kernel(*refs, alpha, y_transposed, n_head, two_in, two_out, emit_bf16, emit_transposed):
    refs = list(refs)
    step = pl.program_id(0)
    if two_in:
        x = jnp.where(step < n_head, refs[0][...], refs[1][...])
        refs = refs[2:]
    else:
        x = refs[0][...]
        refs = refs[1:]
    y_ref, g_ref, b_ref = refs[:3]
    outs = refs[3:]
    y = y_ref[...].T if y_transposed else y_ref[...]
    h = alpha * x + y
    mu = jnp.mean(h, axis=-1, keepdims=True)
    d = h - mu
    var = jnp.mean(d * d, axis=-1, keepdims=True)
    o = d * lax.rsqrt(var + LN_EPS) * g_ref[...] + b_ref[...]
    if two_out:
        head_ref, tail_ref = outs[:2]
        outs = outs[2:]

        @pl.when(step < n_head)
        def _():
            head_ref[...] = o

        @pl.when(step >= n_head)
        def _():
            tail_ref[...] = o
    else:
        outs[0][...] = o
        outs = outs[1:]
    if emit_bf16:
        outs[0][...] = o.astype(BF16)
        outs = outs[1:]
    if emit_transposed:
        outs[0][...] = o.T.astype(BF16)


def _deepnorm_ln(x, y, g, b, alpha, tm, y_transposed=False, emit_bf16=True, emit_transposed=False, split_out=None):
    xs = x if isinstance(x, (tuple, list)) else (x,)
    d = xs[0].shape[1]
    m = sum(a.shape[0] for a in xs)
    two_in, two_out = len(xs) == 2, split_out is not None
    n_head = (xs[0].shape[0] if two_in else split_out if two_out else m) // tm
    assert not (two_in and two_out) or xs[0].shape[0] == split_out
    row = pl.BlockSpec((tm, d), lambda i: (i, 0))
    head = pl.BlockSpec((tm, d), lambda i: (jnp.minimum(i, n_head - 1), 0))
    tail = pl.BlockSpec((tm, d), lambda i: (jnp.maximum(i - n_head, 0), 0))
    col = pl.BlockSpec((d, tm), lambda i: (0, i))
    vec = pl.BlockSpec((1, d), lambda i: (0, 0))
    out_specs = [head, tail] if two_out else [row]
    out_shape = ([jax.ShapeDtypeStruct((split_out, d), F32), jax.ShapeDtypeStruct((m - split_out, d), F32)]
                 if two_out else [jax.ShapeDtypeStruct((m, d), F32)])
    if emit_bf16:
        out_specs.append(row)
        out_shape.append(jax.ShapeDtypeStruct((m, d), BF16))
    if emit_transposed:
        out_specs.append(col)
        out_shape.append(jax.ShapeDtypeStruct((d, m), BF16))
    kern = functools.partial(_ln_kernel, alpha=alpha, y_transposed=y_transposed, n_head=n_head, two_in=two_in,
                             two_out=two_out, emit_bf16=emit_bf16, emit_transposed=emit_transposed)
    return pl.pallas_call(
        kern,
        grid=(m // tm,),
        in_specs=([head, tail] if two_in else [row]) + [col if y_transposed else row, vec, vec],
        out_specs=out_specs,
        out_shape=out_shape,
        compiler_params=_params("arbitrary"),
        name="deepnorm_ln",
    )(*xs, y, g.reshape(1, d), b.reshape(1, d))


def _gla_kernel(*refs, c, nseq, heads, scale, has_init, n_prev):
    refs = list(refs)
    q_ref, k_ref, v_ref, r_ref, alr_ref, wa2_ref, ba_ref, g_ref = refs[:8]
    rest = refs[8:]
    s0_ref = rest.pop(0) if has_init else None
    o_ref, s_ref = rest[n_prev:]
    dk = q_ref.shape[-1] // heads
    dv = v_ref.shape[-1] // heads
    kpad = max(c, LANES)
    qpad = max(c, 16)

    @pl.when(pl.program_id(1) == 0)
    def _():
        if has_init:
            s_ref[...] = s0_ref[...]
        else:
            s_ref[...] = jnp.zeros_like(s_ref)

    row = lax.broadcasted_iota(jnp.int32, (c, c), 0)
    col = lax.broadcasted_iota(jnp.int32, (c, c), 1)
    tri_cc = (row >= col).astype(F32)
    rowp = lax.broadcasted_iota(jnp.int32, (qpad, kpad), 0)
    colp = lax.broadcasted_iota(jnp.int32, (qpad, kpad), 1)
    causal = rowp >= colp
    eye = lax.broadcasted_iota(jnp.int32, (dk, dk), 0) == lax.broadcasted_iota(jnp.int32, (dk, dk), 1)
    eye_bf = eye.astype(BF16)

    def pad_rows(x, rows):
        if x.shape[0] == rows:
            return x
        return jnp.concatenate([x, jnp.zeros((rows - x.shape[0], x.shape[1]), x.dtype)], axis=0)

    outs = []
    for s in range(nseq):
        rows = slice(s * c, (s + 1) * c)
        z = jnp.dot(alr_ref[rows, :], wa2_ref[...], precision=HIGHEST, preferred_element_type=F32) + ba_ref[...]
        log_a = (jnp.minimum(z, 0.0) - jnp.log1p(jnp.exp(-jnp.abs(z)))) * (1.0 / GLA_TAU)
        if c >= LANES // 2:
            b = jnp.dot(tri_cc, log_a, precision=HIGHEST, preferred_element_type=F32)
        else:
            ridx = lax.broadcasted_iota(jnp.int32, log_a.shape, 0)
            b = jnp.zeros_like(log_a)
            for t in range(c):
                b = b + jnp.where(ridx >= t, log_a[t:t + 1, :], 0.0)
        b_last = b[c - 1:c, :]
        k = k_ref[rows, :]
        q_dec_all = (q_ref[rows, :] * scale * jnp.exp(b)).astype(BF16)
        k_inv_all = (k * jnp.exp(-b)).astype(BF16)
        k_end_all = (k * jnp.exp(b_last - b)).astype(BF16)
        v_all = v_ref[rows, :].astype(BF16)
        r = r_ref[rows, :]
        gate = r * jax.nn.sigmoid(r)
        o_heads = []
        for h in range(heads):
            kc = slice(h * dk, (h + 1) * dk)
            vc = slice(h * dv, (h + 1) * dv)
            q_dec = pad_rows(q_dec_all[:, kc], qpad)
            k_inv = pad_rows(k_inv_all[:, kc], kpad)
            k_end = pad_rows(k_end_all[:, kc], kpad)
            v_bf = pad_rows(v_all[:, vc], kpad)
            scores = jnp.where(causal, _dot_nt(q_dec, k_inv), 0.0).astype(BF16)
            state = s_ref[s, h]
            o = (_dot(q_dec, state.astype(BF16)) + _dot(scores, v_bf))[:c]
            k_end_t = _dot_nt(eye_bf, k_end).astype(BF16)
            decay_col = jnp.exp(jnp.sum(jnp.where(eye, b_last[:, kc], 0.0), axis=1, keepdims=True))
            s_ref[s, h] = decay_col * state + _dot(k_end_t, v_bf)
            o = o * lax.rsqrt(jnp.mean(o * o, axis=-1, keepdims=True) + RMS_EPS) * g_ref[...]
            o_heads.append(o * gate[:, vc])
        outs.append(jnp.concatenate(o_heads, axis=1))
    o_all = outs[0] if nseq == 1 else jnp.concatenate(outs, axis=0)
    o_ref[...] = o_all.astype(o_ref.dtype)


def _gla(proj, wa2, ba, norm_g, s0, layer, s_prev, o_prev, *, depth, row0, n_rows, nb, seq_len, c, nseq, cols):
    heads = GLA_HEADS
    dk_tot = wa2.shape[1]
    dk = dk_tot // heads
    dv = norm_g.shape[0]
    dv_tot = heads * dv
    nchunk = seq_len // c
    assert nseq == 1 or nchunk == 1
    rb = nseq * c
    blk0 = row0 // rb
    q_col, k_col, v_col, r_col, lr_col = cols

    def rows_map(col_blk):
        return lambda b, j: (blk0 + b * nchunk + j, col_blk)

    in_specs = [
        pl.BlockSpec((rb, dk_tot), rows_map(q_col // dk_tot)),
        pl.BlockSpec((rb, dk_tot), rows_map(k_col // dk_tot)),
        pl.BlockSpec((rb, dv_tot), rows_map(v_col // dv_tot)),
        pl.BlockSpec((rb, dv_tot), rows_map(r_col // dv_tot)),
        pl.BlockSpec((rb, LANES), rows_map(lr_col // LANES)),
        pl.BlockSpec((LANES, dk_tot), lambda b, j: (0, 0)),
        pl.BlockSpec((1, dk_tot), lambda b, j: (0, 0)),
        pl.BlockSpec((1, dv), lambda b, j: (0, 0)),
    ]
    args = [proj, proj, proj, proj, proj, wa2, ba.reshape(1, -1), norm_g.reshape(1, dv)]
    state_spec = pl.BlockSpec((None, nseq, heads, dk, dv), lambda b, j: (layer, b, 0, 0, 0))
    if s0 is not None:
        in_specs.append(state_spec)
        args.append(s0)
    aliases = {}
    for prev, out_idx in ((s_prev, 1), (o_prev, 0)):
        if prev is not None:
            aliases[len(args)] = out_idx
            in_specs.append(pl.BlockSpec(memory_space=pl.ANY))
            args.append(prev)
    kern = functools.partial(_gla_kernel, c=c, nseq=nseq, heads=heads, scale=dk ** -0.5,
                             has_init=s0 is not None, n_prev=len(aliases))
    return pl.pallas_call(
        kern,
        grid=(nb // nseq, nchunk),
        in_specs=in_specs,
        out_specs=[pl.BlockSpec((rb, dv_tot), rows_map(0)), state_spec],
        out_shape=[jax.ShapeDtypeStruct((n_rows, dv_tot), BF16),
                   jax.ShapeDtypeStruct((depth, nb, heads, dk, dv), F32)],
        input_output_aliases=aliases,
        compiler_params=_params("parallel", "arbitrary"),
        name="gla",
    )(*args)


def _s5_tables(a_re, a_im, log_dt, b_re, b_im, c_re, c_im, c):
    dt = jnp.exp(log_dt)[:, None]
    mag = jnp.exp(a_re * dt)
    ar, ai = mag * jnp.cos(a_im * dt), mag * jnp.sin(a_im * dt)
    den = a_re * a_re + a_im * a_im
    cr = ((ar - 1.0) * a_re + ai * a_im) / den
    ci = (ai * a_re - (ar - 1.0) * a_im) / den
    bbr = cr[..., None] * b_re - ci[..., None] * b_im
    bbi = cr[..., None] * b_im + ci[..., None] * b_re
    tau = jnp.arange(c + 1, dtype=F32)[:, None, None]
    pmag = jnp.exp(tau * (a_re * dt))
    pw_r, pw_i = pmag * jnp.cos(tau * (a_im * dt)), pmag * jnp.sin(tau * (a_im * dt))
    ca_r = c_re[None] * pw_r[:, :, None, :] - c_im[None] * pw_i[:, :, None, :]
    ca_i = c_re[None] * pw_i[:, :, None, :] + c_im[None] * pw_r[:, :, None, :]
    kern = (jnp.einsum('tgmp,gpn->tgmn', ca_r[:c], bbr, precision=HIGHEST)
            - jnp.einsum('tgmp,gpn->tgmn', ca_i[:c], bbi, precision=HIGHEST))
    g, m, n = kern.shape[1:]
    p = a_re.shape[1]
    gpb = LANES // n
    nb = g // gpb

    def block_diag(x):
        lead, _, r, w = x.shape
        copies = jnp.tile(jnp.eye(w, dtype=F32), (1, gpb))
        wide = jnp.dot(x.reshape(-1, w), copies, precision=HIGHEST).reshape(lead, nb, gpb * r, gpb * w)
        return wide * jnp.kron(jnp.eye(gpb, dtype=F32), jnp.ones((r, w), F32))

    lag_blocks = block_diag(kern.transpose(0, 1, 3, 2))
    zero_block = jnp.zeros_like(lag_blocks[0])
    toep = jnp.concatenate(
        [jnp.concatenate([lag_blocks[t - s] if t >= s else zero_block for s in range(c)], axis=1)
         for t in range(c)], axis=2)
    back = (c - 1.0) - tau[:c]
    rev_mag = jnp.exp(back * (a_re * dt))
    rev_r, rev_i = rev_mag * jnp.cos(back * (a_im * dt)), rev_mag * jnp.sin(back * (a_im * dt))
    inj_r = rev_r[..., None] * bbr[None] - rev_i[..., None] * bbi[None]
    inj_i = rev_r[..., None] * bbi[None] + rev_i[..., None] * bbr[None]
    inj = jnp.concatenate([block_diag(x.transpose(0, 1, 3, 2)) for x in (inj_r, inj_i)], axis=3)
    inj = inj.transpose(1, 0, 2, 3).reshape(nb, c * LANES, 2 * gpb * p)
    read = jnp.concatenate(
        [jnp.concatenate(list(block_diag(x.transpose(0, 1, 3, 2))), axis=2)
         for x in (ca_r[1:], -ca_i[1:])], axis=1)
    step_r, step_i = pw_r[c].reshape(nb, 1, gpb * p), pw_i[c].reshape(nb, 1, gpb * p)
    step_same = jnp.concatenate([step_r, step_r], axis=-1)
    step_swap = jnp.concatenate([-step_i, step_i], axis=-1)
    return toep.astype(BF16), inj.astype(BF16), read.astype(BF16), step_same, step_swap


def _chunk_tokens(u_ref, u_next_ref, c, shift):
    r = u_ref.shape[0] // c
    return [_lane_shifted(u_ref[pl.ds(s, r, stride=c), :], u_next_ref[pl.ds(s, r, stride=c), :], shift)
            for s in range(c)]


def _chunk_rows(tokens):
    return jnp.concatenate([x.astype(BF16) for x in tokens], axis=1)


def _u_specs(t_tokens, u_col):
    blk = u_col // LANES
    return [pl.BlockSpec((t_tokens, LANES), lambda b: (0, blk + b)),
            pl.BlockSpec((t_tokens, LANES), lambda b: (0, blk + b + 1))]


def _s5_inject_kernel(u_ref, u_next_ref, w_ref, e_ref, *, c, shift):
    e_ref[...] = _dot(_chunk_rows(_chunk_tokens(u_ref, u_next_ref, c, shift)), w_ref[...])


def _s5_inject(proj, u_col, inj, c):
    t_tokens = proj.shape[0]
    nb, k, w = inj.shape
    return pl.pallas_call(
        functools.partial(_s5_inject_kernel, c=c, shift=u_col % LANES),
        grid=(nb,),
        in_specs=_u_specs(t_tokens, u_col) + [pl.BlockSpec((None, k, w), lambda b: (b, 0, 0))],
        out_specs=pl.BlockSpec((None, t_tokens // c, w), lambda b: (b, 0, 0)),
        out_shape=jax.ShapeDtypeStruct((nb, t_tokens // c, w), F32),
        compiler_params=_params("parallel"),
        name="s5_inject",
    )(proj, proj, inj)


def _s5_scan_kernel(e_ref, x0_ref, same_ref, swap_ref, xprev_ref, xfin_p_ref, xfin_s_ref, *, n_seq, n_chunk):
    half = e_ref.shape[-1] // 2
    same = same_ref[...]
    swap = swap_ref[...]

    def advance(x, e):
        x_swapped = jnp.concatenate([x[:, half:], x[:, :half]], axis=1)
        return same * x + swap * x_swapped + e

    def body(j, xs):
        new = []
        for q, x in enumerate(xs):
            row = q * n_chunk + j
            xprev_ref[pl.ds(row, 1), :] = x
            new.append(advance(x, e_ref[pl.ds(row, 1), :]))
        return tuple(new)

    zero = jnp.zeros((1, e_ref.shape[-1]), F32)
    xs = lax.fori_loop(0, n_chunk, body, (zero,) * n_seq)
    xfin_p_ref[...] = jnp.zeros_like(xfin_p_ref)
    for q, x in enumerate(xs):
        xfin_p_ref[q:q + 1, :] = x
    rp = n_seq * n_chunk
    x0 = x0_ref[...]
    xprev_ref[rp:, :] = x0
    xfin_s_ref[...] = advance(x0, e_ref[rp:, :])


def _s5_scan(e, x0_s, same, swap, n_seq, n_chunk):
    nb, r, w = e.shape
    ns = x0_s.shape[1]
    assert r == n_seq * n_chunk + ns
    pad_seq = -(-n_seq // 8) * 8
    vec = pl.BlockSpec((None, 1, w), lambda b: (b, 0, 0))
    return pl.pallas_call(
        functools.partial(_s5_scan_kernel, n_seq=n_seq, n_chunk=n_chunk),
        grid=(nb,),
        in_specs=[pl.BlockSpec((None, r, w), lambda b: (b, 0, 0)),
                  pl.BlockSpec((None, ns, w), lambda b: (b, 0, 0)), vec, vec],
        out_specs=[pl.BlockSpec((None, r, w), lambda b: (b, 0, 0)),
                   pl.BlockSpec((None, pad_seq, w), lambda b: (b, 0, 0)),
                   pl.BlockSpec((None, ns, w), lambda b: (b, 0, 0))],
        out_shape=[jax.ShapeDtypeStruct((nb, r, w), F32), jax.ShapeDtypeStruct((nb, pad_seq, w), F32),
                   jax.ShapeDtypeStruct((nb, ns, w), F32)],
        compiler_params=_params("parallel"),
        name="s5_scan",
    )(e, x0_s, same, swap)


def _s5_out_kernel(u_ref, u_next_ref, xp_ref, toep_ref, read_ref, d_ref, y_ref, *, c, shift):
    r = u_ref.shape[0] // c
    tokens = _chunk_tokens(u_ref, u_next_ref, c, shift)
    y = _dot(_chunk_rows(tokens), toep_ref[...]) + _dot(xp_ref[...].astype(BF16), read_ref[...])
    for t in range(c):
        y_ref[pl.ds(t, r, stride=c), :] = jax.nn.gelu(y[:, t * LANES:(t + 1) * LANES] + d_ref[...] * tokens[t])


def _s5_out(proj, u_col, xprev, toep, read, d_skip, c):
    t_tokens = proj.shape[0]
    nb, r, w = xprev.shape
    k = toep.shape[1]
    return pl.pallas_call(
        functools.partial(_s5_out_kernel, c=c, shift=u_col % LANES),
        grid=(nb,),
        in_specs=_u_specs(t_tokens, u_col) + [
            pl.BlockSpec((None, r, w), lambda b: (b, 0, 0)),
            pl.BlockSpec((None, k, k), lambda b: (b, 0, 0)),
            pl.BlockSpec((None, w, k), lambda b: (b, 0, 0)),
            pl.BlockSpec((1, LANES), lambda b: (0, b))],
        out_specs=pl.BlockSpec((t_tokens, LANES), lambda b: (0, b)),
        out_shape=jax.ShapeDtypeStruct((t_tokens, nb * LANES), F32),
        compiler_params=_params("parallel"),
        name="s5_out",
    )(proj, proj, xprev, toep, read, d_skip.reshape(1, -1))


def _s5_mixer(proj, u_col, tables, d_skip, st_re, st_im, n_prompt_seq, prompt_len):
    toep, inj, read, step_same, step_swap = tables
    c = S5_CHUNK
    nb = toep.shape[0]
    ns, groups, p = st_re.shape
    gpb = groups // nb

    def to_blocks(st):
        return st.reshape(ns, nb, gpb * p).transpose(1, 0, 2)

    def from_blocks(x):
        return x.transpose(1, 0, 2).reshape(x.shape[1], groups, p)

    e = _s5_inject(proj, u_col, inj, c)
    x0_s = jnp.concatenate([to_blocks(st_re), to_blocks(st_im)], axis=-1)
    xprev, xfin_p, xfin_s = _s5_scan(e, x0_s, step_same, step_swap, n_prompt_seq, prompt_len // c)
    y = _s5_out(proj, u_col, xprev, toep, read, d_skip, c)
    half = gpb * p
    xfin_p = xfin_p[:, :n_prompt_seq]
    return (y, (from_blocks(xfin_p[..., :half]), from_blocks(xfin_p[..., half:])),
            (from_blocks(xfin_s[..., :half]), from_blocks(xfin_s[..., half:])))


def _sorting_network(n):
    pairs = []
    p = 1
    while p < n:
        k = p
        while k >= 1:
            for j in range(k % p, n - k, 2 * k):
                for i in range(min(k, n - j - k)):
                    if (i + j) // (2 * p) == (i + j + k) // (2 * p):
                        pairs.append((i + j, i + j + k))
            k //= 2
        p *= 2
    return pairs


def _peer_route_kernel(q_ref, k1_ref, k2_ref, s1_ref, w1_ref, s2_ref, e2_ref, tau_ref):
    half = k1_ref.shape[1]
    neg = jnp.float32(-jnp.inf)

    def top_values(x):
        slabs = [x[r:r + 8] for r in range(0, x.shape[0], 8)]
        for lo, hi in _sorting_network(len(slabs)):
            slabs[lo], slabs[hi] = jnp.maximum(slabs[lo], slabs[hi]), jnp.minimum(slabs[lo], slabs[hi])
        vals = []
        for taken in range(PEER_TOPK):
            m = jnp.max(slabs[0], axis=0, keepdims=True)
            vals.append(m)
            depth = min(len(slabs), PEER_TOPK - taken - 1)
            popped = slabs[0] == m
            slabs = [jnp.where(popped, slabs[k + 1] if k + 1 < len(slabs) else neg, slabs[k])
                     for k in range(depth)]
        return vals

    for h in range(PEER_HEADS):
        q1 = q_ref[:, (2 * h) * half:(2 * h + 1) * half]
        q2 = q_ref[:, (2 * h + 1) * half:(2 * h + 2) * half]
        s1 = lax.dot_general(k1_ref[...], q1, (((1,), (1,)), ((), ())), precision=HIGHEST,
                             preferred_element_type=F32)
        s2 = lax.dot_general(k2_ref[...], q2, (((1,), (1,)), ((), ())), precision=HIGHEST,
                             preferred_element_type=F32)
        a = top_values(s1)
        b = jnp.concatenate(top_values(s2), axis=0)
        b0 = b[0:1]
        cand = [a[i] + b[:PEER_TOPK // (i + 1)] for i in range(PEER_TOPK)]
        n_cand = sum(x.shape[0] for x in cand)
        n_rows = 8 * pl.next_power_of_2(-(-n_cand // 8))
        cand.append(jnp.full((n_rows - n_cand, s1.shape[1]), neg, F32))
        top = top_values(jnp.concatenate(cand, axis=0))
        zsum = top[0] * 0.0
        for m in top:
            zsum = zsum + jnp.exp(m - top[0])
        s1_ref[h] = s1
        s2_ref[h] = s2
        w1_ref[h] = jnp.exp(s1 - a[0]) / zsum
        e2_ref[h] = jnp.exp(s2 - b0)
        tau_ref[h] = top[-1]


def _peer_route(q, key1, key2, tm):
    t_tokens = q.shape[0]
    nk, half = key1.shape
    big = pl.BlockSpec((PEER_HEADS, nk, tm), lambda i: (0, 0, i))
    big_shape = jax.ShapeDtypeStruct((PEER_HEADS, nk, t_tokens), F32)
    return pl.pallas_call(
        _peer_route_kernel,
        grid=(t_tokens // tm,),
        in_specs=[pl.BlockSpec((tm, q.shape[1]), lambda i: (i, 0)),
                  pl.BlockSpec((nk, half), lambda i: (0, 0)),
                  pl.BlockSpec((nk, half), lambda i: (0, 0))],
        out_specs=[big, big, big, big, pl.BlockSpec((PEER_HEADS, 1, tm), lambda i: (0, 0, i))],
        out_shape=[big_shape, big_shape, big_shape, big_shape,
                   jax.ShapeDtypeStruct((PEER_HEADS, 1, t_tokens), F32)],
        compiler_params=_params("parallel"),
        name="peer_route",
    )(q, key1, key2)


def _peer_ffn_kernel(xt_ref, u_ref, vt_ref, s1_ref, w1_ref, s2_ref, e2_ref, tau_ref, o_ref, p_ref, h_ref, g_ref,
                     *, ni, n_et):
    step = pl.program_id(1)
    et = jnp.minimum(step, n_et - 1)
    slot = step % 2

    @pl.when(step == 0)
    def _():
        o_ref[...] = jnp.zeros_like(o_ref)
        p_ref[...] = jnp.zeros_like(p_ref)

    nk = s2_ref.shape[1]
    te, tm = h_ref.shape
    d = o_ref.shape[0]
    chunks = [(ii, tc) for ii in range(ni) for tc in range(tm // LANES)]

    def block(ii, tc):
        return slice(ii * nk, (ii + 1) * nk), slice(tc * LANES, (tc + 1) * LANES)

    def gate_chunk(ii, tc):
        rows, lanes = block(ii, tc)
        i = et * ni + ii
        g = None
        for h in range(PEER_HEADS):
            s1_row = s1_ref[h, pl.ds(i, 1), :][:, lanes]
            w1_row = w1_ref[h, pl.ds(i, 1), :][:, lanes]
            hit = (s1_row + s2_ref[h, :, lanes]) >= tau_ref[h, :, lanes]
            term = jnp.where(hit, e2_ref[h, :, lanes] * w1_row, 0.0)
            g = term if g is None else g + term
        g_ref[rows, lanes] = g

    def finish_chunk(ii, tc):
        rows, lanes = block(ii, tc)
        p_ref[slot, rows, lanes] = (g_ref[rows, lanes] * jax.nn.gelu(h_ref[rows, lanes])).astype(BF16)

    quarter = len(chunks) // 4
    half = tm // 2
    for c in chunks[:quarter]:
        gate_chunk(*c)
    h_ref[:, :half] = _dot(u_ref[...], xt_ref[:, :half])
    for c in chunks[quarter:2 * quarter]:
        gate_chunk(*c)
    h_ref[:, half:] = _dot(u_ref[...], xt_ref[:, half:])
    p_prev = p_ref[1 - slot]
    vector_work = [[(gate_chunk, c) for c in chunks[2 * quarter:3 * quarter]],
                   [(gate_chunk, c) for c in chunks[3 * quarter:]],
                   [(finish_chunk, c) for c in chunks[:2 * quarter]],
                   [(finish_chunk, c) for c in chunks[2 * quarter:]]]
    dq = d // len(vector_work)
    for k, work in enumerate(vector_work):
        for fn, c in work:
            fn(*c)
        rows = slice(k * dq, (k + 1) * dq)
        o_ref[rows, :] += _dot(vt_ref[rows, :], p_prev)


def _peer_ffn(xt_bf, u_tab, vt_tab, layer, route, tm, te):
    s1, w1, s2, e2, tau = route
    d, t_tokens = xt_bf.shape
    n_exp = u_tab.shape[1]
    nk = s2.shape[1]
    ni = te // nk
    n_et = n_exp // te
    once = pl.Buffered(1)
    tok = pl.BlockSpec((PEER_HEADS, nk, tm), lambda i, j: (0, 0, i), pipeline_mode=once)
    return pl.pallas_call(
        functools.partial(_peer_ffn_kernel, ni=ni, n_et=n_et),
        grid=(t_tokens // tm, n_et + 1),
        in_specs=[pl.BlockSpec((d, tm), lambda i, j: (0, i), pipeline_mode=once),
                  pl.BlockSpec((None, te, d), lambda i, j: (layer, jnp.minimum(j, n_et - 1), 0)),
                  pl.BlockSpec((None, d, te), lambda i, j: (layer, 0, jnp.maximum(j - 1, 0))),
                  tok, tok, tok, tok,
                  pl.BlockSpec((PEER_HEADS, 1, tm), lambda i, j: (0, 0, i), pipeline_mode=once)],
        out_specs=pl.BlockSpec((d, tm), lambda i, j: (0, i)),
        out_shape=jax.ShapeDtypeStruct((d, t_tokens), F32),
        scratch_shapes=[pltpu.VMEM((2, te, tm), BF16), pltpu.VMEM((te, tm), F32), pltpu.VMEM((te, tm), F32)],
        compiler_params=_params("parallel", "arbitrary"),
        name="peer_ffn",
    )(xt_bf, u_tab, vt_tab, s1, w1, s2, e2, tau)


def kernel(x_prompt, x_sample, state_gla, state_s5_re, state_s5_im, ln1_g, ln1_b, w_in, gla_w_a2, gla_b_a, gla_norm_g, s5_a_re, s5_a_im, s5_log_dt, s5_b_re, s5_b_im, s5_c_re, s5_c_im, s5_d, s5_w_glu, s5_b_glu, w_branch, w_out, ln2_g, ln2_b, peer_w_q, peer_key1, peer_key2, peer_u, peer_v):
    depth = w_in.shape[0]
    pb, plen, d = x_prompt.shape
    sb, slen, _ = x_sample.shape
    tp, ts = pb * plen, sb * slen
    t_all = tp + ts
    dk_tot = gla_w_a2.shape[2]
    dv_tot = GLA_HEADS * gla_norm_g.shape[1]
    s5_w = s5_d.shape[1]
    alpha = (2 * depth) ** 0.25
    widths = (dk_tot, dk_tot, dv_tot, dv_tot, GLA_RANK, s5_w, d, d)
    off = [0]
    for wdt in widths:
        off.append(off[-1] + wdt)
    lr0, lr1 = off[4], off[5]
    x = (x_prompt.reshape(tp, d), x_sample.reshape(ts, d))
    x_bf = jnp.concatenate([x[0].astype(BF16), x[1].astype(BF16)], axis=0)
    proj_tile = 4 * LANES
    proj_main = off[-1] // proj_tile * proj_tile
    w_in_t = w_in.transpose(0, 2, 1)
    wa2 = jnp.pad(gla_w_a2, ((0, 0), (0, LANES - GLA_RANK), (0, 0)))
    u_bf = peer_u.astype(BF16)
    vt_bf = peer_v.astype(BF16).transpose(0, 2, 1)
    cols = (off[0], off[1], off[2], off[3], lr0)
    u_col, zg_col, zs_col = off[5], off[6], off[7]
    gla_p = gla_s = None
    re_p, im_p, re_s, im_s = [], [], [], []
    for l in range(depth):
        proj = _mm(x_bf, w_in_t, l, 1024, proj_tile, w_transposed=True, n_out=proj_main + LANES)
        proj = _mm_tail(x_bf, w_in_t, l, proj, 1024, proj_main)

        o_gla, gla_p = _gla(proj, wa2[l], gla_b_a[l], gla_norm_g[l], None, l, gla_p, None, depth=depth, row0=0,
                            n_rows=t_all, nb=pb, seq_len=plen, c=min(GLA_CHUNK, plen), nseq=1, cols=cols)
        o_gla, gla_s = _gla(proj, wa2[l], gla_b_a[l], gla_norm_g[l], state_gla, l, gla_s, o_gla, depth=depth,
                            row0=tp, n_rows=t_all, nb=sb, seq_len=slen, c=min(GLA_CHUNK, slen), nseq=2, cols=cols)

        tables = _s5_tables(s5_a_re[l], s5_a_im[l], s5_log_dt[l], s5_b_re[l], s5_b_im[l], s5_c_re[l], s5_c_im[l],
                            S5_CHUNK)
        y_s5, (r_p, i_p), (r_s, i_s) = _s5_mixer(proj, u_col, tables, s5_d[l], state_s5_re[l], state_s5_im[l],
                                                 pb, plen)
        o_s5 = _glu(y_s5, s5_w_glu, l, s5_b_glu[l], 1024, 1024)

        merged = _branch_merge(o_gla, o_s5, w_branch, l, proj, zg_col, zs_col, 1024, 512)
        mix = _mm(merged, w_out, l, 1024, proj_tile)
        x, x_bf, xt_bf = _deepnorm_ln(x, mix, ln1_g[l], ln1_b[l], alpha, 256, emit_transposed=True)

        q = _mm(x_bf, peer_w_q, l, 1024, proj_tile)
        route = _peer_route(q, peer_key1[l], peer_key2[l], 256)
        ffn_t = _peer_ffn(xt_bf, u_bf, vt_bf, l, route, 512, 512)
        if l + 1 < depth:
            x, x_bf = _deepnorm_ln(x, ffn_t, ln2_g[l], ln2_b[l], alpha, 256, y_transposed=True)
        else:
            y_p, y_s = _deepnorm_ln(x, ffn_t, ln2_g[l], ln2_b[l], alpha, 256, y_transposed=True,
                                    emit_bf16=False, split_out=tp)

        re_p.append(r_p)
        im_p.append(i_p)
        re_s.append(r_s)
        im_s.append(i_s)

    return (y_p.reshape(pb, plen, d), y_s.reshape(sb, slen, d), gla_p, jnp.stack(re_p), jnp.stack(im_p),
            gla_s, jnp.stack(re_s), jnp.stack(im_s))
```

```python
import functools

import jax
import jax.numpy as jnp
from jax import lax
from jax.experimental import pallas as pl
from jax.experimental.pallas import tpu as pltpu

F32 = jnp.float32
BF16 = jnp.bfloat16
HIGHEST = lax.Precision.HIGHEST

V7X_VMEM_BYTES = 64 * 1024 * 1024
VMEM_LIMIT_BYTES = V7X_VMEM_BYTES - 8 * 1024 * 1024
LANES = 128

GLA_HEADS = 4
GLA_RANK = 16
GLA_TAU = 16.0
GLA_CHUNK = 64
RMS_EPS = 1e-6
S5_CHUNK = 8
PEER_HEADS = 8
PEER_TOPK = 16
LN_EPS = 1e-5


def _params(*semantics):
    return pltpu.CompilerParams(dimension_semantics=semantics, vmem_limit_bytes=VMEM_LIMIT_BYTES)


def _dot(a, b):
    return jnp.dot(a, b, preferred_element_type=F32)


def _dot_nt(a, b):
    return lax.dot_general(a, b, (((1,), (1,)), ((), ())), preferred_element_type=F32)


def _mm_kernel(a_ref, b_ref, o_ref):
    o_ref[...] = _dot(a_ref[...], b_ref[...].astype(BF16)).astype(o_ref.dtype)


def _layer_cols(layer, tn, row0=0):
    def spec(rows):
        return pl.BlockSpec((None, rows, tn), lambda i, j: (layer, row0, j))
    return spec


def _mm_nt_kernel(a_ref, bt_ref, o_ref):
    o_ref[...] = _dot_nt(a_ref[...], bt_ref[...].astype(BF16)).astype(o_ref.dtype)


def _mm(a, w, layer, tm, tn, out_dtype=F32, w_transposed=False, n_out=None):
    m, k = a.shape
    n = (w.shape[1] if w_transposed else w.shape[2]) // tn * tn
    w_spec = (pl.BlockSpec((None, tn, k), lambda i, j: (layer, j, 0)) if w_transposed
              else _layer_cols(layer, tn)(k))
    return pl.pallas_call(
        _mm_nt_kernel if w_transposed else _mm_kernel,
        grid=(m // tm, n // tn),
        in_specs=[pl.BlockSpec((tm, k), lambda i, j: (i, 0)), w_spec],
        out_specs=pl.BlockSpec((tm, tn), lambda i, j: (i, j)),
        out_shape=jax.ShapeDtypeStruct((m, n_out or n), out_dtype),
        compiler_params=_params("parallel", "parallel"),
        name="mm",
    )(a, w)


def _mm_tail_kernel(a_ref, bt_ref, prev_ref, o_ref):
    del prev_ref
    tail = _dot_nt(a_ref[...], bt_ref[...].astype(BF16))
    o_ref[...] = jnp.concatenate([tail, jnp.zeros((tail.shape[0], o_ref.shape[1] - tail.shape[1]), F32)], axis=1)


def _mm_tail(a, w_t, layer, out, tm, col0):
    m, k = a.shape
    rows = w_t.shape[1] - col0
    assert 0 < rows < LANES and rows % 8 == 0 and col0 % rows == 0 and col0 % LANES == 0
    return pl.pallas_call(
        _mm_tail_kernel,
        grid=(m // tm,),
        in_specs=[pl.BlockSpec((tm, k), lambda i: (i, 0)),
                  pl.BlockSpec((None, rows, k), lambda i: (layer, col0 // rows, 0)),
                  pl.BlockSpec(memory_space=pl.ANY)],
        out_specs=pl.BlockSpec((tm, LANES), lambda i: (i, col0 // LANES)),
        out_shape=jax.ShapeDtypeStruct(out.shape, out.dtype),
        input_output_aliases={2: 0},
        compiler_params=_params("parallel"),
        name="mm_tail",
    )(a, w_t, out)


def _glu_kernel(y_ref, w_ref, b_ref, ytile_ref, o_ref):
    acc = _dot(y_ref[...].astype(BF16), w_ref[...].astype(BF16)) + b_ref[...]
    o_ref[...] = (ytile_ref[...] * jax.nn.sigmoid(acc)).astype(o_ref.dtype)


def _glu(y, w, layer, b, tm, tn):
    m, k = y.shape
    n = w.shape[2]
    return pl.pallas_call(
        _glu_kernel,
        grid=(m // tm, n // tn),
        in_specs=[pl.BlockSpec((tm, k), lambda i, j: (i, 0)),
                  _layer_cols(layer, tn)(k),
                  pl.BlockSpec((1, tn), lambda i, j: (0, j)),
                  pl.BlockSpec((tm, tn), lambda i, j: (i, j))],
        out_specs=pl.BlockSpec((tm, tn), lambda i, j: (i, j)),
        out_shape=jax.ShapeDtypeStruct((m, n), BF16),
        compiler_params=_params("parallel", "parallel"),
        name="s5_glu",
    )(y, w, b.reshape(1, n), y)


def _lane_shifted(a, b, shift):
    if shift == 0:
        return a
    return jnp.concatenate([a[:, shift:], b[:, :shift]], axis=1)


def _branch_kernel(og_ref, os_ref, w1_ref, w2_ref, zg_ref, zg_next_ref, zs_ref, zs_next_ref, o_ref, *, shift):
    m1 = _dot(og_ref[...], w1_ref[...].astype(BF16))
    m2 = _dot(os_ref[...], w2_ref[...].astype(BF16))
    zg = _lane_shifted(zg_ref[...], zg_next_ref[...], shift)
    zs = _lane_shifted(zs_ref[...], zs_next_ref[...], shift)
    o_ref[...] = (jax.nn.sigmoid(zg) * m1 + jax.nn.sigmoid(zs) * m2).astype(o_ref.dtype)


def _branch_merge(o_gla, o_s5, w, layer, proj, zg_col, zs_col, tm, tn):
    m, k = o_gla.shape
    assert o_s5.shape[1] == k and w.shape[1] == 2 * k
    n = w.shape[2]
    shift = zg_col % LANES
    assert zs_col % LANES == shift and (zg_col - shift) % tn == 0 and (zs_col - shift) % tn == 0

    def gate_specs(col):
        blk, nxt, per = (col - shift) // tn, (col - shift) // LANES, tn // LANES
        return [pl.BlockSpec((tm, tn), lambda i, j: (i, blk + j)),
                pl.BlockSpec((tm, LANES), lambda i, j: (i, nxt + per * (j + 1)))]

    return pl.pallas_call(
        functools.partial(_branch_kernel, shift=shift),
        grid=(m // tm, n // tn),
        in_specs=[pl.BlockSpec((tm, k), lambda i, j: (i, 0)),
                  pl.BlockSpec((tm, k), lambda i, j: (i, 0)),
                  _layer_cols(layer, tn, 0)(k),
                  _layer_cols(layer, tn, 1)(k)] + gate_specs(zg_col) + gate_specs(zs_col),
        out_specs=pl.BlockSpec((tm, tn), lambda i, j: (i, j)),
        out_shape=jax.ShapeDtypeStruct((m, n), BF16),
        compiler_params=_params("parallel", "parallel"),
        name="branch_merge",
    )(o_gla, o_s5, w, w, proj, proj, proj, proj)


def _ln_kernel(*refs, alpha, y_transposed, n_head, two_in, two_out, emit_bf16, emit_transposed):
    refs = list(refs)
    step = pl.program_id(0)
    if two_in:
        x = jnp.where(step < n_head, refs[0][...], refs[1][...])
        refs = refs[2:]
    else:
        x = refs[0][...]
        refs = refs[1:]
    y_ref, g_ref, b_ref = refs[:3]
    outs = refs[3:]
    y = y_ref[...].T if y_transposed else y_ref[...]
    h = alpha * x + y
    mu = jnp.mean(h, axis=-1, keepdims=True)
    d = h - mu
    var = jnp.mean(d * d, axis=-1, keepdims=True)
    o = d * lax.rsqrt(var + LN_EPS) * g_ref[...] + b_ref[...]
    if two_out:
        head_ref, tail_ref = outs[:2]
        outs = outs[2:]

        @pl.when(step < n_head)
        def _():
            head_ref[...] = o

        @pl.when(step >= n_head)
        def _():
            tail_ref[...] = o
    else:
        outs[0][...] = o
        outs = outs[1:]
    if emit_bf16:
        outs[0][...] = o.astype(BF16)
        outs = outs[1:]
    if emit_transposed:
        outs[0][...] = o.T.astype(BF16)


def _deepnorm_ln(x, y, g, b, alpha, tm, y_transposed=False, emit_bf16=True, emit_transposed=False, split_out=None):
    xs = x if isinstance(x, (tuple, list)) else (x,)
    d = xs[0].shape[1]
    m = sum(a.shape[0] for a in xs)
    two_in, two_out = len(xs) == 2, split_out is not None
    n_head = (xs[0].shape[0] if two_in else split_out if two_out else m) // tm
    assert not (two_in and two_out) or xs[0].shape[0] == split_out
    row = pl.BlockSpec((tm, d), lambda i: (i, 0))
    head = pl.BlockSpec((tm, d), lambda i: (jnp.minimum(i, n_head - 1), 0))
    tail = pl.BlockSpec((tm, d), lambda i: (jnp.maximum(i - n_head, 0), 0))
    col = pl.BlockSpec((d, tm), lambda i: (0, i))
    vec = pl.BlockSpec((1, d), lambda i: (0, 0))
    out_specs = [head, tail] if two_out else [row]
    out_shape = ([jax.ShapeDtypeStruct((split_out, d), F32), jax.ShapeDtypeStruct((m - split_out, d), F32)]
                 if two_out else [jax.ShapeDtypeStruct((m, d), F32)])
    if emit_bf16:
        out_specs.append(row)
        out_shape.append(jax.ShapeDtypeStruct((m, d), BF16))
    if emit_transposed:
        out_specs.append(col)
        out_shape.append(jax.ShapeDtypeStruct((d, m), BF16))
    kern = functools.partial(_ln_kernel, alpha=alpha, y_transposed=y_transposed, n_head=n_head, two_in=two_in,
                             two_out=two_out, emit_bf16=emit_bf16, emit_transposed=emit_transposed)
    return pl.pallas_call(
        kern,
        grid=(m // tm,),
        in_specs=([head, tail] if two_in else [row]) + [col if y_transposed else row, vec, vec],
        out_specs=out_specs,
        out_shape=out_shape,
        compiler_params=_params("arbitrary"),
        name="deepnorm_ln",
    )(*xs, y, g.reshape(1, d), b.reshape(1, d))


def _gla_kernel(*refs, c, nseq, heads, scale, has_init, n_prev):
    refs = list(refs)
    q_ref, k_ref, v_ref, r_ref, alr_ref, wa2_ref, ba_ref, g_ref = refs[:8]
    rest = refs[8:]
    s0_ref = rest.pop(0) if has_init else None
    o_ref, s_ref = rest[n_prev:]
    dk = q_ref.shape[-1] // heads
    dv = v_ref.shape[-1] // heads
    kpad = max(c, LANES)
    qpad = max(c, 16)

    @pl.when(pl.program_id(1) == 0)
    def _():
        if has_init:
            s_ref[...] = s0_ref[...]
        else:
            s_ref[...] = jnp.zeros_like(s_ref)

    row = lax.broadcasted_iota(jnp.int32, (c, c), 0)
    col = lax.broadcasted_iota(jnp.int32, (c, c), 1)
    tri_cc = (row >= col).astype(F32)
    rowp = lax.broadcasted_iota(jnp.int32, (qpad, kpad), 0)
    colp = lax.broadcasted_iota(jnp.int32, (qpad, kpad), 1)
    causal = rowp >= colp
    eye = lax.broadcasted_iota(jnp.int32, (dk, dk), 0) == lax.broadcasted_iota(jnp.int32, (dk, dk), 1)
    eye_bf = eye.astype(BF16)

    def pad_rows(x, rows):
        if x.shape[0] == rows:
            return x
        return jnp.concatenate([x, jnp.zeros((rows - x.shape[0], x.shape[1]), x.dtype)], axis=0)

    outs = []
    for s in range(nseq):
        rows = slice(s * c, (s + 1) * c)
        z = jnp.dot(alr_ref[rows, :], wa2_ref[...], precision=HIGHEST, preferred_element_type=F32) + ba_ref[...]
        log_a = (jnp.minimum(z, 0.0) - jnp.log1p(jnp.exp(-jnp.abs(z)))) * (1.0 / GLA_TAU)
        if c >= LANES // 2:
            b = jnp.dot(tri_cc, log_a, precision=HIGHEST, preferred_element_type=F32)
        else:
            ridx = lax.broadcasted_iota(jnp.int32, log_a.shape, 0)
            b = jnp.zeros_like(log_a)
            for t in range(c):
                b = b + jnp.where(ridx >= t, log_a[t:t + 1, :], 0.0)
        b_last = b[c - 1:c, :]
        k = k_ref[rows, :]
        q_dec_all = (q_ref[rows, :] * scale * jnp.exp(b)).astype(BF16)
        k_inv_all = (k * jnp.exp(-b)).astype(BF16)
        k_end_all = (k * jnp.exp(b_last - b)).astype(BF16)
        v_all = v_ref[rows, :].astype(BF16)
        r = r_ref[rows, :]
        gate = r * jax.nn.sigmoid(r)
        o_heads = []
        for h in range(heads):
            kc = slice(h * dk, (h + 1) * dk)
            vc = slice(h * dv, (h + 1) * dv)
            q_dec = pad_rows(q_dec_all[:, kc], qpad)
            k_inv = pad_rows(k_inv_all[:, kc], kpad)
            k_end = pad_rows(k_end_all[:, kc], kpad)
            v_bf = pad_rows(v_all[:, vc], kpad)
            scores = jnp.where(causal, _dot_nt(q_dec, k_inv), 0.0).astype(BF16)
            state = s_ref[s, h]
            o = (_dot(q_dec, state.astype(BF16)) + _dot(scores, v_bf))[:c]
            k_end_t = _dot_nt(eye_bf, k_end).astype(BF16)
            decay_col = jnp.exp(jnp.sum(jnp.where(eye, b_last[:, kc], 0.0), axis=1, keepdims=True))
            s_ref[s, h] = decay_col * state + _dot(k_end_t, v_bf)
            o = o * lax.rsqrt(jnp.mean(o * o, axis=-1, keepdims=True) + RMS_EPS) * g_ref[...]
            o_heads.append(o * gate[:, vc])
        outs.append(jnp.concatenate(o_heads, axis=1))
    o_all = outs[0] if nseq == 1 else jnp.concatenate(outs, axis=0)
    o_ref[...] = o_all.astype(o_ref.dtype)


def _gla(proj, wa2, ba, norm_g, s0, layer, s_prev, o_prev, *, depth, row0, n_rows, nb, seq_len, c, nseq, cols):
    heads = GLA_HEADS
    dk_tot = wa2.shape[1]
    dk = dk_tot // heads
    dv = norm_g.shape[0]
    dv_tot = heads * dv
    nchunk = seq_len // c
    assert nseq == 1 or nchunk == 1
    rb = nseq * c
    blk0 = row0 // rb
    q_col, k_col, v_col, r_col, lr_col = cols

    def rows_map(col_blk):
        return lambda b, j: (blk0 + b * nchunk + j, col_blk)

    in_specs = [
        pl.BlockSpec((rb, dk_tot), rows_map(q_col // dk_tot)),
        pl.BlockSpec((rb, dk_tot), rows_map(k_col // dk_tot)),
        pl.BlockSpec((rb, dv_tot), rows_map(v_col // dv_tot)),
        pl.BlockSpec((rb, dv_tot), rows_map(r_col // dv_tot)),
        pl.BlockSpec((rb, LANES), rows_map(lr_col // LANES)),
        pl.BlockSpec((LANES, dk_tot), lambda b, j: (0, 0)),
        pl.BlockSpec((1, dk_tot), lambda b, j: (0, 0)),
        pl.BlockSpec((1, dv), lambda b, j: (0, 0)),
    ]
    args = [proj, proj, proj, proj, proj, wa2, ba.reshape(1, -1), norm_g.reshape(1, dv)]
    state_spec = pl.BlockSpec((None, nseq, heads, dk, dv), lambda b, j: (layer, b, 0, 0, 0))
    if s0 is not None:
        in_specs.append(state_spec)
        args.append(s0)
    aliases = {}
    for prev, out_idx in ((s_prev, 1), (o_prev, 0)):
        if prev is not None:
            aliases[len(args)] = out_idx
            in_specs.append(pl.BlockSpec(memory_space=pl.ANY))
            args.append(prev)
    kern = functools.partial(_gla_kernel, c=c, nseq=nseq, heads=heads, scale=dk ** -0.5,
                             has_init=s0 is not None, n_prev=len(aliases))
    return pl.pallas_call(
        kern,
        grid=(nb // nseq, nchunk),
        in_specs=in_specs,
        out_specs=[pl.BlockSpec((rb, dv_tot), rows_map(0)), state_spec],
        out_shape=[jax.ShapeDtypeStruct((n_rows, dv_tot), BF16),
                   jax.ShapeDtypeStruct((depth, nb, heads, dk, dv), F32)],
        input_output_aliases=aliases,
        compiler_params=_params("parallel", "arbitrary"),
        name="gla",
    )(*args)


def _s5_tables(a_re, a_im, log_dt, b_re, b_im, c_re, c_im, c):
    dt = jnp.exp(log_dt)[:, None]
    mag = jnp.exp(a_re * dt)
    ar, ai = mag * jnp.cos(a_im * dt), mag * jnp.sin(a_im * dt)
    den = a_re * a_re + a_im * a_im
    cr = ((ar - 1.0) * a_re + ai * a_im) / den
    ci = (ai * a_re - (ar - 1.0) * a_im) / den
    bbr = cr[..., None] * b_re - ci[..., None] * b_im
    bbi = cr[..., None] * b_im + ci[..., None] * b_re
    tau = jnp.arange(c + 1, dtype=F32)[:, None, None]
    pmag = jnp.exp(tau * (a_re * dt))
    pw_r, pw_i = pmag * jnp.cos(tau * (a_im * dt)), pmag * jnp.sin(tau * (a_im * dt))
    ca_r = c_re[None] * pw_r[:, :, None, :] - c_im[None] * pw_i[:, :, None, :]
    ca_i = c_re[None] * pw_i[:, :, None, :] + c_im[None] * pw_r[:, :, None, :]
    kern = (jnp.einsum('tgmp,gpn->tgmn', ca_r[:c], bbr, precision=HIGHEST)
            - jnp.einsum('tgmp,gpn->tgmn', ca_i[:c], bbi, precision=HIGHEST))
    g, m, n = kern.shape[1:]
    p = a_re.shape[1]
    gpb = LANES // n
    nb = g // gpb

    def block_diag(x):
        lead, _, r, w = x.shape
        copies = jnp.tile(jnp.eye(w, dtype=F32), (1, gpb))
        wide = jnp.dot(x.reshape(-1, w), copies, precision=HIGHEST).reshape(lead, nb, gpb * r, gpb * w)
        return wide * jnp.kron(jnp.eye(gpb, dtype=F32), jnp.ones((r, w), F32))

    lag_blocks = block_diag(kern.transpose(0, 1, 3, 2))
    zero_block = jnp.zeros_like(lag_blocks[0])
    toep = jnp.concatenate(
        [jnp.concatenate([lag_blocks[t - s] if t >= s else zero_block for s in range(c)], axis=1)
         for t in range(c)], axis=2)
    back = (c - 1.0) - tau[:c]
    rev_mag = jnp.exp(back * (a_re * dt))
    rev_r, rev_i = rev_mag * jnp.cos(back * (a_im * dt)), rev_mag * jnp.sin(back * (a_im * dt))
    inj_r = rev_r[..., None] * bbr[None] - rev_i[..., None] * bbi[None]
    inj_i = rev_r[..., None] * bbi[None] + rev_i[..., None] * bbr[None]
    inj = jnp.concatenate([block_diag(x.transpose(0, 1, 3, 2)) for x in (inj_r, inj_i)], axis=3)
    inj = inj.transpose(1, 0, 2, 3).reshape(nb, c * LANES, 2 * gpb * p)
    read = jnp.concatenate(
        [jnp.concatenate(list(block_diag(x.transpose(0, 1, 3, 2))), axis=2)
         for x in (ca_r[1:], -ca_i[1:])], axis=1)
    step_r, step_i = pw_r[c].reshape(nb, 1, gpb * p), pw_i[c].reshape(nb, 1, gpb * p)
    step_same = jnp.concatenate([step_r, step_r], axis=-1)
    step_swap = jnp.concatenate([-step_i, step_i], axis=-1)
    return toep.astype(BF16), inj.astype(BF16), read.astype(BF16), step_same, step_swap


def _chunk_tokens(u_ref, u_next_ref, c, shift):
    r = u_ref.shape[0] // c
    return [_lane_shifted(u_ref[pl.ds(s, r, stride=c), :], u_next_ref[pl.ds(s, r, stride=c), :], shift)
            for s in range(c)]


def _chunk_rows(tokens):
    return jnp.concatenate([x.astype(BF16) for x in tokens], axis=1)


def _u_specs(t_tokens, u_col):
    blk = u_col // LANES
    return [pl.BlockSpec((t_tokens, LANES), lambda b: (0, blk + b)),
            pl.BlockSpec((t_tokens, LANES), lambda b: (0, blk + b + 1))]


def _s5_inject_kernel(u_ref, u_next_ref, w_ref, e_ref, *, c, shift):
    e_ref[...] = _dot(_chunk_rows(_chunk_tokens(u_ref, u_next_ref, c, shift)), w_ref[...])


def _s5_inject(proj, u_col, inj, c):
    t_tokens = proj.shape[0]
    nb, k, w = inj.shape
    return pl.pallas_call(
        functools.partial(_s5_inject_kernel, c=c, shift=u_col % LANES),
        grid=(nb,),
        in_specs=_u_specs(t_tokens, u_col) + [pl.BlockSpec((None, k, w), lambda b: (b, 0, 0))],
        out_specs=pl.BlockSpec((None, t_tokens // c, w), lambda b: (b, 0, 0)),
        out_shape=jax.ShapeDtypeStruct((nb, t_tokens // c, w), F32),
        compiler_params=_params("parallel"),
        name="s5_inject",
    )(proj, proj, inj)


def _s5_scan_kernel(e_ref, x0_ref, same_ref, swap_ref, xprev_ref, xfin_p_ref, xfin_s_ref, *, n_seq, n_chunk):
    half = e_ref.shape[-1] // 2
    same = same_ref[...]
    swap = swap_ref[...]

    def advance(x, e):
        x_swapped = jnp.concatenate([x[:, half:], x[:, :half]], axis=1)
        return same * x + swap * x_swapped + e

    def body(j, xs):
        new = []
        for q, x in enumerate(xs):
            row = q * n_chunk + j
            xprev_ref[pl.ds(row, 1), :] = x
            new.append(advance(x, e_ref[pl.ds(row, 1), :]))
        return tuple(new)

    zero = jnp.zeros((1, e_ref.shape[-1]), F32)
    xs = lax.fori_loop(0, n_chunk, body, (zero,) * n_seq)
    xfin_p_ref[...] = jnp.zeros_like(xfin_p_ref)
    for q, x in enumerate(xs):
        xfin_p_ref[q:q + 1, :] = x
    rp = n_seq * n_chunk
    x0 = x0_ref[...]
    xprev_ref[rp:, :] = x0
    xfin_s_ref[...] = advance(x0, e_ref[rp:, :])


def _s5_scan(e, x0_s, same, swap, n_seq, n_chunk):
    nb, r, w = e.shape
    ns = x0_s.shape[1]
    assert r == n_seq * n_chunk + ns
    pad_seq = -(-n_seq // 8) * 8
    vec = pl.BlockSpec((None, 1, w), lambda b: (b, 0, 0))
    return pl.pallas_call(
        functools.partial(_s5_scan_kernel, n_seq=n_seq, n_chunk=n_chunk),
        grid=(nb,),
        in_specs=[pl.BlockSpec((None, r, w), lambda b: (b, 0, 0)),
                  pl.BlockSpec((None, ns, w), lambda b: (b, 0, 0)), vec, vec],
        out_specs=[pl.BlockSpec((None, r, w), lambda b: (b, 0, 0)),
                   pl.BlockSpec((None, pad_seq, w), lambda b: (b, 0, 0)),
                   pl.BlockSpec((None, ns, w), lambda b: (b, 0, 0))],
        out_shape=[jax.ShapeDtypeStruct((nb, r, w), F32), jax.ShapeDtypeStruct((nb, pad_seq, w), F32),
                   jax.ShapeDtypeStruct((nb, ns, w), F32)],
        compiler_params=_params("parallel"),
        name="s5_scan",
    )(e, x0_s, same, swap)


def _s5_out_kernel(u_ref, u_next_ref, xp_ref, toep_ref, read_ref, d_ref, y_ref, *, c, shift):
    r = u_ref.shape[0] // c
    tokens = _chunk_tokens(u_ref, u_next_ref, c, shift)
    y = _dot(_chunk_rows(tokens), toep_ref[...]) + _dot(xp_ref[...].astype(BF16), read_ref[...])
    for t in range(c):
        y_ref[pl.ds(t, r, stride=c), :] = jax.nn.gelu(y[:, t * LANES:(t + 1) * LANES] + d_ref[...] * tokens[t])


def _s5_out(proj, u_col, xprev, toep, read, d_skip, c):
    t_tokens = proj.shape[0]
    nb, r, w = xprev.shape
    k = toep.shape[1]
    return pl.pallas_call(
        functools.partial(_s5_out_kernel, c=c, shift=u_col % LANES),
        grid=(nb,),
        in_specs=_u_specs(t_tokens, u_col) + [
            pl.BlockSpec((None, r, w), lambda b: (b, 0, 0)),
            pl.BlockSpec((None, k, k), lambda b: (b, 0, 0)),
            pl.BlockSpec((None, w, k), lambda b: (b, 0, 0)),
            pl.BlockSpec((1, LANES), lambda b: (0, b))],
        out_specs=pl.BlockSpec((t_tokens, LANES), lambda b: (0, b)),
        out_shape=jax.ShapeDtypeStruct((t_tokens, nb * LANES), F32),
        compiler_params=_params("parallel"),
        name="s5_out",
    )(proj, proj, xprev, toep, read, d_skip.reshape(1, -1))


def _s5_mixer(proj, u_col, tables, d_skip, st_re, st_im, n_prompt_seq, prompt_len):
    toep, inj, read, step_same, step_swap = tables
    c = S5_CHUNK
    nb = toep.shape[0]
    ns, groups, p = st_re.shape
    gpb = groups // nb

    def to_blocks(st):
        return st.reshape(ns, nb, gpb * p).transpose(1, 0, 2)

    def from_blocks(x):
        return x.transpose(1, 0, 2).reshape(x.shape[1], groups, p)

    e = _s5_inject(proj, u_col, inj, c)
    x0_s = jnp.concatenate([to_blocks(st_re), to_blocks(st_im)], axis=-1)
    xprev, xfin_p, xfin_s = _s5_scan(e, x0_s, step_same, step_swap, n_prompt_seq, prompt_len // c)
    y = _s5_out(proj, u_col, xprev, toep, read, d_skip, c)
    half = gpb * p
    xfin_p = xfin_p[:, :n_prompt_seq]
    return (y, (from_blocks(xfin_p[..., :half]), from_blocks(xfin_p[..., half:])),
            (from_blocks(xfin_s[..., :half]), from_blocks(xfin_s[..., half:])))


def _sorting_network(n):
    pairs = []
    p = 1
    while p < n:
        k = p
        while k >= 1:
            for j in range(k % p, n - k, 2 * k):
                for i in range(min(k, n - j - k)):
                    if (i + j) // (2 * p) == (i + j + k) // (2 * p):
                        pairs.append((i + j, i + j + k))
            k //= 2
        p *= 2
    return pairs


def _peer_route_kernel(q_ref, k1_ref, k2_ref, s1_ref, w1_ref, s2_ref, e2_ref, tau_ref):
    half = k1_ref.shape[1]
    neg = jnp.float32(-jnp.inf)

    def top_values(x):
        slabs = [x[r:r + 8] for r in range(0, x.shape[0], 8)]
        for lo, hi in _sorting_network(len(slabs)):
            slabs[lo], slabs[hi] = jnp.maximum(slabs[lo], slabs[hi]), jnp.minimum(slabs[lo], slabs[hi])
        stack_id = lax.broadcasted_iota(jnp.int32, slabs[0].shape, 0)
        vals = []
        for taken in range(PEER_TOPK):
            m = jnp.max(slabs[0], axis=0, keepdims=True)
            vals.append(m)
            depth = min(len(slabs), PEER_TOPK - taken - 1)
            first = jnp.min(jnp.where(slabs[0] == m, stack_id, 8), axis=0, keepdims=True)
            popped = stack_id == first
            slabs = [jnp.where(popped, slabs[k + 1] if k + 1 < len(slabs) else neg, slabs[k])
                     for k in range(depth)]
        return vals

    for h in range(PEER_HEADS):
        q1 = q_ref[:, (2 * h) * half:(2 * h + 1) * half]
        q2 = q_ref[:, (2 * h + 1) * half:(2 * h + 2) * half]
        s1 = lax.dot_general(k1_ref[...], q1, (((1,), (1,)), ((), ())), precision=HIGHEST,
                             preferred_element_type=F32)
        s2 = lax.dot_general(k2_ref[...], q2, (((1,), (1,)), ((), ())), precision=HIGHEST,
                             preferred_element_type=F32)
        a = top_values(s1)
        b = jnp.concatenate(top_values(s2), axis=0)
        b0 = b[0:1]
        cand = [a[i] + b[:PEER_TOPK // (i + 1)] for i in range(PEER_TOPK)]
        n_cand = sum(x.shape[0] for x in cand)
        n_rows = 8 * pl.next_power_of_2(-(-n_cand // 8))
        cand.append(jnp.full((n_rows - n_cand, s1.shape[1]), neg, F32))
        top = top_values(jnp.concatenate(cand, axis=0))
        zsum = top[0] * 0.0
        for m in top:
            zsum = zsum + jnp.exp(m - top[0])
        s1_ref[h] = s1
        s2_ref[h] = s2
        w1_ref[h] = jnp.exp(s1 - a[0]) / zsum
        e2_ref[h] = jnp.exp(s2 - b0)
        tau_ref[h] = top[-1]


def _peer_route(q, key1, key2, tm):
    t_tokens = q.shape[0]
    nk, half = key1.shape
    big = pl.BlockSpec((PEER_HEADS, nk, tm), lambda i: (0, 0, i))
    big_shape = jax.ShapeDtypeStruct((PEER_HEADS, nk, t_tokens), F32)
    return pl.pallas_call(
        _peer_route_kernel,
        grid=(t_tokens // tm,),
        in_specs=[pl.BlockSpec((tm, q.shape[1]), lambda i: (i, 0)),
                  pl.BlockSpec((nk, half), lambda i: (0, 0)),
                  pl.BlockSpec((nk, half), lambda i: (0, 0))],
        out_specs=[big, big, big, big, pl.BlockSpec((PEER_HEADS, 1, tm), lambda i: (0, 0, i))],
        out_shape=[big_shape, big_shape, big_shape, big_shape,
                   jax.ShapeDtypeStruct((PEER_HEADS, 1, t_tokens), F32)],
        compiler_params=_params("parallel"),
        name="peer_route",
    )(q, key1, key2)


def _peer_ffn_kernel(xt_ref, u_ref, vt_ref, s1_ref, w1_ref, s2_ref, e2_ref, tau_ref, o_ref, p_ref, h_ref, g_ref,
                     *, ni, n_et):
    step = pl.program_id(1)
    et = jnp.minimum(step, n_et - 1)
    slot = step % 2

    @pl.when(step == 0)
    def _():
        o_ref[...] = jnp.zeros_like(o_ref)
        p_ref[...] = jnp.zeros_like(p_ref)

    nk = s2_ref.shape[1]
    te, tm = h_ref.shape
    d = o_ref.shape[0]
    chunks = [(ii, tc) for ii in range(ni) for tc in range(tm // LANES)]

    def block(ii, tc):
        return slice(ii * nk, (ii + 1) * nk), slice(tc * LANES, (tc + 1) * LANES)

    def gate_chunk(ii, tc):
        rows, lanes = block(ii, tc)
        i = et * ni + ii
        g = None
        for h in range(PEER_HEADS):
            s1_row = s1_ref[h, pl.ds(i, 1), :][:, lanes]
            w1_row = w1_ref[h, pl.ds(i, 1), :][:, lanes]
            hit = (s1_row + s2_ref[h, :, lanes]) >= tau_ref[h, :, lanes]
            term = jnp.where(hit, e2_ref[h, :, lanes] * w1_row, 0.0)
            g = term if g is None else g + term
        g_ref[rows, lanes] = g

    def finish_chunk(ii, tc):
        rows, lanes = block(ii, tc)
        p_ref[slot, rows, lanes] = (g_ref[rows, lanes] * jax.nn.gelu(h_ref[rows, lanes])).astype(BF16)

    quarter = len(chunks) // 4
    half = tm // 2
    for c in chunks[:quarter]:
        gate_chunk(*c)
    h_ref[:, :half] = _dot(u_ref[...], xt_ref[:, :half])
    for c in chunks[quarter:2 * quarter]:
        gate_chunk(*c)
    h_ref[:, half:] = _dot(u_ref[...], xt_ref[:, half:])
    p_prev = p_ref[1 - slot]
    vector_work = [[(gate_chunk, c) for c in chunks[2 * quarter:3 * quarter]],
                   [(gate_chunk, c) for c in chunks[3 * quarter:]],
                   [(finish_chunk, c) for c in chunks[:2 * quarter]],
                   [(finish_chunk, c) for c in chunks[2 * quarter:]]]
    dq = d // len(vector_work)
    for k, work in enumerate(vector_work):
        for fn, c in work:
            fn(*c)
        rows = slice(k * dq, (k + 1) * dq)
        o_ref[rows, :] += _dot(vt_ref[rows, :], p_prev)


def _peer_ffn(xt_bf, u_tab, vt_tab, layer, route, tm, te):
    s1, w1, s2, e2, tau = route
    d, t_tokens = xt_bf.shape
    n_exp = u_tab.shape[1]
    nk = s2.shape[1]
    ni = te // nk
    n_et = n_exp // te
    once = pl.Buffered(1)
    tok = pl.BlockSpec((PEER_HEADS, nk, tm), lambda i, j: (0, 0, i), pipeline_mode=once)
    return pl.pallas_call(
        functools.partial(_peer_ffn_kernel, ni=ni, n_et=n_et),
        grid=(t_tokens // tm, n_et + 1),
        in_specs=[pl.BlockSpec((d, tm), lambda i, j: (0, i), pipeline_mode=once),
                  pl.BlockSpec((None, te, d), lambda i, j: (layer, jnp.minimum(j, n_et - 1), 0)),
                  pl.BlockSpec((None, d, te), lambda i, j: (layer, 0, jnp.maximum(j - 1, 0))),
                  tok, tok, tok, tok,
                  pl.BlockSpec((PEER_HEADS, 1, tm), lambda i, j: (0, 0, i), pipeline_mode=once)],
        out_specs=pl.BlockSpec((d, tm), lambda i, j: (0, i)),
        out_shape=jax.ShapeDtypeStruct((d, t_tokens), F32),
        scratch_shapes=[pltpu.VMEM((2, te, tm), BF16), pltpu.VMEM((te, tm), F32), pltpu.VMEM((te, tm), F32)],
        compiler_params=_params("parallel", "arbitrary"),
        name="peer_ffn",
    )(xt_bf, u_tab, vt_tab, s1, w1, s2, e2, tau)


def kernel(x_prompt, x_sample, state_gla, state_s5_re, state_s5_im, ln1_g, ln1_b, w_in, gla_w_a2, gla_b_a, gla_norm_g, s5_a_re, s5_a_im, s5_log_dt, s5_b_re, s5_b_im, s5_c_re, s5_c_im, s5_d, s5_w_glu, s5_b_glu, w_branch, w_out, ln2_g, ln2_b, peer_w_q, peer_key1, peer_key2, peer_u, peer_v):
    depth = w_in.shape[0]
    pb, plen, d = x_prompt.shape
    sb, slen, _ = x_sample.shape
    tp, ts = pb * plen, sb * slen
    t_all = tp + ts
    dk_tot = gla_w_a2.shape[2]
    dv_tot = GLA_HEADS * gla_norm_g.shape[1]
    s5_w = s5_d.shape[1]
    alpha = (2 * depth) ** 0.25
    widths = (dk_tot, dk_tot, dv_tot, dv_tot, GLA_RANK, s5_w, d, d)
    off = [0]
    for wdt in widths:
        off.append(off[-1] + wdt)
    lr0, lr1 = off[4], off[5]
    x = (x_prompt.reshape(tp, d), x_sample.reshape(ts, d))
    x_bf = jnp.concatenate([x[0].astype(BF16), x[1].astype(BF16)], axis=0)
    proj_tile = 4 * LANES
    proj_main = off[-1] // proj_tile * proj_tile
    w_in_t = w_in.transpose(0, 2, 1)
    wa2 = jnp.pad(gla_w_a2, ((0, 0), (0, LANES - GLA_RANK), (0, 0)))
    u_bf = peer_u.astype(BF16)
    vt_bf = peer_v.astype(BF16).transpose(0, 2, 1)
    cols = (off[0], off[1], off[2], off[3], lr0)
    u_col, zg_col, zs_col = off[5], off[6], off[7]
    gla_p = gla_s = None
    re_p, im_p, re_s, im_s = [], [], [], []
    for l in range(depth):
        proj = _mm(x_bf, w_in_t, l, 1024, proj_tile, w_transposed=True, n_out=proj_main + LANES)
        proj = _mm_tail(x_bf, w_in_t, l, proj, 1024, proj_main)

        o_gla, gla_p = _gla(proj, wa2[l], gla_b_a[l], gla_norm_g[l], None, l, gla_p, None, depth=depth, row0=0,
                            n_rows=t_all, nb=pb, seq_len=plen, c=min(GLA_CHUNK, plen), nseq=1, cols=cols)
        o_gla, gla_s = _gla(proj, wa2[l], gla_b_a[l], gla_norm_g[l], state_gla, l, gla_s, o_gla, depth=depth,
                            row0=tp, n_rows=t_all, nb=sb, seq_len=slen, c=min(GLA_CHUNK, slen), nseq=2, cols=cols)

        tables = _s5_tables(s5_a_re[l], s5_a_im[l], s5_log_dt[l], s5_b_re[l], s5_b_im[l], s5_c_re[l], s5_c_im[l],
                            S5_CHUNK)
        y_s5, (r_p, i_p), (r_s, i_s) = _s5_mixer(proj, u_col, tables, s5_d[l], state_s5_re[l], state_s5_im[l],
                                                 pb, plen)
        o_s5 = _glu(y_s5, s5_w_glu, l, s5_b_glu[l], 1024, 1024)

        merged = _branch_merge(o_gla, o_s5, w_branch, l, proj, zg_col, zs_col, 1024, 512)
        mix = _mm(merged, w_out, l, 1024, proj_tile)
        x, x_bf, xt_bf = _deepnorm_ln(x, mix, ln1_g[l], ln1_b[l], alpha, 256, emit_transposed=True)

        q = _mm(x_bf, peer_w_q, l, 1024, proj_tile)
        route = _peer_route(q, peer_key1[l], peer_key2[l], 256)
        ffn_t = _peer_ffn(xt_bf, u_bf, vt_bf, l, route, 512, 512)
        if l + 1 < depth:
            x, x_bf = _deepnorm_ln(x, ffn_t, ln2_g[l], ln2_b[l], alpha, 256, y_transposed=True)
        else:
            y_p, y_s = _deepnorm_ln(x, ffn_t, ln2_g[l], ln2_b[l], alpha, 256, y_transposed=True,
                                    emit_bf16=False, split_out=tp)

        re_p.append(r_p)
        im_p.append(i_p)
        re_s.append(r_s)
        im_s.append(i_s)

    return (y_p.reshape(pb, plen, d), y_s.reshape(sb, slen, d), gla_p, jnp.stack(re_p), jnp.stack(im_p),
            gla_s, jnp.stack(re_s), jnp.stack(im_s))
```

```python
import functools

import jax
import jax.numpy as jnp
from jax import lax
from jax.experimental import pallas as pl
from jax.experimental.pallas import tpu as pltpu

F32 = jnp.float32
BF16 = jnp.bfloat16
HIGHEST = lax.Precision.HIGHEST

V7X_VMEM_BYTES = 64 * 1024 * 1024
VMEM_LIMIT_BYTES = V7X_VMEM_BYTES - 8 * 1024 * 1024
LANES = 128

GLA_HEADS = 4
GLA_RANK = 16
GLA_TAU = 16.0
GLA_CHUNK = 64
RMS_EPS = 1e-6
S5_CHUNK = 8
PEER_HEADS = 8
PEER_TOPK = 16
LN_EPS = 1e-5


def _params(*semantics):
    return pltpu.CompilerParams(dimension_semantics=semantics, vmem_limit_bytes=VMEM_LIMIT_BYTES)


def _dot(a, b):
    return jnp.dot(a, b, preferred_element_type=F32)


def _dot_nt(a, b):
    return lax.dot_general(a, b, (((1,), (1,)), ((), ())), preferred_element_type=F32)


def _mm_kernel(a_ref, b_ref, o_ref):
    o_ref[...] = _dot(a_ref[...], b_ref[...].astype(BF16)).astype(o_ref.dtype)


def _layer_cols(layer, tn, row0=0):
    def spec(rows):
        return pl.BlockSpec((None, rows, tn), lambda i, j: (layer, row0, j))
    return spec


def _mm_nt_kernel(a_ref, bt_ref, o_ref):
    o_ref[...] = _dot_nt(a_ref[...], bt_ref[...].astype(BF16)).astype(o_ref.dtype)


def _mm(a, w, layer, tm, tn, out_dtype=F32, w_transposed=False, n_out=None):
    m, k = a.shape
    n = (w.shape[1] if w_transposed else w.shape[2]) // tn * tn
    w_spec = (pl.BlockSpec((None, tn, k), lambda i, j: (layer, j, 0)) if w_transposed
              else _layer_cols(layer, tn)(k))
    return pl.pallas_call(
        _mm_nt_kernel if w_transposed else _mm_kernel,
        grid=(m // tm, n // tn),
        in_specs=[pl.BlockSpec((tm, k), lambda i, j: (i, 0)), w_spec],
        out_specs=pl.BlockSpec((tm, tn), lambda i, j: (i, j)),
        out_shape=jax.ShapeDtypeStruct((m, n_out or n), out_dtype),
        compiler_params=_params("parallel", "parallel"),
        name="mm",
    )(a, w)


def _mm_tail_kernel(a_ref, bt_ref, prev_ref, o_ref):
    del prev_ref
    tail = _dot_nt(a_ref[...], bt_ref[...].astype(BF16))
    o_ref[...] = jnp.concatenate([tail, jnp.zeros((tail.shape[0], o_ref.shape[1] - tail.shape[1]), F32)], axis=1)


def _mm_tail(a, w_t, layer, out, tm, col0):
    m, k = a.shape
    rows = w_t.shape[1] - col0
    assert 0 < rows < LANES and rows % 8 == 0 and col0 % rows == 0 and col0 % LANES == 0
    return pl.pallas_call(
        _mm_tail_kernel,
        grid=(m // tm,),
        in_specs=[pl.BlockSpec((tm, k), lambda i: (i, 0)),
                  pl.BlockSpec((None, rows, k), lambda i: (layer, col0 // rows, 0)),
                  pl.BlockSpec(memory_space=pl.ANY)],
        out_specs=pl.BlockSpec((tm, LANES), lambda i: (i, col0 // LANES)),
        out_shape=jax.ShapeDtypeStruct(out.shape, out.dtype),
        input_output_aliases={2: 0},
        compiler_params=_params("parallel"),
        name="mm_tail",
    )(a, w_t, out)


def _glu_kernel(y_ref, w_ref, b_ref, ytile_ref, o_ref):
    acc = _dot(y_ref[...].astype(BF16), w_ref[...].astype(BF16)) + b_ref[...]
    o_ref[...] = (ytile_ref[...] * jax.nn.sigmoid(acc)).astype(o_ref.dtype)


def _glu(y, w, layer, b, tm, tn):
    m, k = y.shape
    n = w.shape[2]
    return pl.pallas_call(
        _glu_kernel,
        grid=(m // tm, n // tn),
        in_specs=[pl.BlockSpec((tm, k), lambda i, j: (i, 0)),
                  _layer_cols(layer, tn)(k),
                  pl.BlockSpec((1, tn), lambda i, j: (0, j)),
                  pl.BlockSpec((tm, tn), lambda i, j: (i, j))],
        out_specs=pl.BlockSpec((tm, tn), lambda i, j: (i, j)),
        out_shape=jax.ShapeDtypeStruct((m, n), BF16),
        compiler_params=_params("parallel", "parallel"),
        name="s5_glu",
    )(y, w, b.reshape(1, n), y)


def _lane_shifted(a, b, shift):
    if shift == 0:
        return a
    return jnp.concatenate([a[:, shift:], b[:, :shift]], axis=1)


def _branch_kernel(og_ref, os_ref, w1_ref, w2_ref, zg_ref, zg_next_ref, zs_ref, zs_next_ref, o_ref, *, shift):
    m1 = _dot(og_ref[...], w1_ref[...].astype(BF16))
    m2 = _dot(os_ref[...], w2_ref[...].astype(BF16))
    zg = _lane_shifted(zg_ref[...], zg_next_ref[...], shift)
    zs = _lane_shifted(zs_ref[...], zs_next_ref[...], shift)
    o_ref[...] = (jax.nn.sigmoid(zg) * m1 + jax.nn.sigmoid(zs) * m2).astype(o_ref.dtype)


def _branch_merge(o_gla, o_s5, w, layer, proj, zg_col, zs_col, tm, tn):
    m, k = o_gla.shape
    assert o_s5.shape[1] == k and w.shape[1] == 2 * k
    n = w.shape[2]
    shift = zg_col % LANES
    assert zs_col % LANES == shift and (zg_col - shift) % tn == 0 and (zs_col - shift) % tn == 0

    def gate_specs(col):
        blk, nxt, per = (col - shift) // tn, (col - shift) // LANES, tn // LANES
        return [pl.BlockSpec((tm, tn), lambda i, j: (i, blk + j)),
                pl.BlockSpec((tm, LANES), lambda i, j: (i, nxt + per * (j + 1)))]

    return pl.pallas_call(
        functools.partial(_branch_kernel, shift=shift),
        grid=(m // tm, n // tn),
        in_specs=[pl.BlockSpec((tm, k), lambda i, j: (i, 0)),
                  pl.BlockSpec((tm, k), lambda i, j: (i, 0)),
                  _layer_cols(layer, tn, 0)(k),
                  _layer_cols(layer, tn, 1)(k)] + gate_specs(zg_col) + gate_specs(zs_col),
        out_specs=pl.BlockSpec((tm, tn), lambda i, j: (i, j)),
        out_shape=jax.ShapeDtypeStruct((m, n), BF16),
        compiler_params=_params("parallel", "parallel"),
        name="branch_merge",
    )(o_gla, o_s5, w, w, proj, proj, proj, proj)


def _ln_kernel(*refs, alpha, y_transposed, n_head, two_in, two_out, emit_bf16, emit_transposed):
    refs = list(refs)
    step = pl.program_id(0)
    if two_in:
        x = jnp.where(step < n_head, refs[0][...], refs[1][...])
        refs = refs[2:]
    else:
        x = refs[0][...]
        refs = refs[1:]
    y_ref, g_ref, b_ref = refs[:3]
    outs = refs[3:]
    y = y_ref[...].T if y_transposed else y_ref[...]
    h = alpha * x + y
    mu = jnp.mean(h, axis=-1, keepdims=True)
    d = h - mu
    var = jnp.mean(d * d, axis=-1, keepdims=True)
    o = d * lax.rsqrt(var + LN_EPS) * g_ref[...] + b_ref[...]
    if two_out:
        head_ref, tail_ref = outs[:2]
        outs = outs[2:]

        @pl.when(step < n_head)
        def _():
            head_ref[...] = o

        @pl.when(step >= n_head)
        def _():
            tail_ref[...] = o
    else:
        outs[0][...] = o
        outs = outs[1:]
    if emit_bf16:
        outs[0][...] = o.astype(BF16)
        outs = outs[1:]
    if emit_transposed:
        outs[0][...] = o.T.astype(BF16)


def _deepnorm_ln(x, y, g, b, alpha, tm, y_transposed=False, emit_bf16=True, emit_transposed=False, split_out=None):
    xs = x if isinstance(x, (tuple, list)) else (x,)
    d = xs[0].shape[1]
    m = sum(a.shape[0] for a in xs)
    two_in, two_out = len(xs) == 2, split_out is not None
    n_head = (xs[0].shape[0] if two_in else split_out if two_out else m) // tm
    assert not (two_in and two_out) or xs[0].shape[0] == split_out
    row = pl.BlockSpec((tm, d), lambda i: (i, 0))
    head = pl.BlockSpec((tm, d), lambda i: (jnp.minimum(i, n_head - 1), 0))
    tail = pl.BlockSpec((tm, d), lambda i: (jnp.maximum(i - n_head, 0), 0))
    col = pl.BlockSpec((d, tm), lambda i: (0, i))
    vec = pl.BlockSpec((1, d), lambda i: (0, 0))
    out_specs = [head, tail] if two_out else [row]
    out_shape = ([jax.ShapeDtypeStruct((split_out, d), F32), jax.ShapeDtypeStruct((m - split_out, d), F32)]
                 if two_out else [jax.ShapeDtypeStruct((m, d), F32)])
    if emit_bf16:
        out_specs.append(row)
        out_shape.append(jax.ShapeDtypeStruct((m, d), BF16))
    if emit_transposed:
        out_specs.append(col)
        out_shape.append(jax.ShapeDtypeStruct((d, m), BF16))
    kern = functools.partial(_ln_kernel, alpha=alpha, y_transposed=y_transposed, n_head=n_head, two_in=two_in,
                             two_out=two_out, emit_bf16=emit_bf16, emit_transposed=emit_transposed)
    return pl.pallas_call(
        kern,
        grid=(m // tm,),
        in_specs=([head, tail] if two_in else [row]) + [col if y_transposed else row, vec, vec],
        out_specs=out_specs,
        out_shape=out_shape,
        compiler_params=_params("arbitrary"),
        name="deepnorm_ln",
    )(*xs, y, g.reshape(1, d), b.reshape(1, d))


def _gla_kernel(*refs, c, nseq, heads, scale, has_init, n_prev):
    refs = list(refs)
    q_ref, k_ref, v_ref, r_ref, alr_ref, wa2_ref, ba_ref, g_ref = refs[:8]
    rest = refs[8:]
    s0_ref = rest.pop(0) if has_init else None
    o_ref, s_ref = rest[n_prev:]
    dk = q_ref.shape[-1] // heads
    dv = v_ref.shape[-1] // heads
    kpad = max(c, LANES)
    qpad = max(c, 16)

    @pl.when(pl.program_id(1) == 0)
    def _():
        if has_init:
            s_ref[...] = s0_ref[...]
        else:
            s_ref[...] = jnp.zeros_like(s_ref)

    row = lax.broadcasted_iota(jnp.int32, (c, c), 0)
    col = lax.broadcasted_iota(jnp.int32, (c, c), 1)
    tri_cc = (row >= col).astype(F32)
    rowp = lax.broadcasted_iota(jnp.int32, (qpad, kpad), 0)
    colp = lax.broadcasted_iota(jnp.int32, (qpad, kpad), 1)
    causal = rowp >= colp
    eye = lax.broadcasted_iota(jnp.int32, (dk, dk), 0) == lax.broadcasted_iota(jnp.int32, (dk, dk), 1)

    def pad_rows(x, rows):
        if x.shape[0] == rows:
            return x
        return jnp.concatenate([x, jnp.zeros((rows - x.shape[0], x.shape[1]), x.dtype)], axis=0)

    outs = []
    for s in range(nseq):
        rows = slice(s * c, (s + 1) * c)
        z = jnp.dot(alr_ref[rows, :], wa2_ref[...], precision=HIGHEST, preferred_element_type=F32) + ba_ref[...]
        log_a = (jnp.minimum(z, 0.0) - jnp.log1p(jnp.exp(-jnp.abs(z)))) * (1.0 / GLA_TAU)
        if c >= LANES // 2:
            b = jnp.dot(tri_cc, log_a, precision=HIGHEST, preferred_element_type=F32)
        else:
            ridx = lax.broadcasted_iota(jnp.int32, log_a.shape, 0)
            b = jnp.zeros_like(log_a)
            for t in range(c):
                b = b + jnp.where(ridx >= t, log_a[t:t + 1, :], 0.0)
        b_last = b[c - 1:c, :]
        k = k_ref[rows, :]
        q_dec_all = (q_ref[rows, :] * scale * jnp.exp(b)).astype(BF16)
        k_inv_all = (k * jnp.exp(-b)).astype(BF16)
        k_end_all = (k * jnp.exp(b_last - b)).astype(BF16)
        v_all = v_ref[rows, :].astype(BF16)
        r = r_ref[rows, :]
        gate = r * jax.nn.sigmoid(r)
        o_heads = []
        for h in range(heads):
            kc = slice(h * dk, (h + 1) * dk)
            vc = slice(h * dv, (h + 1) * dv)
            q_dec = pad_rows(q_dec_all[:, kc], qpad)
            k_inv = pad_rows(k_inv_all[:, kc], kpad)
            k_end = pad_rows(k_end_all[:, kc], kpad)
            v_bf = pad_rows(v_all[:, vc], kpad)
            scores = jnp.where(causal, _dot_nt(q_dec, k_inv), 0.0).astype(BF16)
            state = s_ref[s, h]
            o = (_dot(q_dec, state.astype(BF16)) + _dot(scores, v_bf))[:c]
            decay_col = jnp.exp(jnp.sum(jnp.where(eye, b_last[:, kc], 0.0), axis=1, keepdims=True))
            s_ref[s, h] = decay_col * state + lax.dot_general(k_end, v_bf, (((0,), (0,)), ((), ())),
                                                              preferred_element_type=F32)
            o = o * lax.rsqrt(jnp.mean(o * o, axis=-1, keepdims=True) + RMS_EPS) * g_ref[...]
            o_heads.append(o * gate[:, vc])
        outs.append(jnp.concatenate(o_heads, axis=1))
    o_all = outs[0] if nseq == 1 else jnp.concatenate(outs, axis=0)
    o_ref[...] = o_all.astype(o_ref.dtype)


def _gla(proj, wa2, ba, norm_g, s0, layer, s_prev, o_prev, *, depth, row0, n_rows, nb, seq_len, c, nseq, cols):
    heads = GLA_HEADS
    dk_tot = wa2.shape[1]
    dk = dk_tot // heads
    dv = norm_g.shape[0]
    dv_tot = heads * dv
    nchunk = seq_len // c
    assert nseq == 1 or nchunk == 1
    rb = nseq * c
    blk0 = row0 // rb
    q_col, k_col, v_col, r_col, lr_col = cols

    def rows_map(col_blk):
        return lambda b, j: (blk0 + b * nchunk + j, col_blk)

    in_specs = [
        pl.BlockSpec((rb, dk_tot), rows_map(q_col // dk_tot)),
        pl.BlockSpec((rb, dk_tot), rows_map(k_col // dk_tot)),
        pl.BlockSpec((rb, dv_tot), rows_map(v_col // dv_tot)),
        pl.BlockSpec((rb, dv_tot), rows_map(r_col // dv_tot)),
        pl.BlockSpec((rb, LANES), rows_map(lr_col // LANES)),
        pl.BlockSpec((LANES, dk_tot), lambda b, j: (0, 0)),
        pl.BlockSpec((1, dk_tot), lambda b, j: (0, 0)),
        pl.BlockSpec((1, dv), lambda b, j: (0, 0)),
    ]
    args = [proj, proj, proj, proj, proj, wa2, ba.reshape(1, -1), norm_g.reshape(1, dv)]
    state_spec = pl.BlockSpec((None, nseq, heads, dk, dv), lambda b, j: (layer, b, 0, 0, 0))
    if s0 is not None:
        in_specs.append(state_spec)
        args.append(s0)
    aliases = {}
    for prev, out_idx in ((s_prev, 1), (o_prev, 0)):
        if prev is not None:
            aliases[len(args)] = out_idx
            in_specs.append(pl.BlockSpec(memory_space=pl.ANY))
            args.append(prev)
    kern = functools.partial(_gla_kernel, c=c, nseq=nseq, heads=heads, scale=dk ** -0.5,
                             has_init=s0 is not None, n_prev=len(aliases))
    return pl.pallas_call(
        kern,
        grid=(nb // nseq, nchunk),
        in_specs=in_specs,
        out_specs=[pl.BlockSpec((rb, dv_tot), rows_map(0)), state_spec],
        out_shape=[jax.ShapeDtypeStruct((n_rows, dv_tot), BF16),
                   jax.ShapeDtypeStruct((depth, nb, heads, dk, dv), F32)],
        input_output_aliases=aliases,
        compiler_params=_params("parallel", "arbitrary"),
        name="gla",
    )(*args)


def _s5_tables(a_re, a_im, log_dt, b_re, b_im, c_re, c_im, c):
    dt = jnp.exp(log_dt)[:, None]
    mag = jnp.exp(a_re * dt)
    ar, ai = mag * jnp.cos(a_im * dt), mag * jnp.sin(a_im * dt)
    den = a_re * a_re + a_im * a_im
    cr = ((ar - 1.0) * a_re + ai * a_im) / den
    ci = (ai * a_re - (ar - 1.0) * a_im) / den
    bbr = cr[..., None] * b_re - ci[..., None] * b_im
    bbi = cr[..., None] * b_im + ci[..., None] * b_re
    tau = jnp.arange(c + 1, dtype=F32)[:, None, None]
    pmag = jnp.exp(tau * (a_re * dt))
    pw_r, pw_i = pmag * jnp.cos(tau * (a_im * dt)), pmag * jnp.sin(tau * (a_im * dt))
    ca_r = c_re[None] * pw_r[:, :, None, :] - c_im[None] * pw_i[:, :, None, :]
    ca_i = c_re[None] * pw_i[:, :, None, :] + c_im[None] * pw_r[:, :, None, :]
    kern = (jnp.einsum('tgmp,gpn->tgmn', ca_r[:c], bbr, precision=HIGHEST)
            - jnp.einsum('tgmp,gpn->tgmn', ca_i[:c], bbi, precision=HIGHEST))
    g, m, n = kern.shape[1:]
    p = a_re.shape[1]
    gpb = LANES // n
    nb = g // gpb

    def block_diag(x):
        lead, _, r, w = x.shape
        copies = jnp.tile(jnp.eye(w, dtype=F32), (1, gpb))
        wide = jnp.dot(x.reshape(-1, w), copies, precision=HIGHEST).reshape(lead, nb, gpb * r, gpb * w)
        return wide * jnp.kron(jnp.eye(gpb, dtype=F32), jnp.ones((r, w), F32))

    lag_blocks = block_diag(kern.transpose(0, 1, 3, 2))
    zero_block = jnp.zeros_like(lag_blocks[0])
    toep = jnp.concatenate(
        [jnp.concatenate([lag_blocks[t - s] if t >= s else zero_block for s in range(c)], axis=1)
         for t in range(c)], axis=2)
    back = (c - 1.0) - tau[:c]
    rev_mag = jnp.exp(back * (a_re * dt))
    rev_r, rev_i = rev_mag * jnp.cos(back * (a_im * dt)), rev_mag * jnp.sin(back * (a_im * dt))
    inj_r = rev_r[..., None] * bbr[None] - rev_i[..., None] * bbi[None]
    inj_i = rev_r[..., None] * bbi[None] + rev_i[..., None] * bbr[None]
    inj = jnp.concatenate([block_diag(x.transpose(0, 1, 3, 2)) for x in (inj_r, inj_i)], axis=3)
    inj = inj.transpose(1, 0, 2, 3).reshape(nb, c * LANES, 2 * gpb * p)
    read = jnp.concatenate(
        [jnp.concatenate(list(block_diag(x.transpose(0, 1, 3, 2))), axis=2)
         for x in (ca_r[1:], -ca_i[1:])], axis=1)
    step_r, step_i = pw_r[c].reshape(nb, 1, gpb * p), pw_i[c].reshape(nb, 1, gpb * p)
    step_same = jnp.concatenate([step_r, step_r], axis=-1)
    step_swap = jnp.concatenate([-step_i, step_i], axis=-1)
    return toep.astype(BF16), inj.astype(BF16), read.astype(BF16), step_same, step_swap


def _chunk_tokens(u_ref, u_next_ref, c, shift):
    r = u_ref.shape[0] // c
    return [_lane_shifted(u_ref[pl.ds(s, r, stride=c), :], u_next_ref[pl.ds(s, r, stride=c), :], shift)
            for s in range(c)]


def _chunk_rows(tokens):
    return jnp.concatenate([x.astype(BF16) for x in tokens], axis=1)


def _u_specs(t_tokens, u_col):
    blk = u_col // LANES
    return [pl.BlockSpec((t_tokens, LANES), lambda b: (0, blk + b)),
            pl.BlockSpec((t_tokens, LANES), lambda b: (0, blk + b + 1))]


def _s5_inject_kernel(u_ref, u_next_ref, w_ref, e_ref, *, c, shift):
    e_ref[...] = _dot(_chunk_rows(_chunk_tokens(u_ref, u_next_ref, c, shift)), w_ref[...])


def _s5_inject(proj, u_col, inj, c):
    t_tokens = proj.shape[0]
    nb, k, w = inj.shape
    return pl.pallas_call(
        functools.partial(_s5_inject_kernel, c=c, shift=u_col % LANES),
        grid=(nb,),
        in_specs=_u_specs(t_tokens, u_col) + [pl.BlockSpec((None, k, w), lambda b: (b, 0, 0))],
        out_specs=pl.BlockSpec((None, t_tokens // c, w), lambda b: (b, 0, 0)),
        out_shape=jax.ShapeDtypeStruct((nb, t_tokens // c, w), F32),
        compiler_params=_params("parallel"),
        name="s5_inject",
    )(proj, proj, inj)


def _s5_scan_kernel(e_ref, x0_ref, same_ref, swap_ref, xprev_ref, xfin_p_ref, xfin_s_ref, *, n_seq, n_chunk):
    half = e_ref.shape[-1] // 2
    same = same_ref[...]
    swap = swap_ref[...]

    def advance(x, e):
        x_swapped = jnp.concatenate([x[:, half:], x[:, :half]], axis=1)
        return same * x + swap * x_swapped + e

    def body(j, xs):
        new = []
        for q, x in enumerate(xs):
            row = q * n_chunk + j
            xprev_ref[pl.ds(row, 1), :] = x
            new.append(advance(x, e_ref[pl.ds(row, 1), :]))
        return tuple(new)

    zero = jnp.zeros((1, e_ref.shape[-1]), F32)
    xs = lax.fori_loop(0, n_chunk, body, (zero,) * n_seq)
    xfin_p_ref[...] = jnp.zeros_like(xfin_p_ref)
    for q, x in enumerate(xs):
        xfin_p_ref[q:q + 1, :] = x
    rp = n_seq * n_chunk
    x0 = x0_ref[...]
    xprev_ref[rp:, :] = x0
    xfin_s_ref[...] = advance(x0, e_ref[rp:, :])


def _s5_scan(e, x0_s, same, swap, n_seq, n_chunk):
    nb, r, w = e.shape
    ns = x0_s.shape[1]
    assert r == n_seq * n_chunk + ns
    pad_seq = -(-n_seq // 8) * 8
    vec = pl.BlockSpec((None, 1, w), lambda b: (b, 0, 0))
    return pl.pallas_call(
        functools.partial(_s5_scan_kernel, n_seq=n_seq, n_chunk=n_chunk),
        grid=(nb,),
        in_specs=[pl.BlockSpec((None, r, w), lambda b: (b, 0, 0)),
                  pl.BlockSpec((None, ns, w), lambda b: (b, 0, 0)), vec, vec],
        out_specs=[pl.BlockSpec((None, r, w), lambda b: (b, 0, 0)),
                   pl.BlockSpec((None, pad_seq, w), lambda b: (b, 0, 0)),
                   pl.BlockSpec((None, ns, w), lambda b: (b, 0, 0))],
        out_shape=[jax.ShapeDtypeStruct((nb, r, w), F32), jax.ShapeDtypeStruct((nb, pad_seq, w), F32),
                   jax.ShapeDtypeStruct((nb, ns, w), F32)],
        compiler_params=_params("parallel"),
        name="s5_scan",
    )(e, x0_s, same, swap)


def _s5_out_kernel(u_ref, u_next_ref, xp_ref, toep_ref, read_ref, d_ref, y_ref, *, c, shift):
    r = u_ref.shape[0] // c
    tokens = _chunk_tokens(u_ref, u_next_ref, c, shift)
    y = _dot(_chunk_rows(tokens), toep_ref[...]) + _dot(xp_ref[...].astype(BF16), read_ref[...])
    for t in range(c):
        y_ref[pl.ds(t, r, stride=c), :] = jax.nn.gelu(y[:, t * LANES:(t + 1) * LANES] + d_ref[...] * tokens[t])


def _s5_out(proj, u_col, xprev, toep, read, d_skip, c):
    t_tokens = proj.shape[0]
    nb, r, w = xprev.shape
    k = toep.shape[1]
    return pl.pallas_call(
        functools.partial(_s5_out_kernel, c=c, shift=u_col % LANES),
        grid=(nb,),
        in_specs=_u_specs(t_tokens, u_col) + [
            pl.BlockSpec((None, r, w), lambda b: (b, 0, 0)),
            pl.BlockSpec((None, k, k), lambda b: (b, 0, 0)),
            pl.BlockSpec((None, w, k), lambda b: (b, 0, 0)),
            pl.BlockSpec((1, LANES), lambda b: (0, b))],
        out_specs=pl.BlockSpec((t_tokens, LANES), lambda b: (0, b)),
        out_shape=jax.ShapeDtypeStruct((t_tokens, nb * LANES), F32),
        compiler_params=_params("parallel"),
        name="s5_out",
    )(proj, proj, xprev, toep, read, d_skip.reshape(1, -1))


def _s5_mixer(proj, u_col, tables, d_skip, st_re, st_im, n_prompt_seq, prompt_len):
    toep, inj, read, step_same, step_swap = tables
    c = S5_CHUNK
    nb = toep.shape[0]
    ns, groups, p = st_re.shape
    gpb = groups // nb

    def to_blocks(st):
        return st.reshape(ns, nb, gpb * p).transpose(1, 0, 2)

    def from_blocks(x):
        return x.transpose(1, 0, 2).reshape(x.shape[1], groups, p)

    e = _s5_inject(proj, u_col, inj, c)
    x0_s = jnp.concatenate([to_blocks(st_re), to_blocks(st_im)], axis=-1)
    xprev, xfin_p, xfin_s = _s5_scan(e, x0_s, step_same, step_swap, n_prompt_seq, prompt_len // c)
    y = _s5_out(proj, u_col, xprev, toep, read, d_skip, c)
    half = gpb * p
    xfin_p = xfin_p[:, :n_prompt_seq]
    return (y, (from_blocks(xfin_p[..., :half]), from_blocks(xfin_p[..., half:])),
            (from_blocks(xfin_s[..., :half]), from_blocks(xfin_s[..., half:])))


def _sorting_network(n):
    pairs = []
    p = 1
    while p < n:
        k = p
        while k >= 1:
            for j in range(k % p, n - k, 2 * k):
                for i in range(min(k, n - j - k)):
                    if (i + j) // (2 * p) == (i + j + k) // (2 * p):
                        pairs.append((i + j, i + j + k))
            k //= 2
        p *= 2
    return pairs


def _peer_route_kernel(q_ref, k1_ref, k2_ref, s1_ref, w1_ref, s2_ref, e2_ref, tau_ref):
    half = k1_ref.shape[1]
    neg = jnp.float32(-jnp.inf)

    def top_values(x):
        slabs = [x[r:r + 8] for r in range(0, x.shape[0], 8)]
        for lo, hi in _sorting_network(len(slabs)):
            slabs[lo], slabs[hi] = jnp.maximum(slabs[lo], slabs[hi]), jnp.minimum(slabs[lo], slabs[hi])
        stack_id = lax.broadcasted_iota(jnp.int32, slabs[0].shape, 0)
        vals = []
        for taken in range(PEER_TOPK):
            m = jnp.max(slabs[0], axis=0, keepdims=True)
            vals.append(m)
            depth = min(len(slabs), PEER_TOPK - taken - 1)
            first = jnp.min(jnp.where(slabs[0] == m, stack_id, 8), axis=0, keepdims=True)
            popped = stack_id == first
            slabs = [jnp.where(popped, slabs[k + 1] if k + 1 < len(slabs) else neg, slabs[k])
                     for k in range(depth)]
        return vals

    for h in range(PEER_HEADS):
        q1 = q_ref[:, (2 * h) * half:(2 * h + 1) * half]
        q2 = q_ref[:, (2 * h + 1) * half:(2 * h + 2) * half]
        s1 = lax.dot_general(k1_ref[...], q1, (((1,), (1,)), ((), ())), precision=HIGHEST,
                             preferred_element_type=F32)
        s2 = lax.dot_general(k2_ref[...], q2, (((1,), (1,)), ((), ())), precision=HIGHEST,
                             preferred_element_type=F32)
        a = top_values(s1)
        b = jnp.concatenate(top_values(s2), axis=0)
        b0 = b[0:1]
        cand = [a[i] + b[:PEER_TOPK // (i + 1)] for i in range(PEER_TOPK)]
        n_cand = sum(x.shape[0] for x in cand)
        n_rows = 8 * pl.next_power_of_2(-(-n_cand // 8))
        cand.append(jnp.full((n_rows - n_cand, s1.shape[1]), neg, F32))
        top = top_values(jnp.concatenate(cand, axis=0))
        zsum = top[0] * 0.0
        for m in top:
            zsum = zsum + jnp.exp(m - top[0])
        s1_ref[h] = s1
        s2_ref[h] = s2
        w1_ref[h] = jnp.exp(s1 - a[0]) / zsum
        e2_ref[h] = jnp.exp(s2 - b0)
        tau_ref[h] = top[-1]


def _peer_route(q, key1, key2, tm):
    t_tokens = q.shape[0]
    nk, half = key1.shape
    big = pl.BlockSpec((PEER_HEADS, nk, tm), lambda i: (0, 0, i))
    big_shape = jax.ShapeDtypeStruct((PEER_HEADS, nk, t_tokens), F32)
    return pl.pallas_call(
        _peer_route_kernel,
        grid=(t_tokens // tm,),
        in_specs=[pl.BlockSpec((tm, q.shape[1]), lambda i: (i, 0)),
                  pl.BlockSpec((nk, half), lambda i: (0, 0)),
                  pl.BlockSpec((nk, half), lambda i: (0, 0))],
        out_specs=[big, big, big, big, pl.BlockSpec((PEER_HEADS, 1, tm), lambda i: (0, 0, i))],
        out_shape=[big_shape, big_shape, big_shape, big_shape,
                   jax.ShapeDtypeStruct((PEER_HEADS, 1, t_tokens), F32)],
        compiler_params=_params("parallel"),
        name="peer_route",
    )(q, key1, key2)


def _peer_ffn_kernel(xt_ref, u_ref, v_ref, s1_ref, w1_ref, s2_ref, e2_ref, tau_ref, o_ref, p_ref, h_ref, g_ref,
                     *, ni, n_et):
    step = pl.program_id(1)
    et = jnp.minimum(step, n_et - 1)
    slot = step % 2

    @pl.when(step == 0)
    def _():
        o_ref[...] = jnp.zeros_like(o_ref)
        p_ref[...] = jnp.zeros_like(p_ref)

    nk = s2_ref.shape[1]
    te, tm = h_ref.shape
    d = o_ref.shape[0]
    chunks = [(ii, tc) for ii in range(ni) for tc in range(tm // LANES)]

    def block(ii, tc):
        return slice(ii * nk, (ii + 1) * nk), slice(tc * LANES, (tc + 1) * LANES)

    def gate_chunk(ii, tc):
        rows, lanes = block(ii, tc)
        i = et * ni + ii
        g = None
        for h in range(PEER_HEADS):
            s1_row = s1_ref[h, pl.ds(i, 1), :][:, lanes]
            w1_row = w1_ref[h, pl.ds(i, 1), :][:, lanes]
            hit = (s1_row + s2_ref[h, :, lanes]) >= tau_ref[h, :, lanes]
            term = jnp.where(hit, e2_ref[h, :, lanes] * w1_row, 0.0)
            g = term if g is None else g + term
        g_ref[rows, lanes] = g

    def finish_chunk(ii, tc):
        rows, lanes = block(ii, tc)
        p_ref[slot, rows, lanes] = (g_ref[rows, lanes] * jax.nn.gelu(h_ref[rows, lanes])).astype(BF16)

    quarter = len(chunks) // 4
    half = tm // 2
    for c in chunks[:quarter]:
        gate_chunk(*c)
    h_ref[:, :half] = _dot(u_ref[...], xt_ref[:, :half])
    for c in chunks[quarter:2 * quarter]:
        gate_chunk(*c)
    h_ref[:, half:] = _dot(u_ref[...], xt_ref[:, half:])
    p_prev = p_ref[1 - slot]
    vector_work = [[(gate_chunk, c) for c in chunks[2 * quarter:3 * quarter]],
                   [(gate_chunk, c) for c in chunks[3 * quarter:]],
                   [(finish_chunk, c) for c in chunks[:2 * quarter]],
                   [(finish_chunk, c) for c in chunks[2 * quarter:]]]
    dq = d // len(vector_work)
    for k, work in enumerate(vector_work):
        for fn, c in work:
            fn(*c)
        rows = slice(k * dq, (k + 1) * dq)
        o_ref[rows, :] += lax.dot_general(v_ref[:, rows], p_prev, (((0,), (0,)), ((), ())),
                                          preferred_element_type=F32)


def _peer_ffn(xt_bf, u_tab, v_tab, layer, route, tm, te):
    s1, w1, s2, e2, tau = route
    d, t_tokens = xt_bf.shape
    n_exp = u_tab.shape[1]
    nk = s2.shape[1]
    ni = te // nk
    n_et = n_exp // te
    once = pl.Buffered(1)
    tok = pl.BlockSpec((PEER_HEADS, nk, tm), lambda i, j: (0, 0, i), pipeline_mode=once)
    return pl.pallas_call(
        functools.partial(_peer_ffn_kernel, ni=ni, n_et=n_et),
        grid=(t_tokens // tm, n_et + 1),
        in_specs=[pl.BlockSpec((d, tm), lambda i, j: (0, i), pipeline_mode=once),
                  pl.BlockSpec((None, te, d), lambda i, j: (layer, jnp.minimum(j, n_et - 1), 0)),
                  pl.BlockSpec((None, te, d), lambda i, j: (layer, jnp.maximum(j - 1, 0), 0)),
                  tok, tok, tok, tok,
                  pl.BlockSpec((PEER_HEADS, 1, tm), lambda i, j: (0, 0, i), pipeline_mode=once)],
        out_specs=pl.BlockSpec((d, tm), lambda i, j: (0, i)),
        out_shape=jax.ShapeDtypeStruct((d, t_tokens), F32),
        scratch_shapes=[pltpu.VMEM((2, te, tm), BF16), pltpu.VMEM((te, tm), F32), pltpu.VMEM((te, tm), F32)],
        compiler_params=_params("parallel", "arbitrary"),
        name="peer_ffn",
    )(xt_bf, u_tab, v_tab, s1, w1, s2, e2, tau)


def kernel(x_prompt, x_sample, state_gla, state_s5_re, state_s5_im, ln1_g, ln1_b, w_in, gla_w_a2, gla_b_a, gla_norm_g, s5_a_re, s5_a_im, s5_log_dt, s5_b_re, s5_b_im, s5_c_re, s5_c_im, s5_d, s5_w_glu, s5_b_glu, w_branch, w_out, ln2_g, ln2_b, peer_w_q, peer_key1, peer_key2, peer_u, peer_v):
    depth = w_in.shape[0]
    pb, plen, d = x_prompt.shape
    sb, slen, _ = x_sample.shape
    tp, ts = pb * plen, sb * slen
    t_all = tp + ts
    dk_tot = gla_w_a2.shape[2]
    dv_tot = GLA_HEADS * gla_norm_g.shape[1]
    s5_w = s5_d.shape[1]
    alpha = (2 * depth) ** 0.25
    widths = (dk_tot, dk_tot, dv_tot, dv_tot, GLA_RANK, s5_w, d, d)
    off = [0]
    for wdt in widths:
        off.append(off[-1] + wdt)
    lr0, lr1 = off[4], off[5]
    x = (x_prompt.reshape(tp, d), x_sample.reshape(ts, d))
    x_bf = jnp.concatenate([x[0].astype(BF16), x[1].astype(BF16)], axis=0)
    proj_tile = 4 * LANES
    proj_main = off[-1] // proj_tile * proj_tile
    w_in_t = w_in.transpose(0, 2, 1)
    wa2 = jnp.pad(gla_w_a2, ((0, 0), (0, LANES - GLA_RANK), (0, 0)))
    u_bf = peer_u.astype(BF16)
    v_bf = peer_v.astype(BF16)
    cols = (off[0], off[1], off[2], off[3], lr0)
    u_col, zg_col, zs_col = off[5], off[6], off[7]
    gla_p = gla_s = None
    re_p, im_p, re_s, im_s = [], [], [], []
    for l in range(depth):
        proj = _mm(x_bf, w_in_t, l, 1024, proj_tile, w_transposed=True, n_out=proj_main + LANES)
        proj = _mm_tail(x_bf, w_in_t, l, proj, 1024, proj_main)

        o_gla, gla_p = _gla(proj, wa2[l], gla_b_a[l], gla_norm_g[l], None, l, gla_p, None, depth=depth, row0=0,
                            n_rows=t_all, nb=pb, seq_len=plen, c=min(GLA_CHUNK, plen), nseq=1, cols=cols)
        o_gla, gla_s = _gla(proj, wa2[l], gla_b_a[l], gla_norm_g[l], state_gla, l, gla_s, o_gla, depth=depth,
                            row0=tp, n_rows=t_all, nb=sb, seq_len=slen, c=min(GLA_CHUNK, slen), nseq=2, cols=cols)

        tables = _s5_tables(s5_a_re[l], s5_a_im[l], s5_log_dt[l], s5_b_re[l], s5_b_im[l], s5_c_re[l], s5_c_im[l],
                            S5_CHUNK)
        y_s5, (r_p, i_p), (r_s, i_s) = _s5_mixer(proj, u_col, tables, s5_d[l], state_s5_re[l], state_s5_im[l],
                                                 pb, plen)
        o_s5 = _glu(y_s5, s5_w_glu, l, s5_b_glu[l], 1024, 1024)

        merged = _branch_merge(o_gla, o_s5, w_branch, l, proj, zg_col, zs_col, 1024, 512)
        mix = _mm(merged, w_out, l, 1024, proj_tile)
        x, x_bf, xt_bf = _deepnorm_ln(x, mix, ln1_g[l], ln1_b[l], alpha, 256, emit_transposed=True)

        q = _mm(x_bf, peer_w_q, l, 1024, proj_tile)
        route = _peer_route(q, peer_key1[l], peer_key2[l], 256)
        ffn_t = _peer_ffn(xt_bf, u_bf, v_bf, l, route, 512, 512)
        if l + 1 < depth:
            x, x_bf = _deepnorm_ln(x, ffn_t, ln2_g[l], ln2_b[l], alpha, 256, y_transposed=True)
        else:
            y_p, y_s = _deepnorm_ln(x, ffn_t, ln2_g[l], ln2_b[l], alpha, 256, y_transposed=True,
                                    emit_bf16=False, split_out=tp)

        re_p.append(r_p)
        im_p.append(i_p)
        re_s.append(r_s)
        im_s.append(i_s)

    return (y_p.reshape(pb, plen, d), y_s.reshape(sb, slen, d), gla_p, jnp.stack(re_p), jnp.stack(im_p),
            gla_s, jnp.stack(re_s), jnp.stack(im_s))
```

```python
import functools

import jax
import jax.numpy as jnp
from jax import lax
from jax.experimental import pallas as pl
from jax.experimental.pallas import tpu as pltpu

F32 = jnp.float32
BF16 = jnp.bfloat16
HIGHEST = lax.Precision.HIGHEST

V7X_VMEM_BYTES = 64 * 1024 * 1024
VMEM_LIMIT_BYTES = V7X_VMEM_BYTES - 8 * 1024 * 1024
LANES = 128

GLA_HEADS = 4
GLA_RANK = 16
GLA_TAU = 16.0
GLA_CHUNK = 64
RMS_EPS = 1e-6
S5_CHUNK = 8
PEER_HEADS = 8
PEER_TOPK = 16
LN_EPS = 1e-5


def _params(*semantics):
    return pltpu.CompilerParams(dimension_semantics=semantics, vmem_limit_bytes=VMEM_LIMIT_BYTES)


def _dot(a, b):
    return jnp.dot(a, b, preferred_element_type=F32)


def _dot_nt(a, b):
    return lax.dot_general(a, b, (((1,), (1,)), ((), ())), preferred_element_type=F32)


def _mm_kernel(a_ref, b_ref, o_ref):
    o_ref[...] = _dot(a_ref[...], b_ref[...].astype(BF16)).astype(o_ref.dtype)


def _layer_cols(layer, tn, row0=0):
    def spec(rows):
        return pl.BlockSpec((None, rows, tn), lambda i, j: (layer, row0, j))
    return spec


def _mm_nt_kernel(a_ref, bt_ref, o_ref):
    o_ref[...] = _dot_nt(a_ref[...], bt_ref[...].astype(BF16)).astype(o_ref.dtype)


def _mm(a, w, layer, tm, tn, out_dtype=F32, w_transposed=False, n_out=None):
    m, k = a.shape
    n = (w.shape[1] if w_transposed else w.shape[2]) // tn * tn
    w_spec = (pl.BlockSpec((None, tn, k), lambda i, j: (layer, j, 0)) if w_transposed
              else _layer_cols(layer, tn)(k))
    return pl.pallas_call(
        _mm_nt_kernel if w_transposed else _mm_kernel,
        grid=(m // tm, n // tn),
        in_specs=[pl.BlockSpec((tm, k), lambda i, j: (i, 0)), w_spec],
        out_specs=pl.BlockSpec((tm, tn), lambda i, j: (i, j)),
        out_shape=jax.ShapeDtypeStruct((m, n_out or n), out_dtype),
        compiler_params=_params("parallel", "parallel"),
        name="mm",
    )(a, w)


def _mm_tail_kernel(a_ref, bt_ref, prev_ref, o_ref):
    del prev_ref
    tail = _dot_nt(a_ref[...], bt_ref[...].astype(BF16))
    o_ref[...] = jnp.concatenate([tail, jnp.zeros((tail.shape[0], o_ref.shape[1] - tail.shape[1]), F32)], axis=1)


def _mm_tail(a, w_t, layer, out, tm, col0):
    m, k = a.shape
    rows = w_t.shape[1] - col0
    assert 0 < rows < LANES and rows % 8 == 0 and col0 % rows == 0 and col0 % LANES == 0
    return pl.pallas_call(
        _mm_tail_kernel,
        grid=(m // tm,),
        in_specs=[pl.BlockSpec((tm, k), lambda i: (i, 0)),
                  pl.BlockSpec((None, rows, k), lambda i: (layer, col0 // rows, 0)),
                  pl.BlockSpec(memory_space=pl.ANY)],
        out_specs=pl.BlockSpec((tm, LANES), lambda i: (i, col0 // LANES)),
        out_shape=jax.ShapeDtypeStruct(out.shape, out.dtype),
        input_output_aliases={2: 0},
        compiler_params=_params("parallel"),
        name="mm_tail",
    )(a, w_t, out)


def _glu_kernel(y_ref, w_ref, b_ref, ytile_ref, o_ref):
    acc = _dot(y_ref[...].astype(BF16), w_ref[...].astype(BF16)) + b_ref[...]
    o_ref[...] = (ytile_ref[...] * jax.nn.sigmoid(acc)).astype(o_ref.dtype)


def _glu(y, w, layer, b, tm, tn):
    m, k = y.shape
    n = w.shape[2]
    return pl.pallas_call(
        _glu_kernel,
        grid=(m // tm, n // tn),
        in_specs=[pl.BlockSpec((tm, k), lambda i, j: (i, 0)),
                  _layer_cols(layer, tn)(k),
                  pl.BlockSpec((1, tn), lambda i, j: (0, j)),
                  pl.BlockSpec((tm, tn), lambda i, j: (i, j))],
        out_specs=pl.BlockSpec((tm, tn), lambda i, j: (i, j)),
        out_shape=jax.ShapeDtypeStruct((m, n), BF16),
        compiler_params=_params("parallel", "parallel"),
        name="s5_glu",
    )(y, w, b.reshape(1, n), y)


def _lane_shifted(a, b, shift):
    if shift == 0:
        return a
    return jnp.concatenate([a[:, shift:], b[:, :shift]], axis=1)


def _branch_kernel(og_ref, os_ref, w1_ref, w2_ref, zg_ref, zg_next_ref, zs_ref, zs_next_ref, o_ref, *, shift):
    m1 = _dot(og_ref[...], w1_ref[...].astype(BF16))
    m2 = _dot(os_ref[...], w2_ref[...].astype(BF16))
    zg = _lane_shifted(zg_ref[...], zg_next_ref[...], shift)
    zs = _lane_shifted(zs_ref[...], zs_next_ref[...], shift)
    o_ref[...] = (jax.nn.sigmoid(zg) * m1 + jax.nn.sigmoid(zs) * m2).astype(o_ref.dtype)


def _branch_merge(o_gla, o_s5, w, layer, proj, zg_col, zs_col, tm, tn):
    m, k = o_gla.shape
    assert o_s5.shape[1] == k and w.shape[1] == 2 * k
    n = w.shape[2]
    shift = zg_col % LANES
    assert zs_col % LANES == shift and (zg_col - shift) % tn == 0 and (zs_col - shift) % tn == 0

    def gate_specs(col):
        blk, nxt, per = (col - shift) // tn, (col - shift) // LANES, tn // LANES
        return [pl.BlockSpec((tm, tn), lambda i, j: (i, blk + j)),
                pl.BlockSpec((tm, LANES), lambda i, j: (i, nxt + per * (j + 1)))]

    return pl.pallas_call(
        functools.partial(_branch_kernel, shift=shift),
        grid=(m // tm, n // tn),
        in_specs=[pl.BlockSpec((tm, k), lambda i, j: (i, 0)),
                  pl.BlockSpec((tm, k), lambda i, j: (i, 0)),
                  _layer_cols(layer, tn, 0)(k),
                  _layer_cols(layer, tn, 1)(k)] + gate_specs(zg_col) + gate_specs(zs_col),
        out_specs=pl.BlockSpec((tm, tn), lambda i, j: (i, j)),
        out_shape=jax.ShapeDtypeStruct((m, n), BF16),
        compiler_params=_params("parallel", "parallel"),
        name="branch_merge",
    )(o_gla, o_s5, w, w, proj, proj, proj, proj)


def _ln_kernel(*refs, alpha, y_transposed, n_head, two_in, two_out, emit_bf16, emit_transposed):
    refs = list(refs)
    step = pl.program_id(0)
    if two_in:
        x = jnp.where(step < n_head, refs[0][...], refs[1][...])
        refs = refs[2:]
    else:
        x = refs[0][...]
        refs = refs[1:]
    y_ref, g_ref, b_ref = refs[:3]
    outs = refs[3:]
    y = y_ref[...].T if y_transposed else y_ref[...]
    h = alpha * x + y
    mu = jnp.mean(h, axis=-1, keepdims=True)
    d = h - mu
    var = jnp.mean(d * d, axis=-1, keepdims=True)
    o = d * lax.rsqrt(var + LN_EPS) * g_ref[...] + b_ref[...]
    if two_out:
        head_ref, tail_ref = outs[:2]
        outs = outs[2:]

        @pl.when(step < n_head)
        def _():
            head_ref[...] = o

        @pl.when(step >= n_head)
        def _():
            tail_ref[...] = o
    else:
        outs[0][...] = o
        outs = outs[1:]
    if emit_bf16:
        outs[0][...] = o.astype(BF16)
        outs = outs[1:]
    if emit_transposed:
        outs[0][...] = o.T.astype(BF16)


def _deepnorm_ln(x, y, g, b, alpha, tm, y_transposed=False, emit_bf16=True, emit_transposed=False, split_out=None):
    xs = x if isinstance(x, (tuple, list)) else (x,)
    d = xs[0].shape[1]
    m = sum(a.shape[0] for a in xs)
    two_in, two_out = len(xs) == 2, split_out is not None
    n_head = (xs[0].shape[0] if two_in else split_out if two_out else m) // tm
    assert not (two_in and two_out) or xs[0].shape[0] == split_out
    row = pl.BlockSpec((tm, d), lambda i: (i, 0))
    head = pl.BlockSpec((tm, d), lambda i: (jnp.minimum(i, n_head - 1), 0))
    tail = pl.BlockSpec((tm, d), lambda i: (jnp.maximum(i - n_head, 0), 0))
    col = pl.BlockSpec((d, tm), lambda i: (0, i))
    vec = pl.BlockSpec((1, d), lambda i: (0, 0))
    out_specs = [head, tail] if two_out else [row]
    out_shape = ([jax.ShapeDtypeStruct((split_out, d), F32), jax.ShapeDtypeStruct((m - split_out, d), F32)]
                 if two_out else [jax.ShapeDtypeStruct((m, d), F32)])
    if emit_bf16:
        out_specs.append(row)
        out_shape.append(jax.ShapeDtypeStruct((m, d), BF16))
    if emit_transposed:
        out_specs.append(col)
        out_shape.append(jax.ShapeDtypeStruct((d, m), BF16))
    kern = functools.partial(_ln_kernel, alpha=alpha, y_transposed=y_transposed, n_head=n_head, two_in=two_in,
                             two_out=two_out, emit_bf16=emit_bf16, emit_transposed=emit_transposed)
    return pl.pallas_call(
        kern,
        grid=(m // tm,),
        in_specs=([head, tail] if two_in else [row]) + [col if y_transposed else row, vec, vec],
        out_specs=out_specs,
        out_shape=out_shape,
        compiler_params=_params("arbitrary"),
        name="deepnorm_ln",
    )(*xs, y, g.reshape(1, d), b.reshape(1, d))


def _gla_kernel(*refs, c, nseq, heads, scale, has_init, n_prev):
    refs = list(refs)
    q_ref, k_ref, v_ref, r_ref, alr_ref, wa2_ref, ba_ref, g_ref = refs[:8]
    rest = refs[8:]
    s0_ref = rest.pop(0) if has_init else None
    o_ref, s_ref = rest[n_prev:]
    dk = q_ref.shape[-1] // heads
    dv = v_ref.shape[-1] // heads
    kpad = max(c, LANES)
    qpad = max(c, 16)

    @pl.when(pl.program_id(1) == 0)
    def _():
        if has_init:
            s_ref[...] = s0_ref[...]
        else:
            s_ref[...] = jnp.zeros_like(s_ref)

    row = lax.broadcasted_iota(jnp.int32, (c, c), 0)
    col = lax.broadcasted_iota(jnp.int32, (c, c), 1)
    tri_cc = (row >= col).astype(F32)
    rowp = lax.broadcasted_iota(jnp.int32, (qpad, kpad), 0)
    colp = lax.broadcasted_iota(jnp.int32, (qpad, kpad), 1)
    causal = rowp >= colp
    eye = lax.broadcasted_iota(jnp.int32, (dk, dk), 0) == lax.broadcasted_iota(jnp.int32, (dk, dk), 1)

    def pad_rows(x, rows):
        if x.shape[0] == rows:
            return x
        return jnp.concatenate([x, jnp.zeros((rows - x.shape[0], x.shape[1]), x.dtype)], axis=0)

    outs = []
    for s in range(nseq):
        rows = slice(s * c, (s + 1) * c)
        z = jnp.dot(alr_ref[rows, :], wa2_ref[...], precision=HIGHEST, preferred_element_type=F32) + ba_ref[...]
        log_a = (jnp.minimum(z, 0.0) - jnp.log1p(jnp.exp(-jnp.abs(z)))) * (1.0 / GLA_TAU)
        if c >= LANES // 2:
            b = jnp.dot(tri_cc, log_a, precision=HIGHEST, preferred_element_type=F32)
        else:
            ridx = lax.broadcasted_iota(jnp.int32, log_a.shape, 0)
            b = jnp.zeros_like(log_a)
            for t in range(c):
                b = b + jnp.where(ridx >= t, log_a[t:t + 1, :], 0.0)
        b_last = b[c - 1:c, :]
        k = k_ref[rows, :]
        q_dec_all = (q_ref[rows, :] * scale * jnp.exp(b)).astype(BF16)
        k_inv_all = (k * jnp.exp(-b)).astype(BF16)
        k_end_all = (k * jnp.exp(b_last - b)).astype(BF16)
        v_all = v_ref[rows, :].astype(BF16)
        r = r_ref[rows, :]
        gate = r * jax.nn.sigmoid(r)
        o_heads = []
        for h in range(heads):
            kc = slice(h * dk, (h + 1) * dk)
            vc = slice(h * dv, (h + 1) * dv)
            q_dec = pad_rows(q_dec_all[:, kc], qpad)
            k_inv = pad_rows(k_inv_all[:, kc], kpad)
            k_end = pad_rows(k_end_all[:, kc], kpad)
            v_bf = pad_rows(v_all[:, vc], kpad)
            scores = jnp.where(causal, _dot_nt(q_dec, k_inv), 0.0).astype(BF16)
            state = s_ref[s, h]
            o = (_dot(q_dec, state.astype(BF16)) + _dot(scores, v_bf))[:c]
            decay_col = jnp.exp(jnp.sum(jnp.where(eye, b_last[:, kc], 0.0), axis=1, keepdims=True))
            s_ref[s, h] = decay_col * state + lax.dot_general(k_end, v_bf, (((0,), (0,)), ((), ())),
                                                              preferred_element_type=F32)
            o = o * lax.rsqrt(jnp.mean(o * o, axis=-1, keepdims=True) + RMS_EPS) * g_ref[...]
            o_heads.append(o * gate[:, vc])
        outs.append(jnp.concatenate(o_heads, axis=1))
    o_all = outs[0] if nseq == 1 else jnp.concatenate(outs, axis=0)
    o_ref[...] = o_all.astype(o_ref.dtype)


def _gla(proj, wa2, ba, norm_g, s0, layer, s_prev, o_prev, *, depth, row0, n_rows, nb, seq_len, c, nseq, cols):
    heads = GLA_HEADS
    dk_tot = wa2.shape[1]
    dk = dk_tot // heads
    dv = norm_g.shape[0]
    dv_tot = heads * dv
    nchunk = seq_len // c
    assert nseq == 1 or nchunk == 1
    rb = nseq * c
    blk0 = row0 // rb
    q_col, k_col, v_col, r_col, lr_col = cols

    def rows_map(col_blk):
        return lambda b, j: (blk0 + b * nchunk + j, col_blk)

    in_specs = [
        pl.BlockSpec((rb, dk_tot), rows_map(q_col // dk_tot)),
        pl.BlockSpec((rb, dk_tot), rows_map(k_col // dk_tot)),
        pl.BlockSpec((rb, dv_tot), rows_map(v_col // dv_tot)),
        pl.BlockSpec((rb, dv_tot), rows_map(r_col // dv_tot)),
        pl.BlockSpec((rb, LANES), rows_map(lr_col // LANES)),
        pl.BlockSpec((LANES, dk_tot), lambda b, j: (0, 0)),
        pl.BlockSpec((1, dk_tot), lambda b, j: (0, 0)),
        pl.BlockSpec((1, dv), lambda b, j: (0, 0)),
    ]
    args = [proj, proj, proj, proj, proj, wa2, ba.reshape(1, -1), norm_g.reshape(1, dv)]
    state_spec = pl.BlockSpec((None, nseq, heads, dk, dv), lambda b, j: (layer, b, 0, 0, 0))
    if s0 is not None:
        in_specs.append(state_spec)
        args.append(s0)
    aliases = {}
    for prev, out_idx in ((s_prev, 1), (o_prev, 0)):
        if prev is not None:
            aliases[len(args)] = out_idx
            in_specs.append(pl.BlockSpec(memory_space=pl.ANY))
            args.append(prev)
    kern = functools.partial(_gla_kernel, c=c, nseq=nseq, heads=heads, scale=dk ** -0.5,
                             has_init=s0 is not None, n_prev=len(aliases))
    return pl.pallas_call(
        kern,
        grid=(nb // nseq, nchunk),
        in_specs=in_specs,
        out_specs=[pl.BlockSpec((rb, dv_tot), rows_map(0)), state_spec],
        out_shape=[jax.ShapeDtypeStruct((n_rows, dv_tot), BF16),
                   jax.ShapeDtypeStruct((depth, nb, heads, dk, dv), F32)],
        input_output_aliases=aliases,
        compiler_params=_params("parallel", "arbitrary"),
        name="gla",
    )(*args)


def _s5_tables(a_re, a_im, log_dt, b_re, b_im, c_re, c_im, c):
    dt = jnp.exp(log_dt)[:, None]
    mag = jnp.exp(a_re * dt)
    ar, ai = mag * jnp.cos(a_im * dt), mag * jnp.sin(a_im * dt)
    den = a_re * a_re + a_im * a_im
    cr = ((ar - 1.0) * a_re + ai * a_im) / den
    ci = (ai * a_re - (ar - 1.0) * a_im) / den
    bbr = cr[..., None] * b_re - ci[..., None] * b_im
    bbi = cr[..., None] * b_im + ci[..., None] * b_re
    tau = jnp.arange(c + 1, dtype=F32)[:, None, None]
    pmag = jnp.exp(tau * (a_re * dt))
    pw_r, pw_i = pmag * jnp.cos(tau * (a_im * dt)), pmag * jnp.sin(tau * (a_im * dt))
    ca_r = c_re[None] * pw_r[:, :, None, :] - c_im[None] * pw_i[:, :, None, :]
    ca_i = c_re[None] * pw_i[:, :, None, :] + c_im[None] * pw_r[:, :, None, :]
    kern = (jnp.einsum('tgmp,gpn->tgmn', ca_r[:c], bbr, precision=HIGHEST)
            - jnp.einsum('tgmp,gpn->tgmn', ca_i[:c], bbi, precision=HIGHEST))
    g, m, n = kern.shape[1:]
    p = a_re.shape[1]
    gpb = LANES // n
    nb = g // gpb

    def block_diag(x):
        lead, _, r, w = x.shape
        copies = jnp.tile(jnp.eye(w, dtype=F32), (1, gpb))
        wide = jnp.dot(x.reshape(-1, w), copies, precision=HIGHEST, preferred_element_type=BF16)
        keep = jnp.kron(jnp.eye(gpb, dtype=F32), jnp.ones((r, w), F32)) > 0
        return jnp.where(keep, wide.reshape(lead, nb, gpb * r, gpb * w), jnp.zeros((), BF16))

    lag_blocks = block_diag(kern.transpose(0, 1, 3, 2))
    zero_block = jnp.zeros_like(lag_blocks[0])
    toep = jnp.concatenate(
        [jnp.concatenate([lag_blocks[t - s] if t >= s else zero_block for s in range(c)], axis=1)
         for t in range(c)], axis=2)
    back = (c - 1.0) - tau[:c]
    rev_mag = jnp.exp(back * (a_re * dt))
    rev_r, rev_i = rev_mag * jnp.cos(back * (a_im * dt)), rev_mag * jnp.sin(back * (a_im * dt))
    inj_r = rev_r[..., None] * bbr[None] - rev_i[..., None] * bbi[None]
    inj_i = rev_r[..., None] * bbi[None] + rev_i[..., None] * bbr[None]
    inj = jnp.concatenate([block_diag(x.transpose(0, 1, 3, 2)) for x in (inj_r, inj_i)], axis=3)
    inj = inj.transpose(1, 0, 2, 3).reshape(nb, c * LANES, 2 * gpb * p)
    read = jnp.concatenate(
        [jnp.concatenate(list(block_diag(x.transpose(0, 1, 3, 2))), axis=2)
         for x in (ca_r[1:], -ca_i[1:])], axis=1)
    step_r, step_i = pw_r[c].reshape(nb, 1, gpb * p), pw_i[c].reshape(nb, 1, gpb * p)
    step_same = jnp.concatenate([step_r, step_r], axis=-1)
    step_swap = jnp.concatenate([-step_i, step_i], axis=-1)
    return toep, inj, read, step_same, step_swap


def _chunk_tokens(u_ref, u_next_ref, c, shift):
    r = u_ref.shape[0] // c
    return [_lane_shifted(u_ref[pl.ds(s, r, stride=c), :], u_next_ref[pl.ds(s, r, stride=c), :], shift)
            for s in range(c)]


def _chunk_rows(tokens):
    return jnp.concatenate([x.astype(BF16) for x in tokens], axis=1)


def _u_specs(t_tokens, u_col):
    blk = u_col // LANES
    return [pl.BlockSpec((t_tokens, LANES), lambda b: (0, blk + b)),
            pl.BlockSpec((t_tokens, LANES), lambda b: (0, blk + b + 1))]


def _s5_inject_kernel(u_ref, u_next_ref, w_ref, e_ref, *, c, shift):
    e_ref[...] = _dot(_chunk_rows(_chunk_tokens(u_ref, u_next_ref, c, shift)), w_ref[...])


def _s5_inject(proj, u_col, inj, c):
    t_tokens = proj.shape[0]
    nb, k, w = inj.shape
    return pl.pallas_call(
        functools.partial(_s5_inject_kernel, c=c, shift=u_col % LANES),
        grid=(nb,),
        in_specs=_u_specs(t_tokens, u_col) + [pl.BlockSpec((None, k, w), lambda b: (b, 0, 0))],
        out_specs=pl.BlockSpec((None, t_tokens // c, w), lambda b: (b, 0, 0)),
        out_shape=jax.ShapeDtypeStruct((nb, t_tokens // c, w), F32),
        compiler_params=_params("parallel"),
        name="s5_inject",
    )(proj, proj, inj)


def _s5_scan_kernel(e_ref, x0_ref, same_ref, swap_ref, xprev_ref, xfin_p_ref, xfin_s_ref, *, n_seq, n_chunk):
    half = e_ref.shape[-1] // 2
    same = same_ref[...]
    swap = swap_ref[...]

    def advance(x, e):
        x_swapped = jnp.concatenate([x[:, half:], x[:, :half]], axis=1)
        return same * x + swap * x_swapped + e

    def body(j, xs):
        new = []
        for q, x in enumerate(xs):
            row = q * n_chunk + j
            xprev_ref[pl.ds(row, 1), :] = x
            new.append(advance(x, e_ref[pl.ds(row, 1), :]))
        return tuple(new)

    zero = jnp.zeros((1, e_ref.shape[-1]), F32)
    xs = lax.fori_loop(0, n_chunk, body, (zero,) * n_seq)
    xfin_p_ref[...] = jnp.zeros_like(xfin_p_ref)
    for q, x in enumerate(xs):
        xfin_p_ref[q:q + 1, :] = x
    rp = n_seq * n_chunk
    x0 = x0_ref[...]
    xprev_ref[rp:, :] = x0
    xfin_s_ref[...] = advance(x0, e_ref[rp:, :])


def _s5_scan(e, x0_s, same, swap, n_seq, n_chunk):
    nb, r, w = e.shape
    ns = x0_s.shape[1]
    assert r == n_seq * n_chunk + ns
    pad_seq = -(-n_seq // 8) * 8
    vec = pl.BlockSpec((None, 1, w), lambda b: (b, 0, 0))
    return pl.pallas_call(
        functools.partial(_s5_scan_kernel, n_seq=n_seq, n_chunk=n_chunk),
        grid=(nb,),
        in_specs=[pl.BlockSpec((None, r, w), lambda b: (b, 0, 0)),
                  pl.BlockSpec((None, ns, w), lambda b: (b, 0, 0)), vec, vec],
        out_specs=[pl.BlockSpec((None, r, w), lambda b: (b, 0, 0)),
                   pl.BlockSpec((None, pad_seq, w), lambda b: (b, 0, 0)),
                   pl.BlockSpec((None, ns, w), lambda b: (b, 0, 0))],
        out_shape=[jax.ShapeDtypeStruct((nb, r, w), F32), jax.ShapeDtypeStruct((nb, pad_seq, w), F32),
                   jax.ShapeDtypeStruct((nb, ns, w), F32)],
        compiler_params=_params("parallel"),
        name="s5_scan",
    )(e, x0_s, same, swap)


def _s5_out_kernel(u_ref, u_next_ref, xp_ref, toep_ref, read_ref, d_ref, y_ref, *, c, shift):
    r = u_ref.shape[0] // c
    tokens = _chunk_tokens(u_ref, u_next_ref, c, shift)
    y = _dot(_chunk_rows(tokens), toep_ref[...]) + _dot(xp_ref[...].astype(BF16), read_ref[...])
    for t in range(c):
        y_ref[pl.ds(t, r, stride=c), :] = jax.nn.gelu(y[:, t * LANES:(t + 1) * LANES] + d_ref[...] * tokens[t])


def _s5_out(proj, u_col, xprev, toep, read, d_skip, c):
    t_tokens = proj.shape[0]
    nb, r, w = xprev.shape
    k = toep.shape[1]
    return pl.pallas_call(
        functools.partial(_s5_out_kernel, c=c, shift=u_col % LANES),
        grid=(nb,),
        in_specs=_u_specs(t_tokens, u_col) + [
            pl.BlockSpec((None, r, w), lambda b: (b, 0, 0)),
            pl.BlockSpec((None, k, k), lambda b: (b, 0, 0)),
            pl.BlockSpec((None, w, k), lambda b: (b, 0, 0)),
            pl.BlockSpec((1, LANES), lambda b: (0, b))],
        out_specs=pl.BlockSpec((t_tokens, LANES), lambda b: (0, b)),
        out_shape=jax.ShapeDtypeStruct((t_tokens, nb * LANES), F32),
        compiler_params=_params("parallel"),
        name="s5_out",
    )(proj, proj, xprev, toep, read, d_skip.reshape(1, -1))


def _s5_mixer(proj, u_col, tables, d_skip, st_re, st_im, n_prompt_seq, prompt_len):
    toep, inj, read, step_same, step_swap = tables
    c = S5_CHUNK
    nb = toep.shape[0]
    ns, groups, p = st_re.shape
    gpb = groups // nb

    def to_blocks(st):
        return st.reshape(ns, nb, gpb * p).transpose(1, 0, 2)

    def from_blocks(x):
        return x.transpose(1, 0, 2).reshape(x.shape[1], groups, p)

    e = _s5_inject(proj, u_col, inj, c)
    x0_s = jnp.concatenate([to_blocks(st_re), to_blocks(st_im)], axis=-1)
    xprev, xfin_p, xfin_s = _s5_scan(e, x0_s, step_same, step_swap, n_prompt_seq, prompt_len // c)
    y = _s5_out(proj, u_col, xprev, toep, read, d_skip, c)
    half = gpb * p
    xfin_p = xfin_p[:, :n_prompt_seq]
    return (y, (from_blocks(xfin_p[..., :half]), from_blocks(xfin_p[..., half:])),
            (from_blocks(xfin_s[..., :half]), from_blocks(xfin_s[..., half:])))


def _sorting_network(n):
    pairs = []
    p = 1
    while p < n:
        k = p
        while k >= 1:
            for j in range(k % p, n - k, 2 * k):
                for i in range(min(k, n - j - k)):
                    if (i + j) // (2 * p) == (i + j + k) // (2 * p):
                        pairs.append((i + j, i + j + k))
            k //= 2
        p *= 2
    return pairs


def _peer_route_kernel(q_ref, k1_ref, k2_ref, s1_ref, w1_ref, s2_ref, e2_ref, tau_ref):
    half = k1_ref.shape[1]
    neg = jnp.float32(-jnp.inf)

    def top_values(x):
        slabs = [x[r:r + 8] for r in range(0, x.shape[0], 8)]
        for lo, hi in _sorting_network(len(slabs)):
            slabs[lo], slabs[hi] = jnp.maximum(slabs[lo], slabs[hi]), jnp.minimum(slabs[lo], slabs[hi])
        stack_id = lax.broadcasted_iota(jnp.int32, slabs[0].shape, 0)
        vals = []
        for taken in range(PEER_TOPK):
            m = jnp.max(slabs[0], axis=0, keepdims=True)
            vals.append(m)
            depth = min(len(slabs), PEER_TOPK - taken - 1)
            first = jnp.min(jnp.where(slabs[0] == m, stack_id, 8), axis=0, keepdims=True)
            popped = stack_id == first
            slabs = [jnp.where(popped, slabs[k + 1] if k + 1 < len(slabs) else neg, slabs[k])
                     for k in range(depth)]
        return vals

    for h in range(PEER_HEADS):
        q1 = q_ref[:, (2 * h) * half:(2 * h + 1) * half]
        q2 = q_ref[:, (2 * h + 1) * half:(2 * h + 2) * half]
        s1 = lax.dot_general(k1_ref[...], q1, (((1,), (1,)), ((), ())), precision=HIGHEST,
                             preferred_element_type=F32)
        s2 = lax.dot_general(k2_ref[...], q2, (((1,), (1,)), ((), ())), precision=HIGHEST,
                             preferred_element_type=F32)
        a = top_values(s1)
        b = jnp.concatenate(top_values(s2), axis=0)
        b0 = b[0:1]
        cand = [a[i] + b[:PEER_TOPK // (i + 1)] for i in range(PEER_TOPK)]
        n_cand = sum(x.shape[0] for x in cand)
        n_rows = 8 * pl.next_power_of_2(-(-n_cand // 8))
        cand.append(jnp.full((n_rows - n_cand, s1.shape[1]), neg, F32))
        top = top_values(jnp.concatenate(cand, axis=0))
        zsum = top[0] * 0.0
        for m in top:
            zsum = zsum + jnp.exp(m - top[0])
        s1_ref[h] = s1
        s2_ref[h] = s2
        w1_ref[h] = jnp.exp(s1 - a[0]) / zsum
        e2_ref[h] = jnp.exp(s2 - b0)
        tau_ref[h] = top[-1]


def _peer_route(q, key1, key2, tm):
    t_tokens = q.shape[0]
    nk, half = key1.shape
    big = pl.BlockSpec((PEER_HEADS, nk, tm), lambda i: (0, 0, i))
    big_shape = jax.ShapeDtypeStruct((PEER_HEADS, nk, t_tokens), F32)
    return pl.pallas_call(
        _peer_route_kernel,
        grid=(t_tokens // tm,),
        in_specs=[pl.BlockSpec((tm, q.shape[1]), lambda i: (i, 0)),
                  pl.BlockSpec((nk, half), lambda i: (0, 0)),
                  pl.BlockSpec((nk, half), lambda i: (0, 0))],
        out_specs=[big, big, big, big, pl.BlockSpec((PEER_HEADS, 1, tm), lambda i: (0, 0, i))],
        out_shape=[big_shape, big_shape, big_shape, big_shape,
                   jax.ShapeDtypeStruct((PEER_HEADS, 1, t_tokens), F32)],
        compiler_params=_params("parallel"),
        name="peer_route",
    )(q, key1, key2)


def _peer_ffn_kernel(xt_ref, u_ref, v_ref, s1_ref, w1_ref, s2_ref, e2_ref, tau_ref, o_ref, p_ref, h_ref, g_ref,
                     *, ni, n_et):
    step = pl.program_id(1)
    et = jnp.minimum(step, n_et - 1)
    slot = step % 2

    @pl.when(step == 0)
    def _():
        o_ref[...] = jnp.zeros_like(o_ref)
        p_ref[...] = jnp.zeros_like(p_ref)

    nk = s2_ref.shape[1]
    te, tm = h_ref.shape
    d = o_ref.shape[0]
    chunks = [(ii, tc) for ii in range(ni) for tc in range(tm // LANES)]

    def block(ii, tc):
        return slice(ii * nk, (ii + 1) * nk), slice(tc * LANES, (tc + 1) * LANES)

    def gate_chunk(ii, tc):
        rows, lanes = block(ii, tc)
        i = et * ni + ii
        g = None
        for h in range(PEER_HEADS):
            s1_row = s1_ref[h, pl.ds(i, 1), :][:, lanes]
            w1_row = w1_ref[h, pl.ds(i, 1), :][:, lanes]
            hit = (s1_row + s2_ref[h, :, lanes]) >= tau_ref[h, :, lanes]
            term = jnp.where(hit, e2_ref[h, :, lanes] * w1_row, 0.0)
            g = term if g is None else g + term
        g_ref[rows, lanes] = g

    def finish_chunk(ii, tc):
        rows, lanes = block(ii, tc)
        p_ref[slot, rows, lanes] = (g_ref[rows, lanes] * jax.nn.gelu(h_ref[rows, lanes])).astype(BF16)

    quarter = len(chunks) // 4
    h_ref[...] = _dot(u_ref[...], xt_ref[...])
    for c in chunks[:2 * quarter]:
        gate_chunk(*c)
    p_prev = p_ref[1 - slot]
    vector_work = [[(gate_chunk, c) for c in chunks[2 * quarter:3 * quarter]],
                   [(gate_chunk, c) for c in chunks[3 * quarter:]],
                   [(finish_chunk, c) for c in chunks[:2 * quarter]],
                   [(finish_chunk, c) for c in chunks[2 * quarter:]]]
    dq = d // len(vector_work)
    for k, work in enumerate(vector_work):
        for fn, c in work:
            fn(*c)
        rows = slice(k * dq, (k + 1) * dq)
        o_ref[rows, :] += lax.dot_general(v_ref[:, rows], p_prev, (((0,), (0,)), ((), ())),
                                          preferred_element_type=F32)


def _peer_ffn(xt_bf, u_tab, v_tab, layer, route, tm, te):
    s1, w1, s2, e2, tau = route
    d, t_tokens = xt_bf.shape
    n_exp = u_tab.shape[1]
    nk = s2.shape[1]
    ni = te // nk
    n_et = n_exp // te
    once = pl.Buffered(1)
    tok = pl.BlockSpec((PEER_HEADS, nk, tm), lambda i, j: (0, 0, i), pipeline_mode=once)
    return pl.pallas_call(
        functools.partial(_peer_ffn_kernel, ni=ni, n_et=n_et),
        grid=(t_tokens // tm, n_et + 1),
        in_specs=[pl.BlockSpec((d, tm), lambda i, j: (0, i), pipeline_mode=once),
                  pl.BlockSpec((None, te, d), lambda i, j: (layer, jnp.minimum(j, n_et - 1), 0)),
                  pl.BlockSpec((None, te, d), lambda i, j: (layer, jnp.maximum(j - 1, 0), 0)),
                  tok, tok, tok, tok,
                  pl.BlockSpec((PEER_HEADS, 1, tm), lambda i, j: (0, 0, i), pipeline_mode=once)],
        out_specs=pl.BlockSpec((d, tm), lambda i, j: (0, i)),
        out_shape=jax.ShapeDtypeStruct((d, t_tokens), F32),
        scratch_shapes=[pltpu.VMEM((2, te, tm), BF16), pltpu.VMEM((te, tm), F32), pltpu.VMEM((te, tm), F32)],
        compiler_params=_params("parallel", "arbitrary"),
        name="peer_ffn",
    )(xt_bf, u_tab, v_tab, s1, w1, s2, e2, tau)


def kernel(x_prompt, x_sample, state_gla, state_s5_re, state_s5_im, ln1_g, ln1_b, w_in, gla_w_a2, gla_b_a, gla_norm_g, s5_a_re, s5_a_im, s5_log_dt, s5_b_re, s5_b_im, s5_c_re, s5_c_im, s5_d, s5_w_glu, s5_b_glu, w_branch, w_out, ln2_g, ln2_b, peer_w_q, peer_key1, peer_key2, peer_u, peer_v):
    depth = w_in.shape[0]
    pb, plen, d = x_prompt.shape
    sb, slen, _ = x_sample.shape
    tp, ts = pb * plen, sb * slen
    t_all = tp + ts
    dk_tot = gla_w_a2.shape[2]
    dv_tot = GLA_HEADS * gla_norm_g.shape[1]
    s5_w = s5_d.shape[1]
    alpha = (2 * depth) ** 0.25
    widths = (dk_tot, dk_tot, dv_tot, dv_tot, GLA_RANK, s5_w, d, d)
    off = [0]
    for wdt in widths:
        off.append(off[-1] + wdt)
    lr0, lr1 = off[4], off[5]
    x = (x_prompt.reshape(tp, d), x_sample.reshape(ts, d))
    x_bf = jnp.concatenate([x[0].astype(BF16), x[1].astype(BF16)], axis=0)
    proj_tile = 4 * LANES
    proj_main = off[-1] // proj_tile * proj_tile
    w_in_t = w_in.transpose(0, 2, 1)
    wa2 = jnp.pad(gla_w_a2, ((0, 0), (0, LANES - GLA_RANK), (0, 0)))
    u_bf = peer_u.astype(BF16)
    v_bf = peer_v.astype(BF16)
    cols = (off[0], off[1], off[2], off[3], lr0)
    u_col, zg_col, zs_col = off[5], off[6], off[7]
    gla_p = gla_s = None
    re_p, im_p, re_s, im_s = [], [], [], []
    for l in range(depth):
        proj = _mm(x_bf, w_in_t, l, 1024, proj_tile, w_transposed=True, n_out=proj_main + LANES)
        proj = _mm_tail(x_bf, w_in_t, l, proj, 1024, proj_main)

        o_gla, gla_p = _gla(proj, wa2[l], gla_b_a[l], gla_norm_g[l], None, l, gla_p, None, depth=depth, row0=0,
                            n_rows=t_all, nb=pb, seq_len=plen, c=min(GLA_CHUNK, plen), nseq=1, cols=cols)
        o_gla, gla_s = _gla(proj, wa2[l], gla_b_a[l], gla_norm_g[l], state_gla, l, gla_s, o_gla, depth=depth,
                            row0=tp, n_rows=t_all, nb=sb, seq_len=slen, c=min(GLA_CHUNK, slen), nseq=2, cols=cols)

        tables = _s5_tables(s5_a_re[l], s5_a_im[l], s5_log_dt[l], s5_b_re[l], s5_b_im[l], s5_c_re[l], s5_c_im[l],
                            S5_CHUNK)
        y_s5, (r_p, i_p), (r_s, i_s) = _s5_mixer(proj, u_col, tables, s5_d[l], state_s5_re[l], state_s5_im[l],
                                                 pb, plen)
        o_s5 = _glu(y_s5, s5_w_glu, l, s5_b_glu[l], 1024, 1024)

        merged = _branch_merge(o_gla, o_s5, w_branch, l, proj, zg_col, zs_col, 1024, 512)
        mix = _mm(merged, w_out, l, 1024, proj_tile)
        x, x_bf, xt_bf = _deepnorm_ln(x, mix, ln1_g[l], ln1_b[l], alpha, 256, emit_transposed=True)

        q = _mm(x_bf, peer_w_q, l, 1024, proj_tile)
        route = _peer_route(q, peer_key1[l], peer_key2[l], 256)
        ffn_t = _peer_ffn(xt_bf, u_bf, v_bf, l, route, 512, 512)
        if l + 1 < depth:
            x, x_bf = _deepnorm_ln(x, ffn_t, ln2_g[l], ln2_b[l], alpha, 256, y_transposed=True)
        else:
            y_p, y_s = _deepnorm_ln(x, ffn_t, ln2_g[l], ln2_b[l], alpha, 256, y_transposed=True,
                                    emit_bf16=False, split_out=tp)

        re_p.append(r_p)
        im_p.append(i_p)
        re_s.append(r_s)
        im_s.append(i_s)

    return (y_p.reshape(pb, plen, d), y_s.reshape(sb, slen, d), gla_p, jnp.stack(re_p), jnp.stack(im_p),
            gla_s, jnp.stack(re_s), jnp.stack(im_s))
```

```python
import functools

import jax
import jax.numpy as jnp
from jax import lax
from jax.experimental import pallas as pl
from jax.experimental.pallas import tpu as pltpu

F32 = jnp.float32
BF16 = jnp.bfloat16
HIGHEST = lax.Precision.HIGHEST

V7X_VMEM_BYTES = 64 * 1024 * 1024
VMEM_LIMIT_BYTES = V7X_VMEM_BYTES - 8 * 1024 * 1024
LANES = 128

GLA_HEADS = 4
GLA_RANK = 16
GLA_TAU = 16.0
GLA_CHUNK = 64
RMS_EPS = 1e-6
S5_CHUNK = 8
PEER_HEADS = 8
PEER_TOPK = 16
LN_EPS = 1e-5


def _params(*semantics):
    return pltpu.CompilerParams(dimension_semantics=semantics, vmem_limit_bytes=VMEM_LIMIT_BYTES)


def _dot(a, b):
    return jnp.dot(a, b, preferred_element_type=F32)


def _dot_nt(a, b):
    return lax.dot_general(a, b, (((1,), (1,)), ((), ())), preferred_element_type=F32)


def _mm_kernel(a_ref, b_ref, o_ref):
    o_ref[...] = _dot(a_ref[...], b_ref[...].astype(BF16)).astype(o_ref.dtype)


def _layer_cols(layer, tn, row0=0):
    def spec(rows):
        return pl.BlockSpec((None, rows, tn), lambda i, j: (layer, row0, j))
    return spec


def _mm_nt_kernel(a_ref, bt_ref, o_ref):
    o_ref[...] = _dot_nt(a_ref[...], bt_ref[...].astype(BF16)).astype(o_ref.dtype)


def _mm(a, w, layer, tm, tn, out_dtype=F32, w_transposed=False, n_out=None):
    m, k = a.shape
    n = (w.shape[1] if w_transposed else w.shape[2]) // tn * tn
    w_spec = (pl.BlockSpec((None, tn, k), lambda i, j: (layer, j, 0)) if w_transposed
              else _layer_cols(layer, tn)(k))
    return pl.pallas_call(
        _mm_nt_kernel if w_transposed else _mm_kernel,
        grid=(m // tm, n // tn),
        in_specs=[pl.BlockSpec((tm, k), lambda i, j: (i, 0)), w_spec],
        out_specs=pl.BlockSpec((tm, tn), lambda i, j: (i, j)),
        out_shape=jax.ShapeDtypeStruct((m, n_out or n), out_dtype),
        compiler_params=_params("parallel", "parallel"),
        name="mm",
    )(a, w)


def _mm_tail_kernel(a_ref, bt_ref, prev_ref, o_ref):
    del prev_ref
    tail = _dot_nt(a_ref[...], bt_ref[...].astype(BF16))
    o_ref[...] = jnp.concatenate([tail, jnp.zeros((tail.shape[0], o_ref.shape[1] - tail.shape[1]), F32)], axis=1)


def _mm_tail(a, w_t, layer, out, tm, col0):
    m, k = a.shape
    rows = w_t.shape[1] - col0
    assert 0 < rows < LANES and rows % 8 == 0 and col0 % rows == 0 and col0 % LANES == 0
    return pl.pallas_call(
        _mm_tail_kernel,
        grid=(m // tm,),
        in_specs=[pl.BlockSpec((tm, k), lambda i: (i, 0)),
                  pl.BlockSpec((None, rows, k), lambda i: (layer, col0 // rows, 0)),
                  pl.BlockSpec(memory_space=pl.ANY)],
        out_specs=pl.BlockSpec((tm, LANES), lambda i: (i, col0 // LANES)),
        out_shape=jax.ShapeDtypeStruct(out.shape, out.dtype),
        input_output_aliases={2: 0},
        compiler_params=_params("parallel"),
        name="mm_tail",
    )(a, w_t, out)


def _glu_kernel(y_ref, w_ref, b_ref, ytile_ref, o_ref):
    acc = _dot(y_ref[...].astype(BF16), w_ref[...].astype(BF16)) + b_ref[...]
    o_ref[...] = (ytile_ref[...] * jax.nn.sigmoid(acc)).astype(o_ref.dtype)


def _glu(y, w, layer, b, tm, tn):
    m, k = y.shape
    n = w.shape[2]
    return pl.pallas_call(
        _glu_kernel,
        grid=(m // tm, n // tn),
        in_specs=[pl.BlockSpec((tm, k), lambda i, j: (i, 0)),
                  _layer_cols(layer, tn)(k),
                  pl.BlockSpec((1, tn), lambda i, j: (0, j)),
                  pl.BlockSpec((tm, tn), lambda i, j: (i, j))],
        out_specs=pl.BlockSpec((tm, tn), lambda i, j: (i, j)),
        out_shape=jax.ShapeDtypeStruct((m, n), BF16),
        compiler_params=_params("parallel", "parallel"),
        name="s5_glu",
    )(y, w, b.reshape(1, n), y)


def _lane_shifted(a, b, shift):
    if shift == 0:
        return a
    return jnp.concatenate([a[:, shift:], b[:, :shift]], axis=1)


def _branch_kernel(og_ref, os_ref, w1_ref, w2_ref, zg_ref, zg_next_ref, zs_ref, zs_next_ref, o_ref, *, shift):
    m1 = _dot(og_ref[...], w1_ref[...].astype(BF16))
    m2 = _dot(os_ref[...], w2_ref[...].astype(BF16))
    zg = _lane_shifted(zg_ref[...], zg_next_ref[...], shift)
    zs = _lane_shifted(zs_ref[...], zs_next_ref[...], shift)
    o_ref[...] = (jax.nn.sigmoid(zg) * m1 + jax.nn.sigmoid(zs) * m2).astype(o_ref.dtype)


def _branch_merge(o_gla, o_s5, w, layer, proj, zg_col, zs_col, tm, tn):
    m, k = o_gla.shape
    assert o_s5.shape[1] == k and w.shape[1] == 2 * k
    n = w.shape[2]
    shift = zg_col % LANES
    assert zs_col % LANES == shift and (zg_col - shift) % tn == 0 and (zs_col - shift) % tn == 0

    def gate_specs(col):
        blk, nxt, per = (col - shift) // tn, (col - shift) // LANES, tn // LANES
        return [pl.BlockSpec((tm, tn), lambda i, j: (i, blk + j)),
                pl.BlockSpec((tm, LANES), lambda i, j: (i, nxt + per * (j + 1)))]

    return pl.pallas_call(
        functools.partial(_branch_kernel, shift=shift),
        grid=(m // tm, n // tn),
        in_specs=[pl.BlockSpec((tm, k), lambda i, j: (i, 0)),
                  pl.BlockSpec((tm, k), lambda i, j: (i, 0)),
                  _layer_cols(layer, tn, 0)(k),
                  _layer_cols(layer, tn, 1)(k)] + gate_specs(zg_col) + gate_specs(zs_col),
        out_specs=pl.BlockSpec((tm, tn), lambda i, j: (i, j)),
        out_shape=jax.ShapeDtypeStruct((m, n), BF16),
        compiler_params=_params("parallel", "parallel"),
        name="branch_merge",
    )(o_gla, o_s5, w, w, proj, proj, proj, proj)


def _ln_kernel(*refs, alpha, y_transposed, n_head, two_in, two_out, emit_bf16, emit_transposed):
    refs = list(refs)
    step = pl.program_id(0)
    if two_in:
        x = jnp.where(step < n_head, refs[0][...], refs[1][...])
        refs = refs[2:]
    else:
        x = refs[0][...]
        refs = refs[1:]
    y_ref, g_ref, b_ref = refs[:3]
    outs = refs[3:]
    y = y_ref[...].T if y_transposed else y_ref[...]
    h = alpha * x + y
    mu = jnp.mean(h, axis=-1, keepdims=True)
    d = h - mu
    var = jnp.mean(d * d, axis=-1, keepdims=True)
    o = d * lax.rsqrt(var + LN_EPS) * g_ref[...] + b_ref[...]
    if two_out:
        head_ref, tail_ref = outs[:2]
        outs = outs[2:]

        @pl.when(step < n_head)
        def _():
            head_ref[...] = o

        @pl.when(step >= n_head)
        def _():
            tail_ref[...] = o
    else:
        outs[0][...] = o
        outs = outs[1:]
    if emit_bf16:
        outs[0][...] = o.astype(BF16)
        outs = outs[1:]
    if emit_transposed:
        outs[0][...] = o.T.astype(BF16)


def _deepnorm_ln(x, y, g, b, alpha, tm, y_transposed=False, emit_bf16=True, emit_transposed=False, split_out=None):
    xs = x if isinstance(x, (tuple, list)) else (x,)
    d = xs[0].shape[1]
    m = sum(a.shape[0] for a in xs)
    two_in, two_out = len(xs) == 2, split_out is not None
    n_head = (xs[0].shape[0] if two_in else split_out if two_out else m) // tm
    assert not (two_in and two_out) or xs[0].shape[0] == split_out
    row = pl.BlockSpec((tm, d), lambda i: (i, 0))
    head = pl.BlockSpec((tm, d), lambda i: (jnp.minimum(i, n_head - 1), 0))
    tail = pl.BlockSpec((tm, d), lambda i: (jnp.maximum(i - n_head, 0), 0))
    col = pl.BlockSpec((d, tm), lambda i: (0, i))
    vec = pl.BlockSpec((1, d), lambda i: (0, 0))
    out_specs = [head, tail] if two_out else [row]
    out_shape = ([jax.ShapeDtypeStruct((split_out, d), F32), jax.ShapeDtypeStruct((m - split_out, d), F32)]
                 if two_out else [jax.ShapeDtypeStruct((m, d), F32)])
    if emit_bf16:
        out_specs.append(row)
        out_shape.append(jax.ShapeDtypeStruct((m, d), BF16))
    if emit_transposed:
        out_specs.append(col)
        out_shape.append(jax.ShapeDtypeStruct((d, m), BF16))
    kern = functools.partial(_ln_kernel, alpha=alpha, y_transposed=y_transposed, n_head=n_head, two_in=two_in,
                             two_out=two_out, emit_bf16=emit_bf16, emit_transposed=emit_transposed)
    return pl.pallas_call(
        kern,
        grid=(m // tm,),
        in_specs=([head, tail] if two_in else [row]) + [col if y_transposed else row, vec, vec],
        out_specs=out_specs,
        out_shape=out_shape,
        compiler_params=_params("arbitrary"),
        name="deepnorm_ln",
    )(*xs, y, g.reshape(1, d), b.reshape(1, d))


def _gla_kernel(*refs, c, nseq, heads, scale, has_init, n_prev):
    refs = list(refs)
    q_ref, k_ref, v_ref, r_ref, alr_ref, wa2_ref, ba_ref, g_ref = refs[:8]
    rest = refs[8:]
    s0_ref = rest.pop(0) if has_init else None
    o_ref, s_ref = rest[n_prev:]
    dk = q_ref.shape[-1] // heads
    dv = v_ref.shape[-1] // heads
    kpad = max(c, LANES)
    qpad = max(c, 16)

    @pl.when(pl.program_id(1) == 0)
    def _():
        if has_init:
            s_ref[...] = s0_ref[...]
        else:
            s_ref[...] = jnp.zeros_like(s_ref)

    row = lax.broadcasted_iota(jnp.int32, (c, c), 0)
    col = lax.broadcasted_iota(jnp.int32, (c, c), 1)
    tri_cc = (row >= col).astype(F32)
    rowp = lax.broadcasted_iota(jnp.int32, (qpad, kpad), 0)
    colp = lax.broadcasted_iota(jnp.int32, (qpad, kpad), 1)
    causal = rowp >= colp
    eye = lax.broadcasted_iota(jnp.int32, (dk, dk), 0) == lax.broadcasted_iota(jnp.int32, (dk, dk), 1)

    def pad_rows(x, rows):
        if x.shape[0] == rows:
            return x
        return jnp.concatenate([x, jnp.zeros((rows - x.shape[0], x.shape[1]), x.dtype)], axis=0)

    outs = []
    for s in range(nseq):
        rows = slice(s * c, (s + 1) * c)
        z = jnp.dot(alr_ref[rows, :], wa2_ref[...], precision=HIGHEST, preferred_element_type=F32) + ba_ref[...]
        log_a = (jnp.minimum(z, 0.0) - jnp.log1p(jnp.exp(-jnp.abs(z)))) * (1.0 / GLA_TAU)
        if c >= LANES // 2:
            b = jnp.dot(tri_cc, log_a, precision=HIGHEST, preferred_element_type=F32)
        else:
            ridx = lax.broadcasted_iota(jnp.int32, log_a.shape, 0)
            b = jnp.zeros_like(log_a)
            for t in range(c):
                b = b + jnp.where(ridx >= t, log_a[t:t + 1, :], 0.0)
        b_last = b[c - 1:c, :]
        k = k_ref[rows, :]
        q_dec_all = (q_ref[rows, :] * scale * jnp.exp(b)).astype(BF16)
        k_inv_all = (k * jnp.exp(-b)).astype(BF16)
        k_end_all = (k * jnp.exp(b_last - b)).astype(BF16)
        v_all = v_ref[rows, :].astype(BF16)
        r = r_ref[rows, :]
        gate = r * jax.nn.sigmoid(r)
        o_heads = []
        for h in range(heads):
            kc = slice(h * dk, (h + 1) * dk)
            vc = slice(h * dv, (h + 1) * dv)
            q_dec = pad_rows(q_dec_all[:, kc], qpad)
            k_inv = pad_rows(k_inv_all[:, kc], kpad)
            k_end = pad_rows(k_end_all[:, kc], kpad)
            v_bf = pad_rows(v_all[:, vc], kpad)
            scores = jnp.where(causal, _dot_nt(q_dec, k_inv), 0.0).astype(BF16)
            state = s_ref[s, h]
            o = (_dot(q_dec, state.astype(BF16)) + _dot(scores, v_bf))[:c]
            decay_col = jnp.exp(jnp.sum(jnp.where(eye, b_last[:, kc], 0.0), axis=1, keepdims=True))
            s_ref[s, h] = decay_col * state + lax.dot_general(k_end, v_bf, (((0,), (0,)), ((), ())),
                                                              preferred_element_type=F32)
            o = o * lax.rsqrt(jnp.mean(o * o, axis=-1, keepdims=True) + RMS_EPS) * g_ref[...]
            o_heads.append(o * gate[:, vc])
        outs.append(jnp.concatenate(o_heads, axis=1))
    o_all = outs[0] if nseq == 1 else jnp.concatenate(outs, axis=0)
    o_ref[...] = o_all.astype(o_ref.dtype)


def _gla(proj, wa2, ba, norm_g, s0, layer, s_prev, o_prev, *, depth, row0, n_rows, nb, seq_len, c, nseq, cols):
    heads = GLA_HEADS
    dk_tot = wa2.shape[1]
    dk = dk_tot // heads
    dv = norm_g.shape[0]
    dv_tot = heads * dv
    nchunk = seq_len // c
    assert nseq == 1 or nchunk == 1
    rb = nseq * c
    blk0 = row0 // rb
    q_col, k_col, v_col, r_col, lr_col = cols

    def rows_map(col_blk):
        return lambda b, j: (blk0 + b * nchunk + j, col_blk)

    in_specs = [
        pl.BlockSpec((rb, dk_tot), rows_map(q_col // dk_tot)),
        pl.BlockSpec((rb, dk_tot), rows_map(k_col // dk_tot)),
        pl.BlockSpec((rb, dv_tot), rows_map(v_col // dv_tot)),
        pl.BlockSpec((rb, dv_tot), rows_map(r_col // dv_tot)),
        pl.BlockSpec((rb, LANES), rows_map(lr_col // LANES)),
        pl.BlockSpec((LANES, dk_tot), lambda b, j: (0, 0)),
        pl.BlockSpec((1, dk_tot), lambda b, j: (0, 0)),
        pl.BlockSpec((1, dv), lambda b, j: (0, 0)),
    ]
    args = [proj, proj, proj, proj, proj, wa2, ba.reshape(1, -1), norm_g.reshape(1, dv)]
    state_spec = pl.BlockSpec((None, nseq, heads, dk, dv), lambda b, j: (layer, b, 0, 0, 0))
    if s0 is not None:
        in_specs.append(state_spec)
        args.append(s0)
    aliases = {}
    for prev, out_idx in ((s_prev, 1), (o_prev, 0)):
        if prev is not None:
            aliases[len(args)] = out_idx
            in_specs.append(pl.BlockSpec(memory_space=pl.ANY))
            args.append(prev)
    kern = functools.partial(_gla_kernel, c=c, nseq=nseq, heads=heads, scale=dk ** -0.5,
                             has_init=s0 is not None, n_prev=len(aliases))
    return pl.pallas_call(
        kern,
        grid=(nb // nseq, nchunk),
        in_specs=in_specs,
        out_specs=[pl.BlockSpec((rb, dv_tot), rows_map(0)), state_spec],
        out_shape=[jax.ShapeDtypeStruct((n_rows, dv_tot), BF16),
                   jax.ShapeDtypeStruct((depth, nb, heads, dk, dv), F32)],
        input_output_aliases=aliases,
        compiler_params=_params("parallel", "arbitrary"),
        name="gla",
    )(*args)


def _s5_tables(a_re, a_im, log_dt, b_re, b_im, c_re, c_im, c):
    dt = jnp.exp(log_dt)[:, None]
    mag = jnp.exp(a_re * dt)
    ar, ai = mag * jnp.cos(a_im * dt), mag * jnp.sin(a_im * dt)
    den = a_re * a_re + a_im * a_im
    cr = ((ar - 1.0) * a_re + ai * a_im) / den
    ci = (ai * a_re - (ar - 1.0) * a_im) / den
    bbr = cr[..., None] * b_re - ci[..., None] * b_im
    bbi = cr[..., None] * b_im + ci[..., None] * b_re
    tau = jnp.arange(c + 1, dtype=F32)[:, None, None]
    pmag = jnp.exp(tau * (a_re * dt))
    pw_r, pw_i = pmag * jnp.cos(tau * (a_im * dt)), pmag * jnp.sin(tau * (a_im * dt))
    ca_r = c_re[None] * pw_r[:, :, None, :] - c_im[None] * pw_i[:, :, None, :]
    ca_i = c_re[None] * pw_i[:, :, None, :] + c_im[None] * pw_r[:, :, None, :]
    kern = (jnp.einsum('tgmp,gpn->tgmn', ca_r[:c], bbr, precision=HIGHEST)
            - jnp.einsum('tgmp,gpn->tgmn', ca_i[:c], bbi, precision=HIGHEST))
    g, m, n = kern.shape[1:]
    p = a_re.shape[1]
    gpb = LANES // n
    nb = g // gpb

    def block_diag(x):
        lead, _, r, w = x.shape
        copies = jnp.tile(jnp.eye(w, dtype=F32), (1, gpb))
        wide = jnp.dot(x.reshape(-1, w), copies, precision=HIGHEST).reshape(lead, nb, gpb * r, gpb * w)
        return wide * jnp.kron(jnp.eye(gpb, dtype=F32), jnp.ones((r, w), F32))

    lag_blocks = block_diag(kern.transpose(0, 1, 3, 2))
    zero_block = jnp.zeros_like(lag_blocks[0])
    toep = jnp.concatenate(
        [jnp.concatenate([lag_blocks[t - s] if t >= s else zero_block for s in range(c)], axis=1)
         for t in range(c)], axis=2)
    back = (c - 1.0) - tau[:c]
    rev_mag = jnp.exp(back * (a_re * dt))
    rev_r, rev_i = rev_mag * jnp.cos(back * (a_im * dt)), rev_mag * jnp.sin(back * (a_im * dt))
    inj_r = rev_r[..., None] * bbr[None] - rev_i[..., None] * bbi[None]
    inj_i = rev_r[..., None] * bbi[None] + rev_i[..., None] * bbr[None]
    inj = jnp.concatenate([block_diag(x.transpose(0, 1, 3, 2)) for x in (inj_r, inj_i)], axis=3)
    inj = inj.transpose(1, 0, 2, 3).reshape(nb, c * LANES, 2 * gpb * p)
    read = jnp.concatenate(
        [jnp.concatenate(list(block_diag(x.transpose(0, 1, 3, 2))), axis=2)
         for x in (ca_r[1:], -ca_i[1:])], axis=1)
    step_r, step_i = pw_r[c].reshape(nb, 1, gpb * p), pw_i[c].reshape(nb, 1, gpb * p)
    step_same = jnp.concatenate([step_r, step_r], axis=-1)
    step_swap = jnp.concatenate([-step_i, step_i], axis=-1)
    return toep.astype(BF16), inj.astype(BF16), read.astype(BF16), step_same, step_swap


def _chunk_tokens(u_ref, u_next_ref, c, shift):
    r = u_ref.shape[0] // c
    return [_lane_shifted(u_ref[pl.ds(s, r, stride=c), :], u_next_ref[pl.ds(s, r, stride=c), :], shift)
            for s in range(c)]


def _chunk_rows(tokens):
    return jnp.concatenate([x.astype(BF16) for x in tokens], axis=1)


def _u_specs(t_tokens, u_col):
    blk = u_col // LANES
    return [pl.BlockSpec((t_tokens, LANES), lambda b: (0, blk + b)),
            pl.BlockSpec((t_tokens, LANES), lambda b: (0, blk + b + 1))]


def _s5_inject_kernel(u_ref, u_next_ref, w_ref, e_ref, *, c, shift):
    e_ref[...] = _dot(_chunk_rows(_chunk_tokens(u_ref, u_next_ref, c, shift)), w_ref[...])


def _s5_inject(proj, u_col, inj, c):
    t_tokens = proj.shape[0]
    nb, k, w = inj.shape
    return pl.pallas_call(
        functools.partial(_s5_inject_kernel, c=c, shift=u_col % LANES),
        grid=(nb,),
        in_specs=_u_specs(t_tokens, u_col) + [pl.BlockSpec((None, k, w), lambda b: (b, 0, 0))],
        out_specs=pl.BlockSpec((None, t_tokens // c, w), lambda b: (b, 0, 0)),
        out_shape=jax.ShapeDtypeStruct((nb, t_tokens // c, w), F32),
        compiler_params=_params("parallel"),
        name="s5_inject",
    )(proj, proj, inj)


def _s5_scan_kernel(e_ref, x0_ref, same_ref, swap_ref, xprev_ref, xfin_p_ref, xfin_s_ref, *, n_seq, n_chunk):
    half = e_ref.shape[-1] // 2
    same = same_ref[...]
    swap = swap_ref[...]

    def advance(x, e):
        x_swapped = jnp.concatenate([x[:, half:], x[:, :half]], axis=1)
        return same * x + swap * x_swapped + e

    def body(j, xs):
        new = []
        for q, x in enumerate(xs):
            row = q * n_chunk + j
            xprev_ref[pl.ds(row, 1), :] = x
            new.append(advance(x, e_ref[pl.ds(row, 1), :]))
        return tuple(new)

    zero = jnp.zeros((1, e_ref.shape[-1]), F32)
    xs = lax.fori_loop(0, n_chunk, body, (zero,) * n_seq)
    xfin_p_ref[...] = jnp.zeros_like(xfin_p_ref)
    for q, x in enumerate(xs):
        xfin_p_ref[q:q + 1, :] = x
    rp = n_seq * n_chunk
    x0 = x0_ref[...]
    xprev_ref[rp:, :] = x0
    xfin_s_ref[...] = advance(x0, e_ref[rp:, :])


def _s5_scan(e, x0_s, same, swap, n_seq, n_chunk):
    nb, r, w = e.shape
    ns = x0_s.shape[1]
    assert r == n_seq * n_chunk + ns
    pad_seq = -(-n_seq // 8) * 8
    vec = pl.BlockSpec((None, 1, w), lambda b: (b, 0, 0))
    return pl.pallas_call(
        functools.partial(_s5_scan_kernel, n_seq=n_seq, n_chunk=n_chunk),
        grid=(nb,),
        in_specs=[pl.BlockSpec((None, r, w), lambda b: (b, 0, 0)),
                  pl.BlockSpec((None, ns, w), lambda b: (b, 0, 0)), vec, vec],
        out_specs=[pl.BlockSpec((None, r, w), lambda b: (b, 0, 0)),
                   pl.BlockSpec((None, pad_seq, w), lambda b: (b, 0, 0)),
                   pl.BlockSpec((None, ns, w), lambda b: (b, 0, 0))],
        out_shape=[jax.ShapeDtypeStruct((nb, r, w), F32), jax.ShapeDtypeStruct((nb, pad_seq, w), F32),
                   jax.ShapeDtypeStruct((nb, ns, w), F32)],
        compiler_params=_params("parallel"),
        name="s5_scan",
    )(e, x0_s, same, swap)


def _s5_out_kernel(u_ref, u_next_ref, xp_ref, toep_ref, read_ref, d_ref, y_ref, *, c, shift):
    r = u_ref.shape[0] // c
    tokens = _chunk_tokens(u_ref, u_next_ref, c, shift)
    y = _dot(_chunk_rows(tokens), toep_ref[...]) + _dot(xp_ref[...].astype(BF16), read_ref[...])
    for t in range(c):
        y_ref[pl.ds(t, r, stride=c), :] = jax.nn.gelu(y[:, t * LANES:(t + 1) * LANES] + d_ref[...] * tokens[t])


def _s5_out(proj, u_col, xprev, toep, read, d_skip, c):
    t_tokens = proj.shape[0]
    nb, r, w = xprev.shape
    k = toep.shape[1]
    return pl.pallas_call(
        functools.partial(_s5_out_kernel, c=c, shift=u_col % LANES),
        grid=(nb,),
        in_specs=_u_specs(t_tokens, u_col) + [
            pl.BlockSpec((None, r, w), lambda b: (b, 0, 0)),
            pl.BlockSpec((None, k, k), lambda b: (b, 0, 0)),
            pl.BlockSpec((None, w, k), lambda b: (b, 0, 0)),
            pl.BlockSpec((1, LANES), lambda b: (0, b))],
        out_specs=pl.BlockSpec((t_tokens, LANES), lambda b: (0, b)),
        out_shape=jax.ShapeDtypeStruct((t_tokens, nb * LANES), F32),
        compiler_params=_params("parallel"),
        name="s5_out",
    )(proj, proj, xprev, toep, read, d_skip.reshape(1, -1))


def _s5_mixer(proj, u_col, tables, d_skip, st_re, st_im, n_prompt_seq, prompt_len):
    toep, inj, read, step_same, step_swap = tables
    c = S5_CHUNK
    nb = toep.shape[0]
    ns, groups, p = st_re.shape
    gpb = groups // nb

    def to_blocks(st):
        return st.reshape(ns, nb, gpb * p).transpose(1, 0, 2)

    def from_blocks(x):
        return x.transpose(1, 0, 2).reshape(x.shape[1], groups, p)

    e = _s5_inject(proj, u_col, inj, c)
    x0_s = jnp.concatenate([to_blocks(st_re), to_blocks(st_im)], axis=-1)
    xprev, xfin_p, xfin_s = _s5_scan(e, x0_s, step_same, step_swap, n_prompt_seq, prompt_len // c)
    y = _s5_out(proj, u_col, xprev, toep, read, d_skip, c)
    half = gpb * p
    xfin_p = xfin_p[:, :n_prompt_seq]
    return (y, (from_blocks(xfin_p[..., :half]), from_blocks(xfin_p[..., half:])),
            (from_blocks(xfin_s[..., :half]), from_blocks(xfin_s[..., half:])))


def _sorting_network(n):
    pairs = []
    p = 1
    while p < n:
        k = p
        while k >= 1:
            for j in range(k % p, n - k, 2 * k):
                for i in range(min(k, n - j - k)):
                    if (i + j) // (2 * p) == (i + j + k) // (2 * p):
                        pairs.append((i + j, i + j + k))
            k //= 2
        p *= 2
    return pairs


def _peer_route_kernel(q_ref, k1_ref, k2_ref, s1_ref, w1_ref, s2_ref, e2_ref, tau_ref):
    half = k1_ref.shape[1]
    neg = jnp.float32(-jnp.inf)

    def top_values(x):
        slabs = [x[r:r + 8] for r in range(0, x.shape[0], 8)]
        for lo, hi in _sorting_network(len(slabs)):
            slabs[lo], slabs[hi] = jnp.maximum(slabs[lo], slabs[hi]), jnp.minimum(slabs[lo], slabs[hi])
        stack_id = lax.broadcasted_iota(jnp.int32, slabs[0].shape, 0)
        vals = []
        for taken in range(PEER_TOPK):
            m = jnp.max(slabs[0], axis=0, keepdims=True)
            vals.append(m)
            depth = min(len(slabs), PEER_TOPK - taken - 1)
            first = jnp.min(jnp.where(slabs[0] == m, stack_id, 8), axis=0, keepdims=True)
            popped = stack_id == first
            slabs = [jnp.where(popped, slabs[k + 1] if k + 1 < len(slabs) else neg, slabs[k])
                     for k in range(depth)]
        return vals

    for h in range(PEER_HEADS):
        q1 = q_ref[:, (2 * h) * half:(2 * h + 1) * half]
        q2 = q_ref[:, (2 * h + 1) * half:(2 * h + 2) * half]
        s1 = lax.dot_general(k1_ref[...], q1, (((1,), (1,)), ((), ())), precision=HIGHEST,
                             preferred_element_type=F32)
        s2 = lax.dot_general(k2_ref[...], q2, (((1,), (1,)), ((), ())), precision=HIGHEST,
                             preferred_element_type=F32)
        a = top_values(s1)
        b = jnp.concatenate(top_values(s2), axis=0)
        b0 = b[0:1]
        cand = [a[i] + b[:PEER_TOPK // (i + 1)] for i in range(PEER_TOPK)]
        n_cand = sum(x.shape[0] for x in cand)
        n_rows = 8 * pl.next_power_of_2(-(-n_cand // 8))
        cand.append(jnp.full((n_rows - n_cand, s1.shape[1]), neg, F32))
        top = top_values(jnp.concatenate(cand, axis=0))
        zsum = top[0] * 0.0
        for m in top:
            zsum = zsum + jnp.exp(m - top[0])
        s1_ref[h] = s1
        s2_ref[h] = s2
        w1_ref[h] = jnp.exp(s1 - a[0]) / zsum
        e2_ref[h] = jnp.exp(s2 - b0)
        tau_ref[h] = top[-1]


def _peer_route(q, key1, key2, tm):
    t_tokens = q.shape[0]
    nk, half = key1.shape
    big = pl.BlockSpec((PEER_HEADS, nk, tm), lambda i: (0, 0, i))
    big_shape = jax.ShapeDtypeStruct((PEER_HEADS, nk, t_tokens), F32)
    return pl.pallas_call(
        _peer_route_kernel,
        grid=(t_tokens // tm,),
        in_specs=[pl.BlockSpec((tm, q.shape[1]), lambda i: (i, 0)),
                  pl.BlockSpec((nk, half), lambda i: (0, 0)),
                  pl.BlockSpec((nk, half), lambda i: (0, 0))],
        out_specs=[big, big, big, big, pl.BlockSpec((PEER_HEADS, 1, tm), lambda i: (0, 0, i))],
        out_shape=[big_shape, big_shape, big_shape, big_shape,
                   jax.ShapeDtypeStruct((PEER_HEADS, 1, t_tokens), F32)],
        compiler_params=_params("parallel"),
        name="peer_route",
    )(q, key1, key2)


def _peer_ffn_kernel(xt_ref, u_ref, v_ref, s1_ref, w1_ref, s2_ref, e2_ref, tau_ref, o_ref, p_ref, h_ref, g_ref,
                     *, ni, n_et):
    step = pl.program_id(1)
    et = jnp.minimum(step, n_et - 1)
    slot = step % 2

    @pl.when(step == 0)
    def _():
        o_ref[...] = jnp.zeros_like(o_ref)
        p_ref[...] = jnp.zeros_like(p_ref)

    nk = s2_ref.shape[1]
    te, tm = h_ref.shape
    d = o_ref.shape[0]
    chunks = [(ii, tc) for ii in range(ni) for tc in range(tm // LANES)]

    def block(ii, tc):
        return slice(ii * nk, (ii + 1) * nk), slice(tc * LANES, (tc + 1) * LANES)

    def gate_chunk(ii, tc):
        rows, lanes = block(ii, tc)
        i = et * ni + ii
        g = None
        for h in range(PEER_HEADS):
            s1_row = s1_ref[h, pl.ds(i, 1), :][:, lanes]
            w1_row = w1_ref[h, pl.ds(i, 1), :][:, lanes]
            hit = (s1_row + s2_ref[h, :, lanes]) >= tau_ref[h, :, lanes]
            term = jnp.where(hit, e2_ref[h, :, lanes] * w1_row, 0.0)
            g = term if g is None else g + term
        g_ref[rows, lanes] = g

    def finish_chunk(ii, tc):
        rows, lanes = block(ii, tc)
        p_ref[slot, rows, lanes] = (g_ref[rows, lanes] * jax.nn.gelu(h_ref[rows, lanes])).astype(BF16)

    quarter = len(chunks) // 4
    half = tm // 2
    for c in chunks[:quarter]:
        gate_chunk(*c)
    h_ref[:, :half] = _dot(u_ref[...], xt_ref[:, :half])
    for c in chunks[quarter:2 * quarter]:
        gate_chunk(*c)
    h_ref[:, half:] = _dot(u_ref[...], xt_ref[:, half:])
    p_prev = p_ref[1 - slot]
    vector_work = [[(gate_chunk, c) for c in chunks[2 * quarter:3 * quarter]],
                   [(gate_chunk, c) for c in chunks[3 * quarter:]],
                   [(finish_chunk, c) for c in chunks[:2 * quarter]],
                   [(finish_chunk, c) for c in chunks[2 * quarter:]]]
    dq = d // len(vector_work)
    for k, work in enumerate(vector_work):
        for fn, c in work:
            fn(*c)
        rows = slice(k * dq, (k + 1) * dq)
        o_ref[rows, :] += lax.dot_general(v_ref[:, rows], p_prev, (((0,), (0,)), ((), ())),
                                          preferred_element_type=F32)


def _peer_ffn(xt_bf, u_tab, v_tab, layer, route, tm, te):
    s1, w1, s2, e2, tau = route
    d, t_tokens = xt_bf.shape
    n_exp = u_tab.shape[1]
    nk = s2.shape[1]
    ni = te // nk
    n_et = n_exp // te
    once = pl.Buffered(1)
    tok = pl.BlockSpec((PEER_HEADS, nk, tm), lambda i, j: (0, 0, i), pipeline_mode=once)
    return pl.pallas_call(
        functools.partial(_peer_ffn_kernel, ni=ni, n_et=n_et),
        grid=(t_tokens // tm, n_et + 1),
        in_specs=[pl.BlockSpec((d, tm), lambda i, j: (0, i), pipeline_mode=once),
                  pl.BlockSpec((None, te, d), lambda i, j: (layer, jnp.minimum(j, n_et - 1), 0)),
                  pl.BlockSpec((None, te, d), lambda i, j: (layer, jnp.maximum(j - 1, 0), 0)),
                  tok, tok, tok, tok,
                  pl.BlockSpec((PEER_HEADS, 1, tm), lambda i, j: (0, 0, i), pipeline_mode=once)],
        out_specs=pl.BlockSpec((d, tm), lambda i, j: (0, i)),
        out_shape=jax.ShapeDtypeStruct((d, t_tokens), F32),
        scratch_shapes=[pltpu.VMEM((2, te, tm), BF16), pltpu.VMEM((te, tm), F32), pltpu.VMEM((te, tm), F32)],
        compiler_params=_params("parallel", "arbitrary"),
        name="peer_ffn",
    )(xt_bf, u_tab, v_tab, s1, w1, s2, e2, tau)


def kernel(x_prompt, x_sample, state_gla, state_s5_re, state_s5_im, ln1_g, ln1_b, w_in, gla_w_a2, gla_b_a, gla_norm_g, s5_a_re, s5_a_im, s5_log_dt, s5_b_re, s5_b_im, s5_c_re, s5_c_im, s5_d, s5_w_glu, s5_b_glu, w_branch, w_out, ln2_g, ln2_b, peer_w_q, peer_key1, peer_key2, peer_u, peer_v):
    depth = w_in.shape[0]
    pb, plen, d = x_prompt.shape
    sb, slen, _ = x_sample.shape
    tp, ts = pb * plen, sb * slen
    t_all = tp + ts
    dk_tot = gla_w_a2.shape[2]
    dv_tot = GLA_HEADS * gla_norm_g.shape[1]
    s5_w = s5_d.shape[1]
    alpha = (2 * depth) ** 0.25
    widths = (dk_tot, dk_tot, dv_tot, dv_tot, GLA_RANK, s5_w, d, d)
    off = [0]
    for wdt in widths:
        off.append(off[-1] + wdt)
    lr0, lr1 = off[4], off[5]
    x = (x_prompt.reshape(tp, d), x_sample.reshape(ts, d))
    x_bf = jnp.concatenate([x[0].astype(BF16), x[1].astype(BF16)], axis=0)
    proj_tile = 4 * LANES
    mm_rows = 1536 if t_all % 1536 == 0 else 1024
    proj_main =off[-1] // proj_tile * proj_tile
    w_in_t = w_in.transpose(0, 2, 1)
    wa2 = jnp.pad(gla_w_a2, ((0, 0), (0, LANES - GLA_RANK), (0, 0)))
    u_bf = peer_u.astype(BF16)
    v_bf = peer_v.astype(BF16)
    cols = (off[0], off[1], off[2], off[3], lr0)
    u_col, zg_col, zs_col = off[5], off[6], off[7]
    gla_p = gla_s = None
    re_p, im_p, re_s, im_s = [], [], [], []
    for l in range(depth):
        proj = _mm(x_bf, w_in_t, l, mm_rows, proj_tile, w_transposed=True, n_out=proj_main + LANES)
        proj = _mm_tail(x_bf, w_in_t, l, proj, 1024, proj_main)

        o_gla, gla_p = _gla(proj, wa2[l], gla_b_a[l], gla_norm_g[l], None, l, gla_p, None, depth=depth, row0=0,
                            n_rows=t_all, nb=pb, seq_len=plen, c=min(GLA_CHUNK, plen), nseq=1, cols=cols)
        o_gla, gla_s = _gla(proj, wa2[l], gla_b_a[l], gla_norm_g[l], state_gla, l, gla_s, o_gla, depth=depth,
                            row0=tp, n_rows=t_all, nb=sb, seq_len=slen, c=min(GLA_CHUNK, slen), nseq=2, cols=cols)

        tables = _s5_tables(s5_a_re[l], s5_a_im[l], s5_log_dt[l], s5_b_re[l], s5_b_im[l], s5_c_re[l], s5_c_im[l],
                            S5_CHUNK)
        y_s5, (r_p, i_p), (r_s, i_s) = _s5_mixer(proj, u_col, tables, s5_d[l], state_s5_re[l], state_s5_im[l],
                                                 pb, plen)
        o_s5 = _glu(y_s5, s5_w_glu, l, s5_b_glu[l], 1024, 1024)

        merged = _branch_merge(o_gla, o_s5, w_branch, l, proj, zg_col, zs_col, 1024, 512)
        mix = _mm(merged, w_out, l, mm_rows, proj_tile)
        x, x_bf, xt_bf = _deepnorm_ln(x, mix, ln1_g[l], ln1_b[l], alpha, 256, emit_transposed=True)

        q = _mm(x_bf, peer_w_q, l, mm_rows, proj_tile)
        route = _peer_route(q, peer_key1[l], peer_key2[l], 256)
        ffn_t = _peer_ffn(xt_bf, u_bf, v_bf, l, route, 512, 512)
        if l + 1 < depth:
            x, x_bf = _deepnorm_ln(x, ffn_t, ln2_g[l], ln2_b[l], alpha, 256, y_transposed=True)
        else:
            y_p, y_s = _deepnorm_ln(x, ffn_t, ln2_g[l], ln2_b[l], alpha, 256, y_transposed=True,
                                    emit_bf16=False, split_out=tp)

        re_p.append(r_p)
        im_p.append(i_p)
        re_s.append(r_s)
        im_s.append(i_s)

    return (y_p.reshape(pb, plen, d), y_s.reshape(sb, slen, d), gla_p, jnp.stack(re_p), jnp.stack(im_p),
            gla_s, jnp.stack(re_s), jnp.stack(im_s))
```

```python
import functools

import jax
import jax.numpy as jnp
from jax import lax
from jax.experimental import pallas as pl
from jax.experimental.pallas import tpu as pltpu

F32 = jnp.float32
BF16 = jnp.bfloat16
HIGHEST = lax.Precision.HIGHEST

V7X_VMEM_BYTES = 64 * 1024 * 1024
VMEM_LIMIT_BYTES = V7X_VMEM_BYTES - 8 * 1024 * 1024
LANES = 128

GLA_HEADS = 4
GLA_RANK = 16
GLA_TAU = 16.0
GLA_CHUNK = 64
RMS_EPS = 1e-6
S5_CHUNK = 8
PEER_HEADS = 8
PEER_TOPK = 16
LN_EPS = 1e-5


def _params(*semantics):
    return pltpu.CompilerParams(dimension_semantics=semantics, vmem_limit_bytes=VMEM_LIMIT_BYTES)


def _dot(a, b):
    return jnp.dot(a, b, preferred_element_type=F32)


def _dot_nt(a, b):
    return lax.dot_general(a, b, (((1,), (1,)), ((), ())), preferred_element_type=F32)


def _mm_kernel(a_ref, b_ref, o_ref):
    o_ref[...] = _dot(a_ref[...], b_ref[...].astype(BF16)).astype(o_ref.dtype)


def _layer_cols(layer, tn, row0=0):
    def spec(rows):
        return pl.BlockSpec((None, rows, tn), lambda i, j: (layer, row0, j))
    return spec


def _mm_nt_kernel(a_ref, bt_ref, o_ref):
    o_ref[...] = _dot_nt(a_ref[...], bt_ref[...].astype(BF16)).astype(o_ref.dtype)


def _mm(a, w, layer, tm, tn, out_dtype=F32, w_transposed=False, n_out=None):
    m, k = a.shape
    n = (w.shape[1] if w_transposed else w.shape[2]) // tn * tn
    w_spec = (pl.BlockSpec((None, tn, k), lambda i, j: (layer, j, 0)) if w_transposed
              else _layer_cols(layer, tn)(k))
    return pl.pallas_call(
        _mm_nt_kernel if w_transposed else _mm_kernel,
        grid=(m // tm, n // tn),
        in_specs=[pl.BlockSpec((tm, k), lambda i, j: (i, 0)), w_spec],
        out_specs=pl.BlockSpec((tm, tn), lambda i, j: (i, j)),
        out_shape=jax.ShapeDtypeStruct((m, n_out or n), out_dtype),
        compiler_params=_params("parallel", "parallel"),
        name="mm",
    )(a, w)


def _mm_tail_kernel(a_ref, bt_ref, prev_ref, o_ref):
    del prev_ref
    tail = _dot_nt(a_ref[...], bt_ref[...].astype(BF16))
    o_ref[...] = jnp.concatenate([tail, jnp.zeros((tail.shape[0], o_ref.shape[1] - tail.shape[1]), F32)], axis=1)


def _mm_tail(a, w_t, layer, out, tm, col0):
    m, k = a.shape
    rows = w_t.shape[1] - col0
    assert 0 < rows < LANES and rows % 8 == 0 and col0 % rows == 0 and col0 % LANES == 0
    return pl.pallas_call(
        _mm_tail_kernel,
        grid=(m // tm,),
        in_specs=[pl.BlockSpec((tm, k), lambda i: (i, 0)),
                  pl.BlockSpec((None, rows, k), lambda i: (layer, col0 // rows, 0)),
                  pl.BlockSpec(memory_space=pl.ANY)],
        out_specs=pl.BlockSpec((tm, LANES), lambda i: (i, col0 // LANES)),
        out_shape=jax.ShapeDtypeStruct(out.shape, out.dtype),
        input_output_aliases={2: 0},
        compiler_params=_params("parallel"),
        name="mm_tail",
    )(a, w_t, out)


def _glu_kernel(y_ref, w_ref, b_ref, ytile_ref, o_ref):
    acc = _dot(y_ref[...].astype(BF16), w_ref[...].astype(BF16)) + b_ref[...]
    o_ref[...] = (ytile_ref[...] * jax.nn.sigmoid(acc)).astype(o_ref.dtype)


def _glu(y, w, layer, b, tm, tn):
    m, k = y.shape
    n = w.shape[2]
    return pl.pallas_call(
        _glu_kernel,
        grid=(m // tm, n // tn),
        in_specs=[pl.BlockSpec((tm, k), lambda i, j: (i, 0)),
                  _layer_cols(layer, tn)(k),
                  pl.BlockSpec((1, tn), lambda i, j: (0, j)),
                  pl.BlockSpec((tm, tn), lambda i, j: (i, j))],
        out_specs=pl.BlockSpec((tm, tn), lambda i, j: (i, j)),
        out_shape=jax.ShapeDtypeStruct((m, n), BF16),
        compiler_params=_params("parallel", "parallel"),
        name="s5_glu",
    )(y, w, b.reshape(1, n), y)


def _lane_shifted(a, b, shift):
    if shift == 0:
        return a
    return jnp.concatenate([a[:, shift:], b[:, :shift]], axis=1)


def _branch_kernel(og_ref, os_ref, w1_ref, w2_ref, zg_ref, zg_next_ref, zs_ref, zs_next_ref, o_ref, *, shift):
    m1 = _dot(og_ref[...], w1_ref[...].astype(BF16))
    m2 = _dot(os_ref[...], w2_ref[...].astype(BF16))
    zg = _lane_shifted(zg_ref[...], zg_next_ref[...], shift)
    zs = _lane_shifted(zs_ref[...], zs_next_ref[...], shift)
    o_ref[...] = (jax.nn.sigmoid(zg) * m1 + jax.nn.sigmoid(zs) * m2).astype(o_ref.dtype)


def _branch_merge(o_gla, o_s5, w, layer, proj, zg_col, zs_col, tm, tn):
    m, k = o_gla.shape
    assert o_s5.shape[1] == k and w.shape[1] == 2 * k
    n = w.shape[2]
    shift = zg_col % LANES
    assert zs_col % LANES == shift and (zg_col - shift) % tn == 0 and (zs_col - shift) % tn == 0

    def gate_specs(col):
        blk, nxt, per = (col - shift) // tn, (col - shift) // LANES, tn // LANES
        return [pl.BlockSpec((tm, tn), lambda i, j: (i, blk + j)),
                pl.BlockSpec((tm, LANES), lambda i, j: (i, nxt + per * (j + 1)))]

    return pl.pallas_call(
        functools.partial(_branch_kernel, shift=shift),
        grid=(m // tm, n // tn),
        in_specs=[pl.BlockSpec((tm, k), lambda i, j: (i, 0)),
                  pl.BlockSpec((tm, k), lambda i, j: (i, 0)),
                  _layer_cols(layer, tn, 0)(k),
                  _layer_cols(layer, tn, 1)(k)] + gate_specs(zg_col) + gate_specs(zs_col),
        out_specs=pl.BlockSpec((tm, tn), lambda i, j: (i, j)),
        out_shape=jax.ShapeDtypeStruct((m, n), BF16),
        compiler_params=_params("parallel", "parallel"),
        name="branch_merge",
    )(o_gla, o_s5, w, w, proj, proj, proj, proj)


def _ln_kernel(*refs, alpha, y_transposed, n_head, two_in, two_out, emit_bf16, emit_transposed):
    refs = list(refs)
    step = pl.program_id(0)
    if two_in:
        x = jnp.where(step < n_head, refs[0][...], refs[1][...])
        refs = refs[2:]
    else:
        x = refs[0][...]
        refs = refs[1:]
    y_ref, g_ref, b_ref = refs[:3]
    outs = refs[3:]
    y = y_ref[...].T if y_transposed else y_ref[...]
    h = alpha * x + y
    mu = jnp.mean(h, axis=-1, keepdims=True)
    d = h - mu
    var = jnp.mean(d * d, axis=-1, keepdims=True)
    o = d * lax.rsqrt(var + LN_EPS) * g_ref[...] + b_ref[...]
    if two_out:
        head_ref, tail_ref = outs[:2]
        outs = outs[2:]

        @pl.when(step < n_head)
        def _():
            head_ref[...] = o

        @pl.when(step >= n_head)
        def _():
            tail_ref[...] = o
    else:
        outs[0][...] = o
        outs = outs[1:]
    if emit_bf16:
        outs[0][...] = o.astype(BF16)
        outs = outs[1:]
    if emit_transposed:
        outs[0][...] = o.T.astype(BF16)


def _deepnorm_ln(x, y, g, b, alpha, tm, y_transposed=False, emit_bf16=True, emit_transposed=False, split_out=None):
    xs = x if isinstance(x, (tuple, list)) else (x,)
    d = xs[0].shape[1]
    m = sum(a.shape[0] for a in xs)
    two_in, two_out = len(xs) == 2, split_out is not None
    n_head = (xs[0].shape[0] if two_in else split_out if two_out else m) // tm
    assert not (two_in and two_out) or xs[0].shape[0] == split_out
    row = pl.BlockSpec((tm, d), lambda i: (i, 0))
    head = pl.BlockSpec((tm, d), lambda i: (jnp.minimum(i, n_head - 1), 0))
    tail = pl.BlockSpec((tm, d), lambda i: (jnp.maximum(i - n_head, 0), 0))
    col = pl.BlockSpec((d, tm), lambda i: (0, i))
    vec = pl.BlockSpec((1, d), lambda i: (0, 0))
    out_specs = [head, tail] if two_out else [row]
    out_shape = ([jax.ShapeDtypeStruct((split_out, d), F32), jax.ShapeDtypeStruct((m - split_out, d), F32)]
                 if two_out else [jax.ShapeDtypeStruct((m, d), F32)])
    if emit_bf16:
        out_specs.append(row)
        out_shape.append(jax.ShapeDtypeStruct((m, d), BF16))
    if emit_transposed:
        out_specs.append(col)
        out_shape.append(jax.ShapeDtypeStruct((d, m), BF16))
    kern = functools.partial(_ln_kernel, alpha=alpha, y_transposed=y_transposed, n_head=n_head, two_in=two_in,
                             two_out=two_out, emit_bf16=emit_bf16, emit_transposed=emit_transposed)
    return pl.pallas_call(
        kern,
        grid=(m // tm,),
        in_specs=([head, tail] if two_in else [row]) + [col if y_transposed else row, vec, vec],
        out_specs=out_specs,
        out_shape=out_shape,
        compiler_params=_params("arbitrary"),
        name="deepnorm_ln",
    )(*xs, y, g.reshape(1, d), b.reshape(1, d))


def _gla_kernel(*refs, c, nseq, heads, scale, has_init, n_prev):
    refs = list(refs)
    q_ref, k_ref, v_ref, r_ref, alr_ref, wa2_ref, ba_ref, g_ref = refs[:8]
    rest = refs[8:]
    s0_ref = rest.pop(0) if has_init else None
    o_ref, s_ref = rest[n_prev:]
    dk = q_ref.shape[-1] // heads
    dv = v_ref.shape[-1] // heads
    kpad = max(c, LANES)
    qpad = max(c, 16)

    @pl.when(pl.program_id(1) == 0)
    def _():
        if has_init:
            s_ref[...] = s0_ref[...]
        else:
            s_ref[...] = jnp.zeros_like(s_ref)

    row = lax.broadcasted_iota(jnp.int32, (c, c), 0)
    col = lax.broadcasted_iota(jnp.int32, (c, c), 1)
    tri_cc = (row >= col).astype(F32)
    rowp = lax.broadcasted_iota(jnp.int32, (qpad, kpad), 0)
    colp = lax.broadcasted_iota(jnp.int32, (qpad, kpad), 1)
    causal = rowp >= colp
    eye = lax.broadcasted_iota(jnp.int32, (dk, dk), 0) == lax.broadcasted_iota(jnp.int32, (dk, dk), 1)

    def pad_rows(x, rows):
        if x.shape[0] == rows:
            return x
        return jnp.concatenate([x, jnp.zeros((rows - x.shape[0], x.shape[1]), x.dtype)], axis=0)

    outs = []
    for s in range(nseq):
        rows = slice(s * c, (s + 1) * c)
        z = jnp.dot(alr_ref[rows, :], wa2_ref[...], precision=HIGHEST, preferred_element_type=F32) + ba_ref[...]
        log_a = (jnp.minimum(z, 0.0) - jnp.log1p(jnp.exp(-jnp.abs(z)))) * (1.0 / GLA_TAU)
        if c >= LANES // 2:
            b = jnp.dot(tri_cc, log_a, precision=HIGHEST, preferred_element_type=F32)
        else:
            ridx = lax.broadcasted_iota(jnp.int32, log_a.shape, 0)
            b = jnp.zeros_like(log_a)
            for t in range(c):
                b = b + jnp.where(ridx >= t, log_a[t:t + 1, :], 0.0)
        b_last = b[c - 1:c, :]
        k = k_ref[rows, :]
        q_dec_all = (q_ref[rows, :] * scale * jnp.exp(b)).astype(BF16)
        k_inv_all = (k * jnp.exp(-b)).astype(BF16)
        k_end_all = (k * jnp.exp(b_last - b)).astype(BF16)
        v_all = v_ref[rows, :].astype(BF16)
        r = r_ref[rows, :]
        gate = r * jax.nn.sigmoid(r)
        o_heads = []
        for h in range(heads):
            kc = slice(h * dk, (h + 1) * dk)
            vc = slice(h * dv, (h + 1) * dv)
            q_dec = pad_rows(q_dec_all[:, kc], qpad)
            k_inv = pad_rows(k_inv_all[:, kc], kpad)
            k_end = pad_rows(k_end_all[:, kc], kpad)
            v_bf = pad_rows(v_all[:, vc], kpad)
            scores = jnp.where(causal, _dot_nt(q_dec, k_inv), 0.0).astype(BF16)
            state = s_ref[s, h]
            o = (_dot(q_dec, state.astype(BF16)) + _dot(scores, v_bf))[:c]
            decay_col = jnp.exp(jnp.sum(jnp.where(eye, b_last[:, kc], 0.0), axis=1, keepdims=True))
            s_ref[s, h] = decay_col * state + lax.dot_general(k_end, v_bf, (((0,), (0,)), ((), ())),
                                                              preferred_element_type=F32)
            o = o * lax.rsqrt(jnp.mean(o * o, axis=-1, keepdims=True) + RMS_EPS) * g_ref[...]
            o_heads.append(o * gate[:, vc])
        outs.append(jnp.concatenate(o_heads, axis=1))
    o_all = outs[0] if nseq == 1 else jnp.concatenate(outs, axis=0)
    o_ref[...] = o_all.astype(o_ref.dtype)


def _gla(proj, wa2, ba, norm_g, s0, layer, s_prev, o_prev, *, depth, row0, n_rows, nb, seq_len, c, nseq, cols):
    heads = GLA_HEADS
    dk_tot = wa2.shape[1]
    dk = dk_tot // heads
    dv = norm_g.shape[0]
    dv_tot = heads * dv
    nchunk = seq_len // c
    assert nseq == 1 or nchunk == 1
    rb = nseq * c
    blk0 = row0 // rb
    q_col, k_col, v_col, r_col, lr_col = cols

    def rows_map(col_blk):
        return lambda b, j: (blk0 + b * nchunk + j, col_blk)

    in_specs = [
        pl.BlockSpec((rb, dk_tot), rows_map(q_col // dk_tot)),
        pl.BlockSpec((rb, dk_tot), rows_map(k_col // dk_tot)),
        pl.BlockSpec((rb, dv_tot), rows_map(v_col // dv_tot)),
        pl.BlockSpec((rb, dv_tot), rows_map(r_col // dv_tot)),
        pl.BlockSpec((rb, LANES), rows_map(lr_col // LANES)),
        pl.BlockSpec((LANES, dk_tot), lambda b, j: (0, 0)),
        pl.BlockSpec((1, dk_tot), lambda b, j: (0, 0)),
        pl.BlockSpec((1, dv), lambda b, j: (0, 0)),
    ]
    args = [proj, proj, proj, proj, proj, wa2, ba.reshape(1, -1), norm_g.reshape(1, dv)]
    state_spec = pl.BlockSpec((None, nseq, heads, dk, dv), lambda b, j: (layer, b, 0, 0, 0))
    if s0 is not None:
        in_specs.append(state_spec)
        args.append(s0)
    aliases = {}
    for prev, out_idx in ((s_prev, 1), (o_prev, 0)):
        if prev is not None:
            aliases[len(args)] = out_idx
            in_specs.append(pl.BlockSpec(memory_space=pl.ANY))
            args.append(prev)
    kern = functools.partial(_gla_kernel, c=c, nseq=nseq, heads=heads, scale=dk ** -0.5,
                             has_init=s0 is not None, n_prev=len(aliases))
    return pl.pallas_call(
        kern,
        grid=(nb // nseq, nchunk),
        in_specs=in_specs,
        out_specs=[pl.BlockSpec((rb, dv_tot), rows_map(0)), state_spec],
        out_shape=[jax.ShapeDtypeStruct((n_rows, dv_tot), BF16),
                   jax.ShapeDtypeStruct((depth, nb, heads, dk, dv), F32)],
        input_output_aliases=aliases,
        compiler_params=_params("parallel", "arbitrary"),
        name="gla",
    )(*args)


def _s5_tables(a_re, a_im, log_dt, b_re, b_im, c_re, c_im, c):
    dt = jnp.exp(log_dt)[:, None]
    mag = jnp.exp(a_re * dt)
    ar, ai = mag * jnp.cos(a_im * dt), mag * jnp.sin(a_im * dt)
    den = a_re * a_re + a_im * a_im
    cr = ((ar - 1.0) * a_re + ai * a_im) / den
    ci = (ai * a_re - (ar - 1.0) * a_im) / den
    bbr = cr[..., None] * b_re - ci[..., None] * b_im
    bbi = cr[..., None] * b_im + ci[..., None] * b_re
    tau = jnp.arange(c + 1, dtype=F32)[:, None, None]
    pmag = jnp.exp(tau * (a_re * dt))
    pw_r, pw_i = pmag * jnp.cos(tau * (a_im * dt)), pmag * jnp.sin(tau * (a_im * dt))
    ca_r = c_re[None] * pw_r[:, :, None, :] - c_im[None] * pw_i[:, :, None, :]
    ca_i = c_re[None] * pw_i[:, :, None, :] + c_im[None] * pw_r[:, :, None, :]
    kern = (jnp.einsum('tgmp,gpn->tgmn', ca_r[:c], bbr, precision=HIGHEST)
            - jnp.einsum('tgmp,gpn->tgmn', ca_i[:c], bbi, precision=HIGHEST))
    g, m, n = kern.shape[1:]
    p = a_re.shape[1]
    gpb = LANES // n
    nb = g // gpb

    def block_diag(x):
        lead, _, r, w = x.shape
        copies = jnp.tile(jnp.eye(w, dtype=F32), (1, gpb))
        wide = jnp.dot(x.reshape(-1, w), copies, precision=HIGHEST).reshape(lead, nb, gpb * r, gpb * w)
        return wide * jnp.kron(jnp.eye(gpb, dtype=F32), jnp.ones((r, w), F32))

    lag_blocks = block_diag(kern.transpose(0, 1, 3, 2))
    zero_block = jnp.zeros_like(lag_blocks[0])
    toep = jnp.concatenate(
        [jnp.concatenate([lag_blocks[t - s] if t >= s else zero_block for s in range(c)], axis=1)
         for t in range(c)], axis=2)
    back = (c - 1.0) - tau[:c]
    rev_mag = jnp.exp(back * (a_re * dt))
    rev_r, rev_i = rev_mag * jnp.cos(back * (a_im * dt)), rev_mag * jnp.sin(back * (a_im * dt))
    inj_r = rev_r[..., None] * bbr[None] - rev_i[..., None] * bbi[None]
    inj_i = rev_r[..., None] * bbi[None] + rev_i[..., None] * bbr[None]
    inj = jnp.concatenate([block_diag(x.transpose(0, 1, 3, 2)) for x in (inj_r, inj_i)], axis=3)
    inj = inj.transpose(1, 0, 2, 3).reshape(nb, c * LANES, 2 * gpb * p)
    read = jnp.concatenate(
        [jnp.concatenate(list(block_diag(x.transpose(0, 1, 3, 2))), axis=2)
         for x in (ca_r[1:], -ca_i[1:])], axis=1)
    step_r, step_i = pw_r[c].reshape(nb, 1, gpb * p), pw_i[c].reshape(nb, 1, gpb * p)
    step_same = jnp.concatenate([step_r, step_r], axis=-1)
    step_swap = jnp.concatenate([-step_i, step_i], axis=-1)
    return toep.astype(BF16), inj.astype(BF16), read.astype(BF16), step_same, step_swap


def _chunk_tokens(u_ref, u_next_ref, c, shift):
    r = u_ref.shape[0] // c
    return [_lane_shifted(u_ref[pl.ds(s, r, stride=c), :], u_next_ref[pl.ds(s, r, stride=c), :], shift)
            for s in range(c)]


def _chunk_rows(tokens):
    return jnp.concatenate([x.astype(BF16) for x in tokens], axis=1)


def _u_specs(t_tokens, u_col):
    blk = u_col // LANES
    return [pl.BlockSpec((t_tokens, LANES), lambda b: (0, blk + b)),
            pl.BlockSpec((t_tokens, LANES), lambda b: (0, blk + b + 1))]


def _s5_inject_kernel(u_ref, u_next_ref, w_ref, e_ref, *, c, shift):
    e_ref[...] = _dot(_chunk_rows(_chunk_tokens(u_ref, u_next_ref, c, shift)), w_ref[...])


def _s5_inject(proj, u_col, inj, c):
    t_tokens = proj.shape[0]
    nb, k, w = inj.shape
    return pl.pallas_call(
        functools.partial(_s5_inject_kernel, c=c, shift=u_col % LANES),
        grid=(nb,),
        in_specs=_u_specs(t_tokens, u_col) + [pl.BlockSpec((None, k, w), lambda b: (b, 0, 0))],
        out_specs=pl.BlockSpec((None, t_tokens // c, w), lambda b: (b, 0, 0)),
        out_shape=jax.ShapeDtypeStruct((nb, t_tokens // c, w), F32),
        compiler_params=_params("parallel"),
        name="s5_inject",
    )(proj, proj, inj)


def _s5_scan_kernel(e_ref, x0_ref, same_ref, swap_ref, xprev_ref, xfin_p_ref, xfin_s_ref, *, n_seq, n_chunk):
    half = e_ref.shape[-1] // 2
    same = same_ref[...]
    swap = swap_ref[...]

    def advance(x, e):
        x_swapped = jnp.concatenate([x[:, half:], x[:, :half]], axis=1)
        return same * x + swap * x_swapped + e

    def body(j, xs):
        new = []
        for q, x in enumerate(xs):
            row = q * n_chunk + j
            xprev_ref[pl.ds(row, 1), :] = x
            new.append(advance(x, e_ref[pl.ds(row, 1), :]))
        return tuple(new)

    zero = jnp.zeros((1, e_ref.shape[-1]), F32)
    xs = lax.fori_loop(0, n_chunk, body, (zero,) * n_seq)
    xfin_p_ref[...] = jnp.zeros_like(xfin_p_ref)
    for q, x in enumerate(xs):
        xfin_p_ref[q:q + 1, :] = x
    rp = n_seq * n_chunk
    x0 = x0_ref[...]
    xprev_ref[rp:, :] = x0
    xfin_s_ref[...] = advance(x0, e_ref[rp:, :])


def _s5_scan(e, x0_s, same, swap, n_seq, n_chunk):
    nb, r, w = e.shape
    ns = x0_s.shape[1]
    assert r == n_seq * n_chunk + ns
    pad_seq = -(-n_seq // 8) * 8
    vec = pl.BlockSpec((None, 1, w), lambda b: (b, 0, 0))
    return pl.pallas_call(
        functools.partial(_s5_scan_kernel, n_seq=n_seq, n_chunk=n_chunk),
        grid=(nb,),
        in_specs=[pl.BlockSpec((None, r, w), lambda b: (b, 0, 0)),
                  pl.BlockSpec((None, ns, w), lambda b: (b, 0, 0)), vec, vec],
        out_specs=[pl.BlockSpec((None, r, w), lambda b: (b, 0, 0)),
                   pl.BlockSpec((None, pad_seq, w), lambda b: (b, 0, 0)),
                   pl.BlockSpec((None, ns, w), lambda b: (b, 0, 0))],
        out_shape=[jax.ShapeDtypeStruct((nb, r, w), F32), jax.ShapeDtypeStruct((nb, pad_seq, w), F32),
                   jax.ShapeDtypeStruct((nb, ns, w), F32)],
        compiler_params=_params("parallel"),
        name="s5_scan",
    )(e, x0_s, same, swap)


def _s5_out_kernel(u_ref, u_next_ref, xp_ref, toep_ref, read_ref, d_ref, y_ref, *, c, shift):
    r = u_ref.shape[0] // c
    tokens = _chunk_tokens(u_ref, u_next_ref, c, shift)
    y = _dot(_chunk_rows(tokens), toep_ref[...]) + _dot(xp_ref[...].astype(BF16), read_ref[...])
    for t in range(c):
        y_ref[pl.ds(t, r, stride=c), :] = jax.nn.gelu(y[:, t * LANES:(t + 1) * LANES] + d_ref[...] * tokens[t])


def _s5_out(proj, u_col, xprev, toep, read, d_skip, c):
    t_tokens = proj.shape[0]
    nb, r, w = xprev.shape
    k = toep.shape[1]
    return pl.pallas_call(
        functools.partial(_s5_out_kernel, c=c, shift=u_col % LANES),
        grid=(nb,),
        in_specs=_u_specs(t_tokens, u_col) + [
            pl.BlockSpec((None, r, w), lambda b: (b, 0, 0)),
            pl.BlockSpec((None, k, k), lambda b: (b, 0, 0)),
            pl.BlockSpec((None, w, k), lambda b: (b, 0, 0)),
            pl.BlockSpec((1, LANES), lambda b: (0, b))],
        out_specs=pl.BlockSpec((t_tokens, LANES), lambda b: (0, b)),
        out_shape=jax.ShapeDtypeStruct((t_tokens, nb * LANES), F32),
        compiler_params=_params("parallel"),
        name="s5_out",
    )(proj, proj, xprev, toep, read, d_skip.reshape(1, -1))


def _s5_mixer(proj, u_col, tables, d_skip, st_re, st_im, n_prompt_seq, prompt_len):
    toep, inj, read, step_same, step_swap = tables
    c = S5_CHUNK
    nb = toep.shape[0]
    ns, groups, p = st_re.shape
    gpb = groups // nb

    def to_blocks(st):
        return st.reshape(ns, nb, gpb * p).transpose(1, 0, 2)

    def from_blocks(x):
        return x.transpose(1, 0, 2).reshape(x.shape[1], groups, p)

    e = _s5_inject(proj, u_col, inj, c)
    x0_s = jnp.concatenate([to_blocks(st_re), to_blocks(st_im)], axis=-1)
    xprev, xfin_p, xfin_s = _s5_scan(e, x0_s, step_same, step_swap, n_prompt_seq, prompt_len // c)
    y = _s5_out(proj, u_col, xprev, toep, read, d_skip, c)
    half = gpb * p
    xfin_p = xfin_p[:, :n_prompt_seq]
    return (y, (from_blocks(xfin_p[..., :half]), from_blocks(xfin_p[..., half:])),
            (from_blocks(xfin_s[..., :half]), from_blocks(xfin_s[..., half:])))


def _sorting_network(n):
    pairs = []
    p = 1
    while p < n:
        k = p
        while k >= 1:
            for j in range(k % p, n - k, 2 * k):
                for i in range(min(k, n - j - k)):
                    if (i + j) // (2 * p) == (i + j + k) // (2 * p):
                        pairs.append((i + j, i + j + k))
            k //= 2
        p *= 2
    return pairs


def _peer_route_kernel(q_ref, k1_ref, k2_ref, s1_ref, w1_ref, s2_ref, e2_ref, tau_ref):
    half = k1_ref.shape[1]
    neg = jnp.float32(-jnp.inf)

    def top_values(x):
        slabs = [x[r:r + 8] for r in range(0, x.shape[0], 8)]
        for lo, hi in _sorting_network(len(slabs)):
            slabs[lo], slabs[hi] = jnp.maximum(slabs[lo], slabs[hi]), jnp.minimum(slabs[lo], slabs[hi])
        stack_id = lax.broadcasted_iota(jnp.int32, slabs[0].shape, 0)
        vals = []
        for taken in range(PEER_TOPK):
            m = jnp.max(slabs[0], axis=0, keepdims=True)
            vals.append(m)
            depth = min(len(slabs), PEER_TOPK - taken - 1)
            first = jnp.min(jnp.where(slabs[0] == m, stack_id, 8), axis=0, keepdims=True)
            popped = stack_id == first
            slabs = [jnp.where(popped, slabs[k + 1] if k + 1 < len(slabs) else neg, slabs[k])
                     for k in range(depth)]
        return vals

    for h in range(PEER_HEADS):
        q1 = q_ref[:, (2 * h) * half:(2 * h + 1) * half]
        q2 = q_ref[:, (2 * h + 1) * half:(2 * h + 2) * half]
        s1 = lax.dot_general(k1_ref[...], q1, (((1,), (1,)), ((), ())), precision=HIGHEST,
                             preferred_element_type=F32)
        s2 = lax.dot_general(k2_ref[...], q2, (((1,), (1,)), ((), ())), precision=HIGHEST,
                             preferred_element_type=F32)
        a = top_values(s1)
        b = jnp.concatenate(top_values(s2), axis=0)
        b0 = b[0:1]
        cand = [a[i] + b[:PEER_TOPK // (i + 1)] for i in range(PEER_TOPK)]
        n_cand = sum(x.shape[0] for x in cand)
        n_rows = 8 * pl.next_power_of_2(-(-n_cand // 8))
        cand.append(jnp.full((n_rows - n_cand, s1.shape[1]), neg, F32))
        top = top_values(jnp.concatenate(cand, axis=0))
        zsum = top[0] * 0.0
        for m in top:
            zsum = zsum + jnp.exp(m - top[0])
        s1_ref[h] = s1
        s2_ref[h] = s2
        w1_ref[h] = jnp.exp(s1 - a[0]) / zsum
        e2_ref[h] = jnp.exp(s2 - b0)
        tau_ref[h] = top[-1]


def _peer_route(q, key1, key2, tm):
    t_tokens = q.shape[0]
    nk, half = key1.shape
    big = pl.BlockSpec((PEER_HEADS, nk, tm), lambda i: (0, 0, i))
    big_shape = jax.ShapeDtypeStruct((PEER_HEADS, nk, t_tokens), F32)
    return pl.pallas_call(
        _peer_route_kernel,
        grid=(t_tokens // tm,),
        in_specs=[pl.BlockSpec((tm, q.shape[1]), lambda i: (i, 0)),
                  pl.BlockSpec((nk, half), lambda i: (0, 0)),
                  pl.BlockSpec((nk, half), lambda i: (0, 0))],
        out_specs=[big, big, big, big, pl.BlockSpec((PEER_HEADS, 1, tm), lambda i: (0, 0, i))],
        out_shape=[big_shape, big_shape, big_shape, big_shape,
                   jax.ShapeDtypeStruct((PEER_HEADS, 1, t_tokens), F32)],
        compiler_params=_params("parallel"),
        name="peer_route",
    )(q, key1, key2)


def _peer_ffn_kernel(xt_ref, u_ref, v_ref, s1_ref, w1_ref, s2_ref, e2_ref, tau_ref, o_ref, p_ref, h_ref, g_ref,
                     *, ni, n_et):
    step = pl.program_id(1)
    et = jnp.minimum(step, n_et - 1)
    slot = step % 2

    @pl.when(step == 0)
    def _():
        o_ref[...] = jnp.zeros_like(o_ref)
        p_ref[...] = jnp.zeros_like(p_ref)

    nk = s2_ref.shape[1]
    te, tm = h_ref.shape
    d = o_ref.shape[0]
    chunks = [(ii, tc) for ii in range(ni) for tc in range(tm // LANES)]

    def block(ii, tc):
        return slice(ii * nk, (ii + 1) * nk), slice(tc * LANES, (tc + 1) * LANES)

    def gate_chunk(ii, tc):
        rows, lanes = block(ii, tc)
        i = et * ni + ii
        g = None
        for h in range(PEER_HEADS):
            s1_row = s1_ref[h, pl.ds(i, 1), :][:, lanes]
            w1_row = w1_ref[h, pl.ds(i, 1), :][:, lanes]
            hit = (s1_row + s2_ref[h, :, lanes]) >= tau_ref[h, :, lanes]
            term = jnp.where(hit, e2_ref[h, :, lanes] * w1_row, 0.0)
            g = term if g is None else g + term
        g_ref[rows, lanes] = g

    def finish_chunk(ii, tc):
        rows, lanes = block(ii, tc)
        p_ref[slot, rows, lanes] = (g_ref[rows, lanes] * jax.nn.gelu(h_ref[rows, lanes])).astype(BF16)

    quarter = len(chunks) // 4
    half = tm // 2
    for c in chunks[:quarter]:
        gate_chunk(*c)
    h_ref[:, :half] = _dot(u_ref[...], xt_ref[:, :half])
    for c in chunks[quarter:2 * quarter]:
        gate_chunk(*c)
    h_ref[:, half:] = _dot(u_ref[...], xt_ref[:, half:])
    p_prev = p_ref[1 - slot]
    vector_work = [[(gate_chunk, c) for c in chunks[2 * quarter:3 * quarter]],
                   [(gate_chunk, c) for c in chunks[3 * quarter:]],
                   [(finish_chunk, c) for c in chunks[:2 * quarter]],
                   [(finish_chunk, c) for c in chunks[2 * quarter:]]]
    dq = d // len(vector_work)
    for k, work in enumerate(vector_work):
        for fn, c in work:
            fn(*c)
        rows = slice(k * dq, (k + 1) * dq)
        o_ref[rows, :] += lax.dot_general(v_ref[:, rows], p_prev, (((0,), (0,)), ((), ())),
                                          preferred_element_type=F32)


def _peer_ffn(xt_bf, u_tab, v_tab, layer, route, tm, te):
    s1, w1, s2, e2, tau = route
    d, t_tokens = xt_bf.shape
    n_exp = u_tab.shape[1]
    nk = s2.shape[1]
    ni = te // nk
    n_et = n_exp // te
    once = pl.Buffered(1)
    tok = pl.BlockSpec((PEER_HEADS, nk, tm), lambda i, j: (0, 0, i), pipeline_mode=once)
    return pl.pallas_call(
        functools.partial(_peer_ffn_kernel, ni=ni, n_et=n_et),
        grid=(t_tokens // tm, n_et + 1),
        in_specs=[pl.BlockSpec((d, tm), lambda i, j: (0, i)),
                  pl.BlockSpec((None, te, d), lambda i, j: (layer, jnp.minimum(j, n_et - 1), 0)),
                  pl.BlockSpec((None, te, d), lambda i, j: (layer, jnp.maximum(j - 1, 0), 0)),
                  tok, tok, tok, tok,
                  pl.BlockSpec((PEER_HEADS, 1, tm), lambda i, j: (0, 0, i), pipeline_mode=once)],
        out_specs=pl.BlockSpec((d, tm), lambda i, j: (0, i)),
        out_shape=jax.ShapeDtypeStruct((d, t_tokens), F32),
        scratch_shapes=[pltpu.VMEM((2, te, tm), BF16), pltpu.VMEM((te, tm), F32), pltpu.VMEM((te, tm), F32)],
        compiler_params=_params("parallel", "arbitrary"),
        name="peer_ffn",
    )(xt_bf, u_tab, v_tab, s1, w1, s2, e2, tau)


def kernel(x_prompt, x_sample, state_gla, state_s5_re, state_s5_im, ln1_g, ln1_b, w_in, gla_w_a2, gla_b_a, gla_norm_g, s5_a_re, s5_a_im, s5_log_dt, s5_b_re, s5_b_im, s5_c_re, s5_c_im, s5_d, s5_w_glu, s5_b_glu, w_branch, w_out, ln2_g, ln2_b, peer_w_q, peer_key1, peer_key2, peer_u, peer_v):
    depth = w_in.shape[0]
    pb, plen, d = x_prompt.shape
    sb, slen, _ = x_sample.shape
    tp, ts = pb * plen, sb * slen
    t_all = tp + ts
    dk_tot = gla_w_a2.shape[2]
    dv_tot = GLA_HEADS * gla_norm_g.shape[1]
    s5_w = s5_d.shape[1]
    alpha = (2 * depth) ** 0.25
    widths = (dk_tot, dk_tot, dv_tot, dv_tot, GLA_RANK, s5_w, d, d)
    off = [0]
    for wdt in widths:
        off.append(off[-1] + wdt)
    lr0, lr1 = off[4], off[5]
    x = (x_prompt.reshape(tp, d), x_sample.reshape(ts, d))
    x_bf = jnp.concatenate([x[0].astype(BF16), x[1].astype(BF16)], axis=0)
    proj_tile = 4 * LANES
    mm_rows = 1536 if t_all % 1536 == 0 else 1024
    proj_main =off[-1] // proj_tile * proj_tile
    w_in_t = w_in.transpose(0, 2, 1)
    wa2 = jnp.pad(gla_w_a2, ((0, 0), (0, LANES - GLA_RANK), (0, 0)))
    u_bf = peer_u.astype(BF16)
    v_bf = peer_v.astype(BF16)
    cols = (off[0], off[1], off[2], off[3], lr0)
    u_col, zg_col, zs_col = off[5], off[6], off[7]
    gla_p = gla_s = None
    re_p, im_p, re_s, im_s = [], [], [], []
    for l in range(depth):
        proj = _mm(x_bf, w_in_t, l, mm_rows, proj_tile, w_transposed=True, n_out=proj_main + LANES)
        proj = _mm_tail(x_bf, w_in_t, l, proj, 1024, proj_main)

        o_gla, gla_p = _gla(proj, wa2[l], gla_b_a[l], gla_norm_g[l], None, l, gla_p, None, depth=depth, row0=0,
                            n_rows=t_all, nb=pb, seq_len=plen, c=min(GLA_CHUNK, plen), nseq=1, cols=cols)
        o_gla, gla_s = _gla(proj, wa2[l], gla_b_a[l], gla_norm_g[l], state_gla, l, gla_s, o_gla, depth=depth,
                            row0=tp, n_rows=t_all, nb=sb, seq_len=slen, c=min(GLA_CHUNK, slen), nseq=2, cols=cols)

        tables = _s5_tables(s5_a_re[l], s5_a_im[l], s5_log_dt[l], s5_b_re[l], s5_b_im[l], s5_c_re[l], s5_c_im[l],
                            S5_CHUNK)
        y_s5, (r_p, i_p), (r_s, i_s) = _s5_mixer(proj, u_col, tables, s5_d[l], state_s5_re[l], state_s5_im[l],
                                                 pb, plen)
        o_s5 = _glu(y_s5, s5_w_glu, l, s5_b_glu[l], 1024, 1024)

        merged = _branch_merge(o_gla, o_s5, w_branch, l, proj, zg_col, zs_col, 1024, 512)
        mix = _mm(merged, w_out, l, mm_rows, proj_tile)
        x, x_bf, xt_bf = _deepnorm_ln(x, mix, ln1_g[l], ln1_b[l], alpha, 256, emit_transposed=True)

        q = _mm(x_bf, peer_w_q, l, mm_rows, proj_tile)
        route = _peer_route(q, peer_key1[l], peer_key2[l], 256)
        ffn_t = _peer_ffn(xt_bf, u_bf, v_bf, l, route, 512, 512)
        if l + 1 < depth:
            x, x_bf = _deepnorm_ln(x, ffn_t, ln2_g[l], ln2_b[l], alpha, 256, y_transposed=True)
        else:
            y_p, y_s = _deepnorm_ln(x, ffn_t, ln2_g[l], ln2_b[l], alpha, 256, y_transposed=True,
                                    emit_bf16=False, split_out=tp)

        re_p.append(r_p)
        im_p.append(i_p)
        re_s.append(r_s)
        im_s.append(i_s)

    return (y_p.reshape(pb, plen, d), y_s.reshape(sb, slen, d), gla_p, jnp.stack(re_p), jnp.stack(im_p),
            gla_s, jnp.stack(re_s), jnp.stack(im_s))
```
